```python
import math
import jax, jax.numpy as jnp
from jax import lax
import numpy as np

D_MODEL = 2048
BATCH = 16
SEQ = 256
DEPTH = 2
DEC_BATCH = 2
DEC_SEQ = 4096
PAST_LEN = 256

GRID_W = 64
EPS = 1e-6
F32 = jnp.float32
QBLK = 128
GROUP_W = D_MODEL // 4
D_MIX = 4 * GROUP_W
DIFF_QK = 64
DIFF_VD = 2 * DIFF_QK
DIFF_HEADS = GROUP_W // DIFF_VD
DIFF_SCALE = DIFF_QK ** -0.5
HEAD_DIM = 64
SWA_HEADS = GROUP_W // HEAD_DIM
SWA_KV_HEADS = SWA_HEADS // 4
SWA_GROUP = SWA_HEADS // SWA_KV_HEADS
SWA_SCALE = HEAD_DIM ** -0.5
WINDOW = 128
ROPE_PAIRS = HEAD_DIM // 4
ROPE_BASE = 10000.0
HY_W = GROUP_W
HY_ORDER = 2
HY_SHORT = 3
HY_BANDS = 16
HY_POS_DIM = 1 + 2 * HY_BANDS
HY_HID = 64
LRU_W = GROUP_W
LRU_BLOCKS = 8
LRU_BD = LRU_W // LRU_BLOCKS
LRU_CONV = 4
LRU_C = 8.0
N_EXPERTS = 64
TOP_K = 6
EXPERT_FF = D_MODEL // 4
SHARED_FF = EXPERT_FF
ROUTE_SCALE = 2.5
MOE_BLK = 128
NEG_INF = -1e30
IN_SIZES = (2 * DIFF_HEADS * DIFF_QK, 2 * DIFF_HEADS * DIFF_QK, DIFF_HEADS * DIFF_VD, SWA_HEADS * HEAD_DIM, SWA_KV_HEADS * HEAD_DIM, SWA_KV_HEADS * HEAD_DIM, (HY_ORDER + 1) * HY_W, LRU_W, LRU_W)
D_IN = sum(IN_SIZES)

kernel_name = 'hybrid_diffusion_prefix_step'


def rmsnorm(x, g):
    xf = x.astype(F32)
    y = xf * lax.rsqrt(jnp.mean(xf * xf, axis=-1, keepdims=True) + EPS)
    return (y * g.astype(F32)).astype(x.dtype)


def dwconv(x, w, left):
    K, L = w.shape[0], x.shape[1]
    xp = jnp.pad(x, ((0, 0), (left, K - 1 - left), (0, 0)))
    acc = xp[:, 0:L] * w[0]
    for k in range(1, K):
        acc = acc + xp[:, k:k + L] * w[k]
    return acc


def axial_rope(L):
    rows = L // GRID_W
    row = jnp.repeat(jnp.arange(rows), GRID_W).astype(F32)
    col = jnp.tile(jnp.arange(GRID_W), rows).astype(F32)
    inv = ROPE_BASE ** (-jnp.arange(ROPE_PAIRS, dtype=F32) / ROPE_PAIRS)
    ang = jnp.stack([row[:, None] * inv, col[:, None] * inv], axis=1)
    return jnp.cos(ang), jnp.sin(ang)


def apply_rope(x, cos, sin):
    xs = x.reshape(*x.shape[:-1], 2, 2, ROPE_PAIRS)
    c = cos[:, None, :, None, :].astype(x.dtype)
    s = sin[:, None, :, None, :].astype(x.dtype)
    x1, x2 = xs[..., 0:1, :], xs[..., 1:2, :]
    return jnp.concatenate([x1 * c - x2 * s, x2 * c + x1 * s], axis=-2).reshape(x.shape)


def diff_attend(q, k, v, lam):
    B, Lq = q.shape[:2]
    nb = Lq // QBLK
    qb = jnp.moveaxis(q.reshape(B, nb, QBLK, *q.shape[2:]), 1, 0)

    def one(qblk):
        s = jnp.einsum('bqhmd,bkhmd->bhmqk', qblk, k, preferred_element_type=F32)
        p = jax.nn.softmax(s, axis=-1)
        w = p[:, :, 0] - lam * p[:, :, 1]
        return jnp.einsum('bhqk,bkhd->bqhd', w.astype(v.dtype), v)

    o = lax.map(one, qb)
    return jnp.moveaxis(o, 0, 1).reshape(B, Lq, q.shape[2], v.shape[-1])


def sink_softmax(s, sink):
    sb = sink.astype(F32)[None, :, :, None, None]
    m = jnp.maximum(jnp.max(s, axis=-1, keepdims=True), sb)
    e = jnp.exp(s - m)
    return e / (jnp.sum(e, axis=-1, keepdims=True) + jnp.exp(sb - m))


def swa_dense(q, k, v, sink):
    B, L = q.shape[:2]
    nb = L // QBLK
    qb = jnp.moveaxis(q.reshape(B, nb, QBLK, *q.shape[2:]), 1, 0)

    def one(qblk):
        p = sink_softmax(jnp.einsum('bqkgd,bnkd->bkgqn', qblk, k, preferred_element_type=F32), sink)
        return jnp.einsum('bkgqn,bnkd->bqkgd', p.astype(v.dtype), v)

    o = lax.map(one, qb)
    return jnp.moveaxis(o, 0, 1).reshape(B, L, SWA_HEADS * HEAD_DIM)


def swa_banded(q, k, v, k_ctx, v_ctx, sink):
    B, L = q.shape[:2]
    nb = L // WINDOW
    pad = lambda t: jnp.pad(t, ((0, 0), (WINDOW, WINDOW), (0, 0), (0, 0))).reshape(B, nb + 2, WINDOW, *t.shape[2:])
    kb, vb = pad(k), pad(v)
    kwin = jnp.concatenate([kb[:, :nb], kb[:, 1:nb + 1], kb[:, 2:]], axis=2)
    vwin = jnp.concatenate([vb[:, :nb], vb[:, 1:nb + 1], vb[:, 2:]], axis=2)
    qi = jnp.arange(WINDOW)[:, None]
    kj = jnp.arange(3 * WINDOW)[None, :]
    rel = kj - qi
    band = (rel >= 0) & (rel <= 2 * WINDOW)
    qb = jnp.moveaxis(q.reshape(B, nb, WINDOW, *q.shape[2:]), 1, 0)

    def one(args):
        b, qblk, kw, vw = args
        kpos = (b - 1) * WINDOW + kj
        mask = band & (kpos >= 0) & (kpos < L)
        s_loc = jnp.where(mask, jnp.einsum('bqkgd,bnkd->bkgqn', qblk, kw, preferred_element_type=F32), NEG_INF)
        s_ctx = jnp.einsum('bqkgd,bnkd->bkgqn', qblk, k_ctx, preferred_element_type=F32)
        p = sink_softmax(jnp.concatenate([s_loc, s_ctx], axis=-1), sink).astype(v.dtype)
        return (jnp.einsum('bkgqn,bnkd->bqkgd', p[..., :3 * WINDOW], vw)
                + jnp.einsum('bkgqn,bnkd->bqkgd', p[..., 3 * WINDOW:], v_ctx))

    o = lax.map(one, (jnp.arange(nb), qb, jnp.moveaxis(kwin, 1, 0), jnp.moveaxis(vwin, 1, 0)))
    return jnp.moveaxis(o, 0, 1).reshape(B, L, SWA_HEADS * HEAD_DIM)


def hyena_filters(L, w1, b1, w2, b2, w3, freq, log_rate):
    tn = jnp.arange(L, dtype=F32) / L
    ang = 2.0 * math.pi * tn[:, None] * jnp.arange(1, HY_BANDS + 1, dtype=F32)
    z = jnp.concatenate([tn[:, None], jnp.sin(ang), jnp.cos(ang)], axis=-1)
    f = jnp.sin(freq[0].astype(F32) * (z @ w1.astype(F32) + b1.astype(F32)))
    f = jnp.sin(freq[1].astype(F32) * (f @ w2.astype(F32) + b2.astype(F32)))
    f = (f @ w3.astype(F32)).reshape(L, HY_ORDER, 2, HY_W)
    f = f * jnp.exp(-tn[:, None, None, None] * jnp.exp(log_rate.astype(F32)))
    return f * lax.rsqrt(jnp.sum(f * f, axis=(0, 2), keepdims=True) + EPS)


def bidir_fftconv(u, kf, kb):
    L, C = u.shape[1], u.shape[2]
    kfull = jnp.concatenate([kf.at[0].add(kb[0]), jnp.zeros((1, C), F32), kb[1:][::-1]], axis=0)
    Kf = jnp.fft.rfft(kfull, axis=0)
    Uf = jnp.fft.rfft(u, n=2 * L, axis=1)
    return jnp.fft.irfft(Uf * Kf[None], n=2 * L, axis=1)[:, :L]


def hyena_mix(u, p):
    L = u.shape[1]
    u = dwconv(u, p['hy_conv'], HY_SHORT // 2)
    parts = jnp.split(u, HY_ORDER + 1, axis=-1)
    filt = hyena_filters(L, p['hy_w1'], p['hy_b1'], p['hy_w2'], p['hy_b2'], p['hy_w3'], p['hy_freq'], p['hy_log_rate'])
    skip = p['hy_skip'].astype(F32)
    z = parts[0].astype(F32)
    for o in range(HY_ORDER):
        z = parts[o + 1].astype(F32) * (bidir_fftconv(z, filt[:, o, 0], filt[:, o, 1]) + skip[o] * z)
    return z.astype(u.dtype)


def _lin_combine(e1, e2):
    a1, b1 = e1
    a2, b2 = e2
    return a1 * a2, a2 * b1 + b2


def rglru_bidir(x, wa, ba, wx, bx, lam, h0_f, h0_b):
    B, L, W = x.shape
    xb = x.reshape(B, L, LRU_BLOCKS, LRU_BD)
    hs = []
    for d, (h0, rev) in enumerate(((h0_f, False), (h0_b, True))):
        r = jax.nn.sigmoid(jnp.einsum('blnd,nde->blne', xb, wa[d]).reshape(B, L, W) + ba[d])
        i = jax.nn.sigmoid(jnp.einsum('blnd,nde->blne', xb, wx[d]).reshape(B, L, W) + bx[d])
        log_a = LRU_C * r.astype(F32) * jax.nn.log_sigmoid(lam[d].astype(F32))
        a = jnp.exp(log_a)
        bterm = jnp.sqrt(-jnp.expm1(2.0 * log_a)) * (i * x).astype(F32)
        first = L - 1 if rev else 0
        bterm = bterm.at[:, first].add(a[:, first] * h0.astype(F32))
        _, h = lax.associative_scan(_lin_combine, (a, bterm), axis=1, reverse=rev)
        hs.append(h)
    return hs[0] + hs[1], hs[0][:, -1], hs[1][:, 0]


def swiglu(x, w_gu, w_dn):
    a, b = jnp.split(x @ w_gu, 2, axis=-1)
    return (jax.nn.silu(a) * b) @ w_dn


def routed_experts(xf, idx, g, w_gu, w_dn):
    T, D = xf.shape
    A = T * TOP_K
    e_flat = idx.reshape(A)
    order = jnp.argsort(e_flat)
    e_sorted = e_flat[order]
    tok = order // TOP_K
    counts = jnp.bincount(e_flat, length=N_EXPERTS)
    padded = (counts + MOE_BLK - 1) // MOE_BLK * MOE_BLK
    pad_end = jnp.cumsum(padded)
    pad_start = pad_end - padded
    start = jnp.cumsum(counts) - counts
    dest = pad_start[e_sorted] + (jnp.arange(A) - start[e_sorted])
    n_blk = (A + MOE_BLK - 1) // MOE_BLK + N_EXPERTS
    buf = jnp.zeros((n_blk * MOE_BLK, D), xf.dtype).at[dest].set(xf[tok])
    blk_e = jnp.minimum(jnp.searchsorted(pad_end, jnp.arange(n_blk) * MOE_BLK, side='right'), N_EXPERTS - 1)

    def run(args):
        xb, e = args
        return swiglu(xb, w_gu[e], w_dn[e])

    out = lax.map(run, (buf.reshape(n_blk, MOE_BLK, D), blk_e)).reshape(n_blk * MOE_BLK, D)
    y = out[dest] * g.reshape(A)[order][:, None]
    return jnp.zeros_like(xf).at[tok].add(y)


def moe(h, p):
    B, L, D = h.shape
    xf = h.reshape(B * L, D)
    scores = jax.nn.sigmoid(jnp.matmul(xf, p['router_w'], preferred_element_type=F32))
    _, idx = lax.top_k(scores + p['router_bias'].astype(F32), TOP_K)
    g = jnp.take_along_axis(scores, idx, axis=-1)
    g = ROUTE_SCALE * g / jnp.sum(g, axis=-1, keepdims=True)
    y = routed_experts(xf, idx, g.astype(xf.dtype), p['moe_w_gu'], p['moe_w_dn']) + swiglu(xf, p['sh_w_gu'], p['sh_w_dn'])
    return y.reshape(B, L, D)


def mixer(h, p, lam_init, ctx):
    B, L, _ = h.shape
    proj = h @ p['w_in']
    dq, dk, dv, sq, sk, sv, hy, lx, lg = jnp.split(proj, np.cumsum(IN_SIZES)[:-1].tolist(), axis=-1)
    dq = dq.reshape(B, L, DIFF_HEADS, 2, DIFF_QK) * DIFF_SCALE
    dk = dk.reshape(B, L, DIFF_HEADS, 2, DIFF_QK)
    dv = dv.reshape(B, L, DIFF_HEADS, DIFF_VD)
    sq = sq.reshape(B, L, SWA_KV_HEADS, SWA_GROUP, HEAD_DIM) * SWA_SCALE
    sk = sk.reshape(B, L, SWA_KV_HEADS, HEAD_DIM)
    sv = sv.reshape(B, L, SWA_KV_HEADS, HEAD_DIM)
    lq1, lk1, lq2, lk2 = p['diff_lam'].astype(F32)
    lam = jnp.exp(jnp.sum(lq1 * lk1)) - jnp.exp(jnp.sum(lq2 * lk2)) + lam_init
    sink = p['swa_sink'].reshape(SWA_KV_HEADS, SWA_GROUP)
    if ctx is None:
        d_out = diff_attend(dq, dk, dv, lam)
        s_out = swa_dense(sq, sk, sv, sink)
        h0_f = jnp.zeros((B, LRU_W), F32)
        h0_b = jnp.zeros((B, LRU_W), F32)
    else:
        ck_d, cv_d, ck_s, cv_s, st = ctx
        cos, sin = axial_rope(L)
        dq_r = apply_rope(dq.reshape(B, L, 2 * DIFF_HEADS, DIFF_QK), cos, sin).reshape(dq.shape)
        dk_r = apply_rope(dk.reshape(B, L, 2 * DIFF_HEADS, DIFF_QK), cos, sin).reshape(dk.shape)
        d_out = diff_attend(dq_r, jnp.concatenate([dk_r, ck_d], axis=1), jnp.concatenate([dv, cv_d], axis=1), lam)
        sq_r = apply_rope(sq.reshape(B, L, SWA_HEADS, HEAD_DIM), cos, sin).reshape(sq.shape)
        s_out = swa_banded(sq_r, apply_rope(sk, cos, sin), sv, ck_s, cv_s, sink)
        h0_f, h0_b = st[:, 0], st[:, 1]
    d_out = (rmsnorm(d_out, p['diff_subln']) * (1.0 - lam_init)).reshape(B, L, DIFF_HEADS * DIFF_VD)
    hy_out = hyena_mix(hy, p)
    lru_in = dwconv(lx, p['lru_conv'], LRU_CONV // 2)
    hl, hf_last, hb_first = rglru_bidir(lru_in, p['lru_wa'], p['lru_ba'], p['lru_wx'], p['lru_bx'], p['lru_lam'], h0_f, h0_b)
    lru_out = hl.astype(h.dtype) * jax.nn.gelu(lg)
    out = jnp.concatenate([d_out, s_out, hy_out, lru_out], axis=-1) @ p['w_out']
    if ctx is None:
        return out, (dk, dv, sk, sv, jnp.stack([hf_last, hb_first], axis=1).astype(h.dtype))
    return out, None


def run_layer(x, cvec, p, lam_init, ctx):
    mod = (jax.nn.silu(cvec) @ p['w_mod'] + p['b_mod'])[:, None, :]
    sh1, sc1, g1, sh2, sc2, g2 = jnp.split(mod, 6, axis=-1)
    h = rmsnorm(x, p['norm1']) * (1.0 + sc1) + sh1
    m, ctx_out = mixer(h, p, lam_init, ctx)
    x = x + g1 * m
    h = rmsnorm(x, p['norm2']) * (1.0 + sc2) + sh2
    x = x + g2 * moe(h, p)
    return x, ctx_out


def setup_inputs(seed: int = 0) -> dict:
    key = jax.random.key(seed)
    ks = jax.random.split(key, 40)
    nrm = lambda k, shape, s: s * jax.random.normal(k, shape, F32)
    D, E, F = D_MODEL, N_EXPERTS, EXPERT_FF
    s_lru = jax.random.uniform(ks[33], (DEPTH, 2, LRU_W), F32, 0.9, 0.999) ** (1.0 / LRU_C)
    return {
        'x_prompt': nrm(ks[0], (BATCH, SEQ, D), 1.0),
        'x_sample': nrm(ks[1], (DEC_BATCH, DEC_SEQ, D), 1.0),
        'cache_diff_k': nrm(ks[2], (DEC_BATCH, DEPTH, PAST_LEN, DIFF_HEADS, 2, DIFF_QK), 1.0),
        'cache_diff_v': nrm(ks[3], (DEC_BATCH, DEPTH, PAST_LEN, DIFF_HEADS, DIFF_VD), 1.0),
        'cache_swa_k': nrm(ks[4], (DEC_BATCH, DEPTH, PAST_LEN, SWA_KV_HEADS, HEAD_DIM), 1.0),
        'cache_swa_v': nrm(ks[5], (DEC_BATCH, DEPTH, PAST_LEN, SWA_KV_HEADS, HEAD_DIM), 1.0),
        'state_lru': nrm(ks[6], (DEC_BATCH, DEPTH, 2, LRU_W), 0.5),
        'c': nrm(ks[7], (DEC_BATCH, D), 1.0),
        'c_ctx': nrm(ks[8], (D,), 1.0),
        'w_mod': nrm(ks[9], (DEPTH, D, 6 * D), 0.5 * D ** -0.5),
        'b_mod': nrm(ks[10], (DEPTH, 6 * D), 0.02),
        'norm1': 1.0 + nrm(ks[11], (DEPTH, D), 0.05),
        'norm2': 1.0 + nrm(ks[12], (DEPTH, D), 0.05),
        'w_in': nrm(ks[13], (DEPTH, D, D_IN), D ** -0.5),
        'w_out': nrm(ks[14], (DEPTH, D_MIX, D), D_MIX ** -0.5),
        'diff_lam': nrm(ks[15], (DEPTH, 4, DIFF_QK), 0.1),
        'diff_subln': 1.0 + nrm(ks[16], (DEPTH, DIFF_VD), 0.05),
        'swa_sink': nrm(ks[17], (DEPTH, SWA_HEADS), 0.5),
        'hy_conv': nrm(ks[18], (DEPTH, HY_SHORT, (HY_ORDER + 1) * HY_W), HY_SHORT ** -0.5),
        'hy_w1': nrm(ks[19], (DEPTH, HY_POS_DIM, HY_HID), HY_POS_DIM ** -0.5),
        'hy_b1': nrm(ks[20], (DEPTH, HY_HID), 0.1),
        'hy_w2': nrm(ks[21], (DEPTH, HY_HID, HY_HID), HY_HID ** -0.5),
        'hy_b2': nrm(ks[22], (DEPTH, HY_HID), 0.1),
        'hy_w3': nrm(ks[23], (DEPTH, HY_HID, HY_ORDER * 2 * HY_W), HY_HID ** -0.5),
        'hy_freq': 1.0 + nrm(ks[24], (DEPTH, 2, HY_HID), 0.1),
        'hy_log_rate': jnp.log(jax.random.uniform(ks[25], (DEPTH, HY_ORDER, 2, HY_W), F32, 3.0, 15.0)),
        'hy_skip': 1.0 + nrm(ks[26], (DEPTH, HY_ORDER, HY_W), 0.1),
        'lru_conv': nrm(ks[27], (DEPTH, LRU_CONV, LRU_W), 0.5),
        'lru_wa': nrm(ks[28], (DEPTH, 2, LRU_BLOCKS, LRU_BD, LRU_BD), LRU_BD ** -0.5),
        'lru_ba': nrm(ks[29], (DEPTH, 2, LRU_W), 0.1),
        'lru_wx': nrm(ks[30], (DEPTH, 2, LRU_BLOCKS, LRU_BD, LRU_BD), LRU_BD ** -0.5),
        'lru_bx': nrm(ks[31], (DEPTH, 2, LRU_W), 0.1),
        'lru_lam': jnp.log(s_lru) - jnp.log1p(-s_lru),
        'router_w': nrm(ks[34], (DEPTH, D, E), D ** -0.5),
        'router_bias': nrm(ks[35], (DEPTH, E), 0.01),
        'moe_w_gu': nrm(ks[36], (DEPTH, E, D, 2 * F), D ** -0.5),
        'moe_w_dn': nrm(ks[37], (DEPTH, E, F, D), F ** -0.5),
        'sh_w_gu': nrm(ks[38], (DEPTH, D, 2 * SHARED_FF), D ** -0.5),
        'sh_w_dn': nrm(ks[39], (DEPTH, SHARED_FF, D), SHARED_FF ** -0.5),
        'final_norm': 1.0 + nrm(ks[32], (D,), 0.05),
    }


def reference(x_prompt, x_sample, cache_diff_k, cache_diff_v, cache_swa_k, cache_swa_v, state_lru, c, c_ctx,
              w_mod, b_mod, norm1, norm2, w_in, w_out, diff_lam, diff_subln, swa_sink, hy_conv, hy_w1, hy_b1,
              hy_w2, hy_b2, hy_w3, hy_freq, hy_log_rate, hy_skip, lru_conv, lru_wa, lru_ba, lru_wx, lru_bx, lru_lam,
              router_w, router_bias, moe_w_gu, moe_w_dn, sh_w_gu, sh_w_dn, final_norm):
    xp, xs = x_prompt, x_sample
    dk_l, dv_l, sk_l, sv_l, st_l = [], [], [], [], []
    for l in range(DEPTH):
        p = {'w_mod': w_mod[l], 'b_mod': b_mod[l], 'norm1': norm1[l], 'norm2': norm2[l], 'w_in': w_in[l],
             'w_out': w_out[l], 'diff_lam': diff_lam[l], 'diff_subln': diff_subln[l], 'swa_sink': swa_sink[l],
             'hy_conv': hy_conv[l], 'hy_w1': hy_w1[l], 'hy_b1': hy_b1[l], 'hy_w2': hy_w2[l], 'hy_b2': hy_b2[l],
             'hy_w3': hy_w3[l], 'hy_freq': hy_freq[l], 'hy_log_rate': hy_log_rate[l], 'hy_skip': hy_skip[l],
             'lru_conv': lru_conv[l], 'lru_wa': lru_wa[l], 'lru_ba': lru_ba[l], 'lru_wx': lru_wx[l],
             'lru_bx': lru_bx[l], 'lru_lam': lru_lam[l], 'router_w': router_w[l], 'router_bias': router_bias[l],
             'moe_w_gu': moe_w_gu[l], 'moe_w_dn': moe_w_dn[l], 'sh_w_gu': sh_w_gu[l], 'sh_w_dn': sh_w_dn[l]}
        lam_init = 0.8 - 0.6 * math.exp(-0.3 * l)
        xp, (k_d, v_d, k_s, v_s, st) = run_layer(xp, c_ctx[None, :], p, lam_init, None)
        dk_l.append(k_d); dv_l.append(v_d); sk_l.append(k_s); sv_l.append(v_s); st_l.append(st)
        ctx = (cache_diff_k[:, l], cache_diff_v[:, l], cache_swa_k[:, l], cache_swa_v[:, l], state_lru[:, l])
        xs, _ = run_layer(xs, c, p, lam_init, ctx)
    y_prompt = rmsnorm(xp, final_norm)
    y_sample = rmsnorm(xs, final_norm)
    new_diff_k = jnp.stack(dk_l, axis=1)
    new_diff_v = jnp.stack(dv_l, axis=1)
    new_swa_k = jnp.stack(sk_l, axis=1)
    new_swa_v = jnp.stack(sv_l, axis=1)
    new_state_lru = jnp.stack(st_l, axis=1)
    return (y_prompt, y_sample, new_diff_k, new_diff_v, new_swa_k, new_swa_v, new_state_lru)
```

```python
import math
import jax, jax.numpy as jnp
from jax import lax
import numpy as np
from jax.experimental import pallas as pl

D_MODEL = 2048
BATCH = 16
SEQ = 256
DEPTH = 2
DEC_BATCH = 2
DEC_SEQ = 4096
PAST_LEN = 256

GRID_W = 64
EPS = 1e-6
F32 = jnp.float32
QBLK = 128
GROUP_W = D_MODEL // 4
D_MIX = 4 * GROUP_W
DIFF_QK = 64
DIFF_VD = 2 * DIFF_QK
DIFF_HEADS = GROUP_W // DIFF_VD
DIFF_SCALE = DIFF_QK ** -0.5
HEAD_DIM = 64
SWA_HEADS = GROUP_W // HEAD_DIM
SWA_KV_HEADS = SWA_HEADS // 4
SWA_GROUP = SWA_HEADS // SWA_KV_HEADS
SWA_SCALE = HEAD_DIM ** -0.5
WINDOW = 128
ROPE_PAIRS = HEAD_DIM // 4
ROPE_BASE = 10000.0
HY_W = GROUP_W
HY_ORDER = 2
HY_SHORT = 3
HY_BANDS = 16
HY_POS_DIM = 1 + 2 * HY_BANDS
HY_HID = 64
LRU_W = GROUP_W
LRU_BLOCKS = 8
LRU_BD = LRU_W // LRU_BLOCKS
LRU_CONV = 4
LRU_C = 8.0
N_EXPERTS = 64
TOP_K = 6
EXPERT_FF = D_MODEL // 4
SHARED_FF = EXPERT_FF
ROUTE_SCALE = 2.5
MOE_BLK = 128
NEG_INF = -1e30
IN_SIZES = (2 * DIFF_HEADS * DIFF_QK, 2 * DIFF_HEADS * DIFF_QK, DIFF_HEADS * DIFF_VD, SWA_HEADS * HEAD_DIM, SWA_KV_HEADS * HEAD_DIM, SWA_KV_HEADS * HEAD_DIM, (HY_ORDER + 1) * HY_W, LRU_W, LRU_W)
D_IN = sum(IN_SIZES)


def rmsnorm(x, g):
    xf = x.astype(F32)
    y = xf * lax.rsqrt(jnp.mean(xf * xf, axis=-1, keepdims=True) + EPS)
    return (y * g.astype(F32)).astype(x.dtype)


def dwconv(x, w, left):
    K, L = w.shape[0], x.shape[1]
    xp = jnp.pad(x, ((0, 0), (left, K - 1 - left), (0, 0)))
    acc = xp[:, 0:L] * w[0]
    for k in range(1, K):
        acc = acc + xp[:, k:k + L] * w[k]
    return acc


def axial_rope(L):
    rows = L // GRID_W
    row = jnp.repeat(jnp.arange(rows), GRID_W).astype(F32)
    col = jnp.tile(jnp.arange(GRID_W), rows).astype(F32)
    inv = ROPE_BASE ** (-jnp.arange(ROPE_PAIRS, dtype=F32) / ROPE_PAIRS)
    ang = jnp.stack([row[:, None] * inv, col[:, None] * inv], axis=1)
    return jnp.cos(ang), jnp.sin(ang)


def apply_rope(x, cos, sin):
    xs = x.reshape(*x.shape[:-1], 2, 2, ROPE_PAIRS)
    c = cos[:, None, :, None, :].astype(x.dtype)
    s = sin[:, None, :, None, :].astype(x.dtype)
    x1, x2 = xs[..., 0:1, :], xs[..., 1:2, :]
    return jnp.concatenate([x1 * c - x2 * s, x2 * c + x1 * s], axis=-2).reshape(x.shape)


def diff_attend(q, k, v, lam):
    B, Lq = q.shape[:2]
    nb = Lq // QBLK
    qb = jnp.moveaxis(q.reshape(B, nb, QBLK, *q.shape[2:]), 1, 0)

    def one(qblk):
        s = jnp.einsum('bqhmd,bkhmd->bhmqk', qblk, k, preferred_element_type=F32)
        p = jax.nn.softmax(s, axis=-1)
        w = p[:, :, 0] - lam * p[:, :, 1]
        return jnp.einsum('bhqk,bkhd->bqhd', w.astype(v.dtype), v)

    o = lax.map(one, qb)
    return jnp.moveaxis(o, 0, 1).reshape(B, Lq, q.shape[2], v.shape[-1])


def sink_softmax(s, sink):
    sb = sink.astype(F32)[None, :, :, None, None]
    m = jnp.maximum(jnp.max(s, axis=-1, keepdims=True), sb)
    e = jnp.exp(s - m)
    return e / (jnp.sum(e, axis=-1, keepdims=True) + jnp.exp(sb - m))


def swa_dense(q, k, v, sink):
    B, L = q.shape[:2]
    nb = L // QBLK
    qb = jnp.moveaxis(q.reshape(B, nb, QBLK, *q.shape[2:]), 1, 0)

    def one(qblk):
        p = sink_softmax(jnp.einsum('bqkgd,bnkd->bkgqn', qblk, k, preferred_element_type=F32), sink)
        return jnp.einsum('bkgqn,bnkd->bqkgd', p.astype(v.dtype), v)

    o = lax.map(one, qb)
    return jnp.moveaxis(o, 0, 1).reshape(B, L, SWA_HEADS * HEAD_DIM)


def swa_banded(q, k, v, k_ctx, v_ctx, sink):
    B, L = q.shape[:2]
    nb = L // WINDOW
    pad = lambda t: jnp.pad(t, ((0, 0), (WINDOW, WINDOW), (0, 0), (0, 0))).reshape(B, nb + 2, WINDOW, *t.shape[2:])
    kb, vb = pad(k), pad(v)
    kwin = jnp.concatenate([kb[:, :nb], kb[:, 1:nb + 1], kb[:, 2:]], axis=2)
    vwin = jnp.concatenate([vb[:, :nb], vb[:, 1:nb + 1], vb[:, 2:]], axis=2)
    qi = jnp.arange(WINDOW)[:, None]
    kj = jnp.arange(3 * WINDOW)[None, :]
    rel = kj - qi
    band = (rel >= 0) & (rel <= 2 * WINDOW)
    qb = jnp.moveaxis(q.reshape(B, nb, WINDOW, *q.shape[2:]), 1, 0)

    def one(args):
        b, qblk, kw, vw = args
        kpos = (b - 1) * WINDOW + kj
        mask = band & (kpos >= 0) & (kpos < L)
        s_loc = jnp.where(mask, jnp.einsum('bqkgd,bnkd->bkgqn', qblk, kw, preferred_element_type=F32), NEG_INF)
        s_ctx = jnp.einsum('bqkgd,bnkd->bkgqn', qblk, k_ctx, preferred_element_type=F32)
        p = sink_softmax(jnp.concatenate([s_loc, s_ctx], axis=-1), sink).astype(v.dtype)
        return (jnp.einsum('bkgqn,bnkd->bqkgd', p[..., :3 * WINDOW], vw)
                + jnp.einsum('bkgqn,bnkd->bqkgd', p[..., 3 * WINDOW:], v_ctx))

    o = lax.map(one, (jnp.arange(nb), qb, jnp.moveaxis(kwin, 1, 0), jnp.moveaxis(vwin, 1, 0)))
    return jnp.moveaxis(o, 0, 1).reshape(B, L, SWA_HEADS * HEAD_DIM)


def hyena_filters(L, w1, b1, w2, b2, w3, freq, log_rate):
    tn = jnp.arange(L, dtype=F32) / L
    ang = 2.0 * math.pi * tn[:, None] * jnp.arange(1, HY_BANDS + 1, dtype=F32)
    z = jnp.concatenate([tn[:, None], jnp.sin(ang), jnp.cos(ang)], axis=-1)
    f = jnp.sin(freq[0].astype(F32) * (z @ w1.astype(F32) + b1.astype(F32)))
    f = jnp.sin(freq[1].astype(F32) * (f @ w2.astype(F32) + b2.astype(F32)))
    f = (f @ w3.astype(F32)).reshape(L, HY_ORDER, 2, HY_W)
    f = f * jnp.exp(-tn[:, None, None, None] * jnp.exp(log_rate.astype(F32)))
    return f * lax.rsqrt(jnp.sum(f * f, axis=(0, 2), keepdims=True) + EPS)


def bidir_fftconv(u, kf, kb):
    L, C = u.shape[1], u.shape[2]
    kfull = jnp.concatenate([kf.at[0].add(kb[0]), jnp.zeros((1, C), F32), kb[1:][::-1]], axis=0)
    Kf = jnp.fft.rfft(kfull, axis=0)
    Uf = jnp.fft.rfft(u, n=2 * L, axis=1)
    return jnp.fft.irfft(Uf * Kf[None], n=2 * L, axis=1)[:, :L]


def hyena_mix(u, p):
    L = u.shape[1]
    u = dwconv(u, p['hy_conv'], HY_SHORT // 2)
    parts = jnp.split(u, HY_ORDER + 1, axis=-1)
    filt = hyena_filters(L, p['hy_w1'], p['hy_b1'], p['hy_w2'], p['hy_b2'], p['hy_w3'], p['hy_freq'], p['hy_log_rate'])
    skip = p['hy_skip'].astype(F32)
    z = parts[0].astype(F32)
    for o in range(HY_ORDER):
        z = parts[o + 1].astype(F32) * (bidir_fftconv(z, filt[:, o, 0], filt[:, o, 1]) + skip[o] * z)
    return z.astype(u.dtype)


def _lin_combine(e1, e2):
    a1, b1 = e1
    a2, b2 = e2
    return a1 * a2, a2 * b1 + b2


def rglru_bidir(x, wa, ba, wx, bx, lam, h0_f, h0_b):
    B, L, W = x.shape
    xb = x.reshape(B, L, LRU_BLOCKS, LRU_BD)
    hs = []
    for d, (h0, rev) in enumerate(((h0_f, False), (h0_b, True))):
        r = jax.nn.sigmoid(jnp.einsum('blnd,nde->blne', xb, wa[d]).reshape(B, L, W) + ba[d])
        i = jax.nn.sigmoid(jnp.einsum('blnd,nde->blne', xb, wx[d]).reshape(B, L, W) + bx[d])
        log_a = LRU_C * r.astype(F32) * jax.nn.log_sigmoid(lam[d].astype(F32))
        a = jnp.exp(log_a)
        bterm = jnp.sqrt(-jnp.expm1(2.0 * log_a)) * (i * x).astype(F32)
        first = L - 1 if rev else 0
        bterm = bterm.at[:, first].add(a[:, first] * h0.astype(F32))
        _, h = lax.associative_scan(_lin_combine, (a, bterm), axis=1, reverse=rev)
        hs.append(h)
    return hs[0] + hs[1], hs[0][:, -1], hs[1][:, 0]


def swiglu(x, w_gu, w_dn):
    a, b = jnp.split(x @ w_gu, 2, axis=-1)
    return (jax.nn.silu(a) * b) @ w_dn


def routed_experts(xf, idx, g, w_gu, w_dn):
    T, D = xf.shape
    A = T * TOP_K
    e_flat = idx.reshape(A)
    order = jnp.argsort(e_flat)
    e_sorted = e_flat[order]
    tok = order // TOP_K
    counts = jnp.bincount(e_flat, length=N_EXPERTS)
    padded = (counts + MOE_BLK - 1) // MOE_BLK * MOE_BLK
    pad_end = jnp.cumsum(padded)
    pad_start = pad_end - padded
    start = jnp.cumsum(counts) - counts
    dest = pad_start[e_sorted] + (jnp.arange(A) - start[e_sorted])
    n_blk = (A + MOE_BLK - 1) // MOE_BLK + N_EXPERTS
    buf = jnp.zeros((n_blk * MOE_BLK, D), xf.dtype).at[dest].set(xf[tok])
    blk_e = jnp.minimum(jnp.searchsorted(pad_end, jnp.arange(n_blk) * MOE_BLK, side='right'), N_EXPERTS - 1)

    def run(args):
        xb, e = args
        return swiglu(xb, w_gu[e], w_dn[e])

    out = lax.map(run, (buf.reshape(n_blk, MOE_BLK, D), blk_e)).reshape(n_blk * MOE_BLK, D)
    y = out[dest] * g.reshape(A)[order][:, None]
    return jnp.zeros_like(xf).at[tok].add(y)


def moe(h, p):
    B, L, D = h.shape
    xf = h.reshape(B * L, D)
    scores = jax.nn.sigmoid(jnp.matmul(xf, p['router_w'], preferred_element_type=F32))
    _, idx = lax.top_k(scores + p['router_bias'].astype(F32), TOP_K)
    g = jnp.take_along_axis(scores, idx, axis=-1)
    g = ROUTE_SCALE * g / jnp.sum(g, axis=-1, keepdims=True)
    y = routed_experts(xf, idx, g.astype(xf.dtype), p['moe_w_gu'], p['moe_w_dn']) + swiglu(xf, p['sh_w_gu'], p['sh_w_dn'])
    return y.reshape(B, L, D)


def mixer(h, p, lam_init, ctx):
    B, L, _ = h.shape
    proj = h @ p['w_in']
    dq, dk, dv, sq, sk, sv, hy, lx, lg = jnp.split(proj, np.cumsum(IN_SIZES)[:-1].tolist(), axis=-1)
    dq = dq.reshape(B, L, DIFF_HEADS, 2, DIFF_QK) * DIFF_SCALE
    dk = dk.reshape(B, L, DIFF_HEADS, 2, DIFF_QK)
    dv = dv.reshape(B, L, DIFF_HEADS, DIFF_VD)
    sq = sq.reshape(B, L, SWA_KV_HEADS, SWA_GROUP, HEAD_DIM) * SWA_SCALE
    sk = sk.reshape(B, L, SWA_KV_HEADS, HEAD_DIM)
    sv = sv.reshape(B, L, SWA_KV_HEADS, HEAD_DIM)
    lq1, lk1, lq2, lk2 = p['diff_lam'].astype(F32)
    lam = jnp.exp(jnp.sum(lq1 * lk1)) - jnp.exp(jnp.sum(lq2 * lk2)) + lam_init
    sink = p['swa_sink'].reshape(SWA_KV_HEADS, SWA_GROUP)
    if ctx is None:
        d_out = diff_attend(dq, dk, dv, lam)
        s_out = swa_dense(sq, sk, sv, sink)
        h0_f = jnp.zeros((B, LRU_W), F32)
        h0_b = jnp.zeros((B, LRU_W), F32)
    else:
        ck_d, cv_d, ck_s, cv_s, st = ctx
        cos, sin = axial_rope(L)
        dq_r = apply_rope(dq.reshape(B, L, 2 * DIFF_HEADS, DIFF_QK), cos, sin).reshape(dq.shape)
        dk_r = apply_rope(dk.reshape(B, L, 2 * DIFF_HEADS, DIFF_QK), cos, sin).reshape(dk.shape)
        d_out = diff_attend(dq_r, jnp.concatenate([dk_r, ck_d], axis=1), jnp.concatenate([dv, cv_d], axis=1), lam)
        sq_r = apply_rope(sq.reshape(B, L, SWA_HEADS, HEAD_DIM), cos, sin).reshape(sq.shape)
        s_out = swa_banded(sq_r, apply_rope(sk, cos, sin), sv, ck_s, cv_s, sink)
        h0_f, h0_b = st[:, 0], st[:, 1]
    d_out = (rmsnorm(d_out, p['diff_subln']) * (1.0 - lam_init)).reshape(B, L, DIFF_HEADS * DIFF_VD)
    hy_out = hyena_mix(hy, p)
    lru_in = dwconv(lx, p['lru_conv'], LRU_CONV // 2)
    hl, hf_last, hb_first = rglru_bidir(lru_in, p['lru_wa'], p['lru_ba'], p['lru_wx'], p['lru_bx'], p['lru_lam'], h0_f, h0_b)
    lru_out = hl.astype(h.dtype) * jax.nn.gelu(lg)
    out = jnp.concatenate([d_out, s_out, hy_out, lru_out], axis=-1) @ p['w_out']
    if ctx is None:
        return out, (dk, dv, sk, sv, jnp.stack([hf_last, hb_first], axis=1).astype(h.dtype))
    return out, None


def run_layer(x, cvec, p, lam_init, ctx):
    mod = (jax.nn.silu(cvec) @ p['w_mod'] + p['b_mod'])[:, None, :]
    sh1, sc1, g1, sh2, sc2, g2 = jnp.split(mod, 6, axis=-1)
    h = rmsnorm(x, p['norm1']) * (1.0 + sc1) + sh1
    m, ctx_out = mixer(h, p, lam_init, ctx)
    x = x + g1 * m
    h = rmsnorm(x, p['norm2']) * (1.0 + sc2) + sh2
    x = x + g2 * moe(h, p)
    return x, ctx_out


def _final_norm_body(x_ref, g_ref, o_ref):
    x = x_ref[...]
    o_ref[...] = x * lax.rsqrt(jnp.mean(x * x, axis=-1, keepdims=True) + EPS) * g_ref[...]


def final_rmsnorm(x, g):
    B, L, D = x.shape
    xf = x.reshape(B * L, D)
    out = pl.pallas_call(
        _final_norm_body,
        grid=(B * L // 512,),
        in_specs=[pl.BlockSpec((512, D), lambda i: (i, 0)), pl.BlockSpec((1, D), lambda i: (0, 0))],
        out_specs=pl.BlockSpec((512, D), lambda i: (i, 0)),
        out_shape=jax.ShapeDtypeStruct(xf.shape, xf.dtype),
    )(xf, g.reshape(1, D))
    return out.reshape(B, L, D)


def kernel(x_prompt, x_sample, cache_diff_k, cache_diff_v, cache_swa_k, cache_swa_v, state_lru, c, c_ctx,
           w_mod, b_mod, norm1, norm2, w_in, w_out, diff_lam, diff_subln, swa_sink, hy_conv, hy_w1, hy_b1,
           hy_w2, hy_b2, hy_w3, hy_freq, hy_log_rate, hy_skip, lru_conv, lru_wa, lru_ba, lru_wx, lru_bx, lru_lam,
           router_w, router_bias, moe_w_gu, moe_w_dn, sh_w_gu, sh_w_dn, final_norm):
    xp, xs = x_prompt, x_sample
    dk_l, dv_l, sk_l, sv_l, st_l = [], [], [], [], []
    for l in range(DEPTH):
        p = {'w_mod': w_mod[l], 'b_mod': b_mod[l], 'norm1': norm1[l], 'norm2': norm2[l], 'w_in': w_in[l],
             'w_out': w_out[l], 'diff_lam': diff_lam[l], 'diff_subln': diff_subln[l], 'swa_sink': swa_sink[l],
             'hy_conv': hy_conv[l], 'hy_w1': hy_w1[l], 'hy_b1': hy_b1[l], 'hy_w2': hy_w2[l], 'hy_b2': hy_b2[l],
             'hy_w3': hy_w3[l], 'hy_freq': hy_freq[l], 'hy_log_rate': hy_log_rate[l], 'hy_skip': hy_skip[l],
             'lru_conv': lru_conv[l], 'lru_wa': lru_wa[l], 'lru_ba': lru_ba[l], 'lru_wx': lru_wx[l],
             'lru_bx': lru_bx[l], 'lru_lam': lru_lam[l], 'router_w': router_w[l], 'router_bias': router_bias[l],
             'moe_w_gu': moe_w_gu[l], 'moe_w_dn': moe_w_dn[l], 'sh_w_gu': sh_w_gu[l], 'sh_w_dn': sh_w_dn[l]}
        lam_init = 0.8 - 0.6 * math.exp(-0.3 * l)
        xp, (k_d, v_d, k_s, v_s, st) = run_layer(xp, c_ctx[None, :], p, lam_init, None)
        dk_l.append(k_d); dv_l.append(v_d); sk_l.append(k_s); sv_l.append(v_s); st_l.append(st)
        ctx = (cache_diff_k[:, l], cache_diff_v[:, l], cache_swa_k[:, l], cache_swa_v[:, l], state_lru[:, l])
        xs, _ = run_layer(xs, c, p, lam_init, ctx)
    y_prompt = final_rmsnorm(xp, final_norm)
    y_sample = final_rmsnorm(xs, final_norm)
    new_diff_k = jnp.stack(dk_l, axis=1)
    new_diff_v = jnp.stack(dv_l, axis=1)
    new_swa_k = jnp.stack(sk_l, axis=1)
    new_swa_v = jnp.stack(sv_l, axis=1)
    new_state_lru = jnp.stack(st_l, axis=1)
    return (y_prompt, y_sample, new_diff_k, new_diff_v, new_swa_k, new_swa_v, new_state_lru)
```

```python
import math
import jax, jax.numpy as jnp
from jax import lax
import numpy as np
from jax.experimental import pallas as pl
from jax.experimental.pallas import tpu as pltpu

D_MODEL = 2048
BATCH = 16
SEQ = 256
DEPTH = 2
DEC_BATCH = 2
DEC_SEQ = 4096
PAST_LEN = 256

GRID_W = 64
EPS = 1e-6
F32 = jnp.float32
QBLK = 128
GROUP_W = D_MODEL // 4
D_MIX = 4 * GROUP_W
DIFF_QK = 64
DIFF_VD = 2 * DIFF_QK
DIFF_HEADS = GROUP_W // DIFF_VD
DIFF_SCALE = DIFF_QK ** -0.5
HEAD_DIM = 64
SWA_HEADS = GROUP_W // HEAD_DIM
SWA_KV_HEADS = SWA_HEADS // 4
SWA_GROUP = SWA_HEADS // SWA_KV_HEADS
SWA_SCALE = HEAD_DIM ** -0.5
WINDOW = 128
ROPE_PAIRS = HEAD_DIM // 4
ROPE_BASE = 10000.0
HY_W = GROUP_W
HY_ORDER = 2
HY_SHORT = 3
HY_BANDS = 16
HY_POS_DIM = 1 + 2 * HY_BANDS
HY_HID = 64
LRU_W = GROUP_W
LRU_BLOCKS = 8
LRU_BD = LRU_W // LRU_BLOCKS
LRU_CONV = 4
LRU_C = 8.0
N_EXPERTS = 64
TOP_K = 6
EXPERT_FF = D_MODEL // 4
SHARED_FF = EXPERT_FF
ROUTE_SCALE = 2.5
MOE_BLK = 128
NEG_INF = -1e30
IN_SIZES = (2 * DIFF_HEADS * DIFF_QK, 2 * DIFF_HEADS * DIFF_QK, DIFF_HEADS * DIFF_VD, SWA_HEADS * HEAD_DIM, SWA_KV_HEADS * HEAD_DIM, SWA_KV_HEADS * HEAD_DIM, (HY_ORDER + 1) * HY_W, LRU_W, LRU_W)
D_IN = sum(IN_SIZES)


def rmsnorm(x, g):
    xf = x.astype(F32)
    y = xf * lax.rsqrt(jnp.mean(xf * xf, axis=-1, keepdims=True) + EPS)
    return (y * g.astype(F32)).astype(x.dtype)


def dwconv(x, w, left):
    K, L = w.shape[0], x.shape[1]
    xp = jnp.pad(x, ((0, 0), (left, K - 1 - left), (0, 0)))
    acc = xp[:, 0:L] * w[0]
    for k in range(1, K):
        acc = acc + xp[:, k:k + L] * w[k]
    return acc


def axial_rope(L):
    rows = L // GRID_W
    row = jnp.repeat(jnp.arange(rows), GRID_W).astype(F32)
    col = jnp.tile(jnp.arange(GRID_W), rows).astype(F32)
    inv = ROPE_BASE ** (-jnp.arange(ROPE_PAIRS, dtype=F32) / ROPE_PAIRS)
    ang = jnp.stack([row[:, None] * inv, col[:, None] * inv], axis=1)
    return jnp.cos(ang), jnp.sin(ang)


def apply_rope(x, cos, sin):
    xs = x.reshape(*x.shape[:-1], 2, 2, ROPE_PAIRS)
    c = cos[:, None, :, None, :].astype(x.dtype)
    s = sin[:, None, :, None, :].astype(x.dtype)
    x1, x2 = xs[..., 0:1, :], xs[..., 1:2, :]
    return jnp.concatenate([x1 * c - x2 * s, x2 * c + x1 * s], axis=-2).reshape(x.shape)


def diff_attend(q, k, v, lam):
    B, Lq = q.shape[:2]
    nb = Lq // QBLK
    qb = jnp.moveaxis(q.reshape(B, nb, QBLK, *q.shape[2:]), 1, 0)

    def one(qblk):
        s = jnp.einsum('bqhmd,bkhmd->bhmqk', qblk, k, preferred_element_type=F32)
        p = jax.nn.softmax(s, axis=-1)
        w = p[:, :, 0] - lam * p[:, :, 1]
        return jnp.einsum('bhqk,bkhd->bqhd', w.astype(v.dtype), v)

    o = lax.map(one, qb)
    return jnp.moveaxis(o, 0, 1).reshape(B, Lq, q.shape[2], v.shape[-1])


def sink_softmax(s, sink):
    sb = sink.astype(F32)[None, :, :, None, None]
    m = jnp.maximum(jnp.max(s, axis=-1, keepdims=True), sb)
    e = jnp.exp(s - m)
    return e / (jnp.sum(e, axis=-1, keepdims=True) + jnp.exp(sb - m))


def swa_dense(q, k, v, sink):
    B, L = q.shape[:2]
    nb = L // QBLK
    qb = jnp.moveaxis(q.reshape(B, nb, QBLK, *q.shape[2:]), 1, 0)

    def one(qblk):
        p = sink_softmax(jnp.einsum('bqkgd,bnkd->bkgqn', qblk, k, preferred_element_type=F32), sink)
        return jnp.einsum('bkgqn,bnkd->bqkgd', p.astype(v.dtype), v)

    o = lax.map(one, qb)
    return jnp.moveaxis(o, 0, 1).reshape(B, L, SWA_HEADS * HEAD_DIM)


def swa_banded(q, k, v, k_ctx, v_ctx, sink):
    B, L = q.shape[:2]
    nb = L // WINDOW
    pad = lambda t: jnp.pad(t, ((0, 0), (WINDOW, WINDOW), (0, 0), (0, 0))).reshape(B, nb + 2, WINDOW, *t.shape[2:])
    kb, vb = pad(k), pad(v)
    kwin = jnp.concatenate([kb[:, :nb], kb[:, 1:nb + 1], kb[:, 2:]], axis=2)
    vwin = jnp.concatenate([vb[:, :nb], vb[:, 1:nb + 1], vb[:, 2:]], axis=2)
    qi = jnp.arange(WINDOW)[:, None]
    kj = jnp.arange(3 * WINDOW)[None, :]
    rel = kj - qi
    band = (rel >= 0) & (rel <= 2 * WINDOW)
    qb = jnp.moveaxis(q.reshape(B, nb, WINDOW, *q.shape[2:]), 1, 0)

    def one(args):
        b, qblk, kw, vw = args
        kpos = (b - 1) * WINDOW + kj
        mask = band & (kpos >= 0) & (kpos < L)
        s_loc = jnp.where(mask, jnp.einsum('bqkgd,bnkd->bkgqn', qblk, kw, preferred_element_type=F32), NEG_INF)
        s_ctx = jnp.einsum('bqkgd,bnkd->bkgqn', qblk, k_ctx, preferred_element_type=F32)
        p = sink_softmax(jnp.concatenate([s_loc, s_ctx], axis=-1), sink).astype(v.dtype)
        return (jnp.einsum('bkgqn,bnkd->bqkgd', p[..., :3 * WINDOW], vw)
                + jnp.einsum('bkgqn,bnkd->bqkgd', p[..., 3 * WINDOW:], v_ctx))

    o = lax.map(one, (jnp.arange(nb), qb, jnp.moveaxis(kwin, 1, 0), jnp.moveaxis(vwin, 1, 0)))
    return jnp.moveaxis(o, 0, 1).reshape(B, L, SWA_HEADS * HEAD_DIM)


def hyena_filters(L, w1, b1, w2, b2, w3, freq, log_rate):
    tn = jnp.arange(L, dtype=F32) / L
    ang = 2.0 * math.pi * tn[:, None] * jnp.arange(1, HY_BANDS + 1, dtype=F32)
    z = jnp.concatenate([tn[:, None], jnp.sin(ang), jnp.cos(ang)], axis=-1)
    f = jnp.sin(freq[0].astype(F32) * (z @ w1.astype(F32) + b1.astype(F32)))
    f = jnp.sin(freq[1].astype(F32) * (f @ w2.astype(F32) + b2.astype(F32)))
    f = (f @ w3.astype(F32)).reshape(L, HY_ORDER, 2, HY_W)
    f = f * jnp.exp(-tn[:, None, None, None] * jnp.exp(log_rate.astype(F32)))
    return f * lax.rsqrt(jnp.sum(f * f, axis=(0, 2), keepdims=True) + EPS)


def bidir_fftconv(u, kf, kb):
    L, C = u.shape[1], u.shape[2]
    kfull = jnp.concatenate([kf.at[0].add(kb[0]), jnp.zeros((1, C), F32), kb[1:][::-1]], axis=0)
    Kf = jnp.fft.rfft(kfull, axis=0)
    Uf = jnp.fft.rfft(u, n=2 * L, axis=1)
    return jnp.fft.irfft(Uf * Kf[None], n=2 * L, axis=1)[:, :L]


def hyena_mix(u, p):
    L = u.shape[1]
    u = dwconv(u, p['hy_conv'], HY_SHORT // 2)
    parts = jnp.split(u, HY_ORDER + 1, axis=-1)
    filt = hyena_filters(L, p['hy_w1'], p['hy_b1'], p['hy_w2'], p['hy_b2'], p['hy_w3'], p['hy_freq'], p['hy_log_rate'])
    skip = p['hy_skip'].astype(F32)
    z = parts[0].astype(F32)
    for o in range(HY_ORDER):
        z = parts[o + 1].astype(F32) * (bidir_fftconv(z, filt[:, o, 0], filt[:, o, 1]) + skip[o] * z)
    return z.astype(u.dtype)


def _lin_combine(e1, e2):
    a1, b1 = e1
    a2, b2 = e2
    return a1 * a2, a2 * b1 + b2


def rglru_bidir(x, wa, ba, wx, bx, lam, h0_f, h0_b):
    B, L, W = x.shape
    xb = x.reshape(B, L, LRU_BLOCKS, LRU_BD)
    hs = []
    for d, (h0, rev) in enumerate(((h0_f, False), (h0_b, True))):
        r = jax.nn.sigmoid(jnp.einsum('blnd,nde->blne', xb, wa[d]).reshape(B, L, W) + ba[d])
        i = jax.nn.sigmoid(jnp.einsum('blnd,nde->blne', xb, wx[d]).reshape(B, L, W) + bx[d])
        log_a = LRU_C * r.astype(F32) * jax.nn.log_sigmoid(lam[d].astype(F32))
        a = jnp.exp(log_a)
        bterm = jnp.sqrt(-jnp.expm1(2.0 * log_a)) * (i * x).astype(F32)
        first = L - 1 if rev else 0
        bterm = bterm.at[:, first].add(a[:, first] * h0.astype(F32))
        _, h = lax.associative_scan(_lin_combine, (a, bterm), axis=1, reverse=rev)
        hs.append(h)
    return hs[0] + hs[1], hs[0][:, -1], hs[1][:, 0]


BF16 = jnp.bfloat16
T_PROMPT = BATCH * SEQ
T_SAMPLE = DEC_BATCH * DEC_SEQ
T_ALL = T_PROMPT + T_SAMPLE
N_COND = 1 + DEC_BATCH
TM = 1024
TM_SH = 512
TN_IN = 256
TN_OUT = 512
TN_MOD = 1024
TM_E = 256
N_ASSIGN = T_ALL * TOP_K
N_BLK_MAX = N_ASSIGN // TM_E + N_EXPERTS
LANES = 128
VMEM_LIMIT = 56 * 1024 * 1024


def _cparams(*sem):
    return pltpu.CompilerParams(dimension_semantics=sem, vmem_limit_bytes=VMEM_LIMIT)


def _cond_row(i, tm):
    return jnp.where(i < T_PROMPT // tm, 0, 1 + (i - T_PROMPT // tm) // (DEC_SEQ // tm))


def _mod_spec(which, tm, tn=D_MODEL):
    if tn == D_MODEL:
        return pl.BlockSpec((1, 1, D_MODEL), lambda i, *_: (_cond_row(i, tm) * 6 + which, 0, 0))
    return pl.BlockSpec((1, 1, tn), lambda i, j: (_cond_row(i, tm) * 6 + which, 0, j))


def _mod_body(c_ref, w_ref, b_ref, o_ref):
    c = c_ref[...]
    a = (c * jax.nn.sigmoid(c)).astype(BF16)
    o_ref[...] = jnp.dot(a, w_ref[...].astype(BF16), preferred_element_type=F32) + b_ref[0]


def modulation(cond, w_mod, b_mod, l):
    n = 6 * D_MODEL
    return pl.pallas_call(
        _mod_body,
        grid=(n // TN_MOD,),
        in_specs=[pl.BlockSpec((8, D_MODEL), lambda j: (0, 0)),
                  pl.BlockSpec((None, D_MODEL, TN_MOD), lambda j: (l, 0, j)),
                  pl.BlockSpec((None, 1, TN_MOD), lambda j: (l, 0, j))],
        out_specs=pl.BlockSpec((8, TN_MOD), lambda j: (0, j)),
        out_shape=jax.ShapeDtypeStruct((8, n), F32),
        compiler_params=_cparams("arbitrary"),
        name="modulation",
    )(cond, w_mod, b_mod.reshape(DEPTH, 1, n))


def _norm_mod(x, g, sc, sh):
    y = x * lax.rsqrt(jnp.mean(x * x, axis=-1, keepdims=True) + EPS) * g
    return y * (1.0 + sc) + sh


def _in_proj_body(x_ref, g_ref, sc_ref, sh_ref, w_ref, o_ref, hb_ref):
    @pl.when(pl.program_id(1) == 0)
    def _():
        hb_ref[...] = _norm_mod(x_ref[...], g_ref[0], sc_ref[0], sh_ref[0]).astype(BF16)

    o_ref[...] = jnp.dot(hb_ref[...], w_ref[...].astype(BF16), preferred_element_type=F32)


def in_proj(x, norm1, mod, w_in, l):
    return pl.pallas_call(
        _in_proj_body,
        grid=(T_ALL // TM, D_IN // TN_IN),
        in_specs=[pl.BlockSpec((TM, D_MODEL), lambda i, j: (i, 0)),
                  pl.BlockSpec((None, 1, D_MODEL), lambda i, j: (l, 0, 0)),
                  _mod_spec(1, TM), _mod_spec(0, TM),
                  pl.BlockSpec((None, D_MODEL, TN_IN), lambda i, j: (l, 0, j))],
        out_specs=pl.BlockSpec((TM, TN_IN), lambda i, j: (i, j)),
        out_shape=jax.ShapeDtypeStruct((T_ALL, D_IN), F32),
        scratch_shapes=[pltpu.VMEM((TM, D_MODEL), BF16)],
        compiler_params=_cparams("arbitrary", "arbitrary"),
        name="in_proj",
    )(x, norm1.reshape(DEPTH, 1, D_MODEL), mod, mod, w_in)


def _out_proj_body(m_ref, w_ref, x_ref, g1_ref, o_ref, mb_ref):
    @pl.when(pl.program_id(1) == 0)
    def _():
        mb_ref[...] = m_ref[...].astype(BF16)

    o_ref[...] = x_ref[...] + g1_ref[0] * jnp.dot(mb_ref[...], w_ref[...].astype(BF16), preferred_element_type=F32)


def out_proj(mix, w_out, x, mod, l):
    return pl.pallas_call(
        _out_proj_body,
        grid=(T_ALL // TM, D_MODEL // TN_OUT),
        in_specs=[pl.BlockSpec((TM, D_MIX), lambda i, j: (i, 0)),
                  pl.BlockSpec((None, D_MIX, TN_OUT), lambda i, j: (l, 0, j)),
                  pl.BlockSpec((TM, TN_OUT), lambda i, j: (i, j)),
                  _mod_spec(2, TM, TN_OUT)],
        out_specs=pl.BlockSpec((TM, TN_OUT), lambda i, j: (i, j)),
        out_shape=jax.ShapeDtypeStruct((T_ALL, D_MODEL), F32),
        scratch_shapes=[pltpu.VMEM((TM, D_MIX), BF16)],
        compiler_params=_cparams("arbitrary", "arbitrary"),
        name="out_proj",
    )(mix, w_out, x, mod)


def _router_body(x_ref, g_ref, sc_ref, sh_ref, wr_ref, rb_ref, h_ref, idx_ref, gate_ref, rank_ref, cnt_ref, carry_ref):
    i = pl.program_id(0)

    @pl.when(i == 0)
    def _():
        carry_ref[...] = jnp.zeros_like(carry_ref)

    h = _norm_mod(x_ref[...], g_ref[0], sc_ref[0], sh_ref[0])
    h_ref[...] = h.astype(BF16)
    logits = jnp.dot(h, wr_ref[...], preferred_element_type=F32, precision=lax.Precision.HIGHEST)
    s = jax.nn.sigmoid(logits)
    cur = s + rb_ref[0]
    e_iota = lax.broadcasted_iota(jnp.int32, s.shape, 1).astype(F32)
    lane = lax.broadcasted_iota(jnp.int32, (TM, LANES), 1)
    r_iota = lax.broadcasted_iota(jnp.int32, (TM, TM), 0)
    c_iota = lax.broadcasted_iota(jnp.int32, (TM, TM), 1)
    tri = (c_iota <= r_iota).astype(BF16)
    idx_out = jnp.zeros((TM, LANES), F32)
    gate_out = jnp.zeros((TM, LANES), F32)
    rank_out = jnp.zeros((TM, LANES), F32)
    gsum = jnp.zeros((TM, 1), F32)
    carry = carry_ref[...]
    for k in range(TOP_K):
        m = jnp.max(cur, axis=-1, keepdims=True)
        ek = jnp.min(jnp.where(cur == m, e_iota, float(N_EXPERTS)), axis=-1, keepdims=True)
        hit = e_iota == ek
        gk = jnp.sum(jnp.where(hit, s, 0.0), axis=-1, keepdims=True)
        cur = jnp.where(hit, -jnp.inf, cur)
        onehot = hit.astype(BF16)
        cum = jnp.dot(tri, onehot, preferred_element_type=F32)
        rk = jnp.sum(jnp.where(hit, carry + cum, 0.0), axis=-1, keepdims=True) - 1.0
        carry = carry + cum[TM - 1:TM, :]
        gsum = gsum + gk
        idx_out = jnp.where(lane == k, ek, idx_out)
        gate_out = jnp.where(lane == k, gk, gate_out)
        rank_out = jnp.where(lane == k, rk, rank_out)
    carry_ref[...] = carry
    idx_ref[...] = idx_out.astype(jnp.int32)
    gate_ref[...] = ROUTE_SCALE * gate_out / gsum
    rank_ref[...] = rank_out.astype(jnp.int32)
    cnt_ref[...] = carry.astype(jnp.int32)


def router(x1, norm2, mod, router_w, router_bias, l):
    tok_out = lambda dt: jax.ShapeDtypeStruct((T_ALL, LANES), dt)
    tok_spec = pl.BlockSpec((TM, LANES), lambda i: (i, 0))
    return pl.pallas_call(
        _router_body,
        grid=(T_ALL // TM,),
        in_specs=[pl.BlockSpec((TM, D_MODEL), lambda i: (i, 0)),
                  pl.BlockSpec((None, 1, D_MODEL), lambda i: (l, 0, 0)),
                  _mod_spec(4, TM), _mod_spec(3, TM),
                  pl.BlockSpec((None, D_MODEL, N_EXPERTS), lambda i: (l, 0, 0)),
                  pl.BlockSpec((None, 1, N_EXPERTS), lambda i: (l, 0, 0))],
        out_specs=[pl.BlockSpec((TM, D_MODEL), lambda i: (i, 0)), tok_spec, tok_spec, tok_spec,
                   pl.BlockSpec((1, N_EXPERTS), lambda i: (0, 0))],
        out_shape=[jax.ShapeDtypeStruct((T_ALL, D_MODEL), BF16), tok_out(jnp.int32), tok_out(F32), tok_out(jnp.int32),
                   jax.ShapeDtypeStruct((1, N_EXPERTS), jnp.int32)],
        scratch_shapes=[pltpu.VMEM((1, N_EXPERTS), F32)],
        compiler_params=_cparams("arbitrary"),
        name="router",
    )(x1, norm2.reshape(DEPTH, 1, D_MODEL), mod, mod, router_w, router_bias.reshape(DEPTH, 1, N_EXPERTS))


def _experts_body(be_ref, nv_ref, x_ref, wgu_ref, wdn_ref, y_ref, wgu_b, wdn_b):
    i = pl.program_id(0)

    @pl.when(i < nv_ref[0])
    def _():
        prev = be_ref[jnp.maximum(i - 1, 0)]

        @pl.when(jnp.logical_or(i == 0, be_ref[i] != prev))
        def _():
            wgu_b[...] = wgu_ref[...].astype(BF16)
            wdn_b[...] = wdn_ref[...].astype(BF16)

        hmid = jnp.dot(x_ref[...], wgu_b[...], preferred_element_type=F32)
        a, b = hmid[:, :EXPERT_FF], hmid[:, EXPERT_FF:]
        act = (a * jax.nn.sigmoid(a) * b).astype(BF16)
        y_ref[...] = jnp.dot(act, wdn_b[...], preferred_element_type=F32)


def routed_experts(xbuf, blk_e, n_valid, moe_w_gu, moe_w_dn, l):
    row = lambda i, be, nv: (jnp.minimum(i, nv[0] - 1), 0)
    return pl.pallas_call(
        _experts_body,
        grid_spec=pltpu.PrefetchScalarGridSpec(
            num_scalar_prefetch=2,
            grid=(N_BLK_MAX,),
            in_specs=[pl.BlockSpec((TM_E, D_MODEL), row),
                      pl.BlockSpec((None, None, D_MODEL, 2 * EXPERT_FF), lambda i, be, nv: (l, be[i], 0, 0)),
                      pl.BlockSpec((None, None, EXPERT_FF, D_MODEL), lambda i, be, nv: (l, be[i], 0, 0))],
            out_specs=pl.BlockSpec((TM_E, D_MODEL), row),
            scratch_shapes=[pltpu.VMEM((D_MODEL, 2 * EXPERT_FF), BF16), pltpu.VMEM((EXPERT_FF, D_MODEL), BF16)]),
        out_shape=jax.ShapeDtypeStruct((N_BLK_MAX * TM_E, D_MODEL), F32),
        compiler_params=_cparams("arbitrary"),
        name="routed_experts",
    )(blk_e, n_valid, xbuf, moe_w_gu, moe_w_dn)


def _shared_body(h_ref, wgu_ref, wdn_ref, moe_ref, x_ref, g2_ref, o_ref, wgu_b, wdn_b):
    @pl.when(pl.program_id(0) == 0)
    def _():
        wgu_b[...] = wgu_ref[...].astype(BF16)
        wdn_b[...] = wdn_ref[...].astype(BF16)

    hmid = jnp.dot(h_ref[...], wgu_b[...], preferred_element_type=F32)
    a, b = hmid[:, :SHARED_FF], hmid[:, SHARED_FF:]
    act = (a * jax.nn.sigmoid(a) * b).astype(BF16)
    sh = jnp.dot(act, wdn_b[...], preferred_element_type=F32)
    o_ref[...] = x_ref[...] + g2_ref[0] * (moe_ref[...] + sh)


def shared_combine(h2, sh_w_gu, sh_w_dn, moe_out, x1, mod, l):
    tile = pl.BlockSpec((TM_SH, D_MODEL), lambda i: (i, 0))
    once = pl.Buffered(1)
    return pl.pallas_call(
        _shared_body,
        grid=(T_ALL // TM_SH,),
        in_specs=[tile,
                  pl.BlockSpec((None, D_MODEL, 2 * SHARED_FF), lambda i: (l, 0, 0), pipeline_mode=once),
                  pl.BlockSpec((None, SHARED_FF, D_MODEL), lambda i: (l, 0, 0), pipeline_mode=once),
                  tile, tile, _mod_spec(5, TM_SH)],
        out_specs=tile,
        out_shape=jax.ShapeDtypeStruct((T_ALL, D_MODEL), F32),
        scratch_shapes=[pltpu.VMEM((D_MODEL, 2 * SHARED_FF), BF16), pltpu.VMEM((SHARED_FF, D_MODEL), BF16)],
        compiler_params=_cparams("arbitrary"),
        name="shared_combine",
    )(h2, sh_w_gu, sh_w_dn, moe_out, x1, mod)


def _final_norm_body(x_ref, g_ref, o_ref):
    x = x_ref[...]
    o_ref[...] = x * lax.rsqrt(jnp.mean(x * x, axis=-1, keepdims=True) + EPS) * g_ref[...]


def final_rmsnorm(x, g):
    return pl.pallas_call(
        _final_norm_body,
        grid=(T_ALL // TM,),
        in_specs=[pl.BlockSpec((TM, D_MODEL), lambda i: (i, 0)), pl.BlockSpec((1, D_MODEL), lambda i: (0, 0))],
        out_specs=pl.BlockSpec((TM, D_MODEL), lambda i: (i, 0)),
        out_shape=jax.ShapeDtypeStruct(x.shape, x.dtype),
        compiler_params=_cparams("arbitrary"),
        name="final_norm",
    )(x, g.reshape(1, D_MODEL))


def moe_layer(x1, mod, p, l):
    h2, idx, gates, rank, counts = router(x1, p['norm2'], mod, p['router_w'], p['router_bias'], l)
    idx, gates, rank = idx[:, :TOP_K], gates[:, :TOP_K], rank[:, :TOP_K]
    nblk = (counts[0] + TM_E - 1) // TM_E
    blk_end = jnp.cumsum(nblk)
    row_start = (blk_end - nblk) * TM_E
    dest = row_start[idx] + rank
    n_valid = blk_end[-1:].astype(jnp.int32)
    blk = jnp.minimum(jnp.arange(N_BLK_MAX, dtype=jnp.int32), n_valid[0] - 1)
    blk_e = jnp.minimum(jnp.searchsorted(blk_end, blk, side='right'), N_EXPERTS - 1).astype(jnp.int32)
    tok = jnp.broadcast_to(jnp.arange(T_ALL, dtype=jnp.int32)[:, None], dest.shape)
    src = jnp.zeros((N_BLK_MAX * TM_E,), jnp.int32).at[dest.reshape(-1)].set(tok.reshape(-1))
    xbuf = h2[src]
    ybuf = routed_experts(xbuf, blk_e, n_valid, p['moe_w_gu'], p['moe_w_dn'], l)
    moe_out = jnp.zeros((T_ALL, D_MODEL), F32)
    for k in range(TOP_K):
        moe_out = moe_out + gates[:, k:k + 1] * ybuf[dest[:, k]]
    return shared_combine(h2, p['sh_w_gu'], p['sh_w_dn'], moe_out, x1, mod, l)


def mixer_heads(proj, p, lam_init, ctx):
    B, L, _ = proj.shape
    dq, dk, dv, sq, sk, sv, hy, lx, lg = jnp.split(proj, np.cumsum(IN_SIZES)[:-1].tolist(), axis=-1)
    dq = dq.reshape(B, L, DIFF_HEADS, 2, DIFF_QK) * DIFF_SCALE
    dk = dk.reshape(B, L, DIFF_HEADS, 2, DIFF_QK)
    dv = dv.reshape(B, L, DIFF_HEADS, DIFF_VD)
    sq = sq.reshape(B, L, SWA_KV_HEADS, SWA_GROUP, HEAD_DIM) * SWA_SCALE
    sk = sk.reshape(B, L, SWA_KV_HEADS, HEAD_DIM)
    sv = sv.reshape(B, L, SWA_KV_HEADS, HEAD_DIM)
    lq1, lk1, lq2, lk2 = p['diff_lam'].astype(F32)
    lam = jnp.exp(jnp.sum(lq1 * lk1)) - jnp.exp(jnp.sum(lq2 * lk2)) + lam_init
    sink = p['swa_sink'].reshape(SWA_KV_HEADS, SWA_GROUP)
    if ctx is None:
        d_out = diff_attend(dq, dk, dv, lam)
        s_out = swa_dense(sq, sk, sv, sink)
        h0_f = jnp.zeros((B, LRU_W), F32)
        h0_b = jnp.zeros((B, LRU_W), F32)
    else:
        ck_d, cv_d, ck_s, cv_s, st = ctx
        cos, sin = axial_rope(L)
        dq_r = apply_rope(dq.reshape(B, L, 2 * DIFF_HEADS, DIFF_QK), cos, sin).reshape(dq.shape)
        dk_r = apply_rope(dk.reshape(B, L, 2 * DIFF_HEADS, DIFF_QK), cos, sin).reshape(dk.shape)
        d_out = diff_attend(dq_r, jnp.concatenate([dk_r, ck_d], axis=1), jnp.concatenate([dv, cv_d], axis=1), lam)
        sq_r = apply_rope(sq.reshape(B, L, SWA_HEADS, HEAD_DIM), cos, sin).reshape(sq.shape)
        s_out = swa_banded(sq_r, apply_rope(sk, cos, sin), sv, ck_s, cv_s, sink)
        h0_f, h0_b = st[:, 0], st[:, 1]
    d_out = (rmsnorm(d_out, p['diff_subln']) * (1.0 - lam_init)).reshape(B, L, DIFF_HEADS * DIFF_VD)
    hy_out = hyena_mix(hy, p)
    lru_in = dwconv(lx, p['lru_conv'], LRU_CONV // 2)
    hl, hf_last, hb_first = rglru_bidir(lru_in, p['lru_wa'], p['lru_ba'], p['lru_wx'], p['lru_bx'], p['lru_lam'], h0_f, h0_b)
    lru_out = hl.astype(proj.dtype) * jax.nn.gelu(lg)
    mix = jnp.concatenate([d_out, s_out, hy_out, lru_out], axis=-1).reshape(B * L, D_MIX)
    if ctx is None:
        return mix, (dk, dv, sk, sv, jnp.stack([hf_last, hb_first], axis=1).astype(proj.dtype))
    return mix, None


def kernel(x_prompt, x_sample, cache_diff_k, cache_diff_v, cache_swa_k, cache_swa_v, state_lru, c, c_ctx,
           w_mod, b_mod, norm1, norm2, w_in, w_out, diff_lam, diff_subln, swa_sink, hy_conv, hy_w1, hy_b1,
           hy_w2, hy_b2, hy_w3, hy_freq, hy_log_rate, hy_skip, lru_conv, lru_wa, lru_ba, lru_wx, lru_bx, lru_lam,
           router_w, router_bias, moe_w_gu, moe_w_dn, sh_w_gu, sh_w_dn, final_norm):
    x = jnp.concatenate([x_prompt.reshape(T_PROMPT, D_MODEL), x_sample.reshape(T_SAMPLE, D_MODEL)], axis=0)
    cond = jnp.concatenate([c_ctx[None, :], c, jnp.zeros((8 - N_COND, D_MODEL), F32)], axis=0)
    dk_l, dv_l, sk_l, sv_l, st_l = [], [], [], [], []
    for l in range(DEPTH):
        p = {'diff_lam': diff_lam[l], 'diff_subln': diff_subln[l], 'swa_sink': swa_sink[l],
             'hy_conv': hy_conv[l], 'hy_w1': hy_w1[l], 'hy_b1': hy_b1[l], 'hy_w2': hy_w2[l], 'hy_b2': hy_b2[l],
             'hy_w3': hy_w3[l], 'hy_freq': hy_freq[l], 'hy_log_rate': hy_log_rate[l], 'hy_skip': hy_skip[l],
             'lru_conv': lru_conv[l], 'lru_wa': lru_wa[l], 'lru_ba': lru_ba[l], 'lru_wx': lru_wx[l],
             'lru_bx': lru_bx[l], 'lru_lam': lru_lam[l],
             'norm2': norm2, 'router_w': router_w, 'router_bias': router_bias, 'moe_w_gu': moe_w_gu,
             'moe_w_dn': moe_w_dn, 'sh_w_gu': sh_w_gu, 'sh_w_dn': sh_w_dn}
        lam_init = 0.8 - 0.6 * math.exp(-0.3 * l)
        mod = modulation(cond, w_mod, b_mod, l)[:N_COND].reshape(N_COND * 6, 1, D_MODEL)
        proj = in_proj(x, norm1, mod, w_in, l)
        mix_p, (k_d, v_d, k_s, v_s, st) = mixer_heads(proj[:T_PROMPT].reshape(BATCH, SEQ, D_IN), p, lam_init, None)
        dk_l.append(k_d); dv_l.append(v_d); sk_l.append(k_s); sv_l.append(v_s); st_l.append(st)
        ctx = (cache_diff_k[:, l], cache_diff_v[:, l], cache_swa_k[:, l], cache_swa_v[:, l], state_lru[:, l])
        mix_s, _ = mixer_heads(proj[T_PROMPT:].reshape(DEC_BATCH, DEC_SEQ, D_IN), p, lam_init, ctx)
        x1 = out_proj(jnp.concatenate([mix_p, mix_s], axis=0), w_out, x, mod, l)
        x = moe_layer(x1, mod, p, l)
    y = final_rmsnorm(x, final_norm)
    y_prompt = y[:T_PROMPT].reshape(BATCH, SEQ, D_MODEL)
    y_sample = y[T_PROMPT:].reshape(DEC_BATCH, DEC_SEQ, D_MODEL)
    return (y_prompt, y_sample, jnp.stack(dk_l, axis=1), jnp.stack(dv_l, axis=1), jnp.stack(sk_l, axis=1),
            jnp.stack(sv_l, axis=1), jnp.stack(st_l, axis=1))
```

```python
import math
from functools import partial
import jax, jax.numpy as jnp
from jax import lax
import numpy as np
from jax.experimental import pallas as pl
from jax.experimental.pallas import tpu as pltpu

D_MODEL = 2048
BATCH = 16
SEQ = 256
DEPTH = 2
DEC_BATCH = 2
DEC_SEQ = 4096
PAST_LEN = 256

GRID_W = 64
EPS = 1e-6
F32 = jnp.float32
QBLK = 128
GROUP_W = D_MODEL // 4
D_MIX = 4 * GROUP_W
DIFF_QK = 64
DIFF_VD = 2 * DIFF_QK
DIFF_HEADS = GROUP_W // DIFF_VD
DIFF_SCALE = DIFF_QK ** -0.5
HEAD_DIM = 64
SWA_HEADS = GROUP_W // HEAD_DIM
SWA_KV_HEADS = SWA_HEADS // 4
SWA_GROUP = SWA_HEADS // SWA_KV_HEADS
SWA_SCALE = HEAD_DIM ** -0.5
WINDOW = 128
ROPE_PAIRS = HEAD_DIM // 4
ROPE_BASE = 10000.0
HY_W = GROUP_W
HY_ORDER = 2
HY_SHORT = 3
HY_BANDS = 16
HY_POS_DIM = 1 + 2 * HY_BANDS
HY_HID = 64
LRU_W = GROUP_W
LRU_BLOCKS = 8
LRU_BD = LRU_W // LRU_BLOCKS
LRU_CONV = 4
LRU_C = 8.0
N_EXPERTS = 64
TOP_K = 6
EXPERT_FF = D_MODEL // 4
SHARED_FF = EXPERT_FF
ROUTE_SCALE = 2.5
MOE_BLK = 128
NEG_INF = -1e30
IN_SIZES = (2 * DIFF_HEADS * DIFF_QK, 2 * DIFF_HEADS * DIFF_QK, DIFF_HEADS * DIFF_VD, SWA_HEADS * HEAD_DIM, SWA_KV_HEADS * HEAD_DIM, SWA_KV_HEADS * HEAD_DIM, (HY_ORDER + 1) * HY_W, LRU_W, LRU_W)
D_IN = sum(IN_SIZES)


def rmsnorm(x, g):
    xf = x.astype(F32)
    y = xf * lax.rsqrt(jnp.mean(xf * xf, axis=-1, keepdims=True) + EPS)
    return (y * g.astype(F32)).astype(x.dtype)


def dwconv(x, w, left):
    K, L = w.shape[0], x.shape[1]
    xp = jnp.pad(x, ((0, 0), (left, K - 1 - left), (0, 0)))
    acc = xp[:, 0:L] * w[0]
    for k in range(1, K):
        acc = acc + xp[:, k:k + L] * w[k]
    return acc


def axial_rope(L):
    rows = L // GRID_W
    row = jnp.repeat(jnp.arange(rows), GRID_W).astype(F32)
    col = jnp.tile(jnp.arange(GRID_W), rows).astype(F32)
    inv = ROPE_BASE ** (-jnp.arange(ROPE_PAIRS, dtype=F32) / ROPE_PAIRS)
    ang = jnp.stack([row[:, None] * inv, col[:, None] * inv], axis=1)
    return jnp.cos(ang), jnp.sin(ang)


def apply_rope(x, cos, sin):
    xs = x.reshape(*x.shape[:-1], 2, 2, ROPE_PAIRS)
    c = cos[:, None, :, None, :].astype(x.dtype)
    s = sin[:, None, :, None, :].astype(x.dtype)
    x1, x2 = xs[..., 0:1, :], xs[..., 1:2, :]
    return jnp.concatenate([x1 * c - x2 * s, x2 * c + x1 * s], axis=-2).reshape(x.shape)


def diff_attend(q, k, v, lam):
    B, Lq = q.shape[:2]
    nb = Lq // QBLK
    qb = jnp.moveaxis(q.reshape(B, nb, QBLK, *q.shape[2:]), 1, 0)

    def one(qblk):
        s = jnp.einsum('bqhmd,bkhmd->bhmqk', qblk, k, preferred_element_type=F32)
        p = jax.nn.softmax(s, axis=-1)
        w = p[:, :, 0] - lam * p[:, :, 1]
        return jnp.einsum('bhqk,bkhd->bqhd', w.astype(v.dtype), v)

    o = lax.map(one, qb)
    return jnp.moveaxis(o, 0, 1).reshape(B, Lq, q.shape[2], v.shape[-1])


def sink_softmax(s, sink):
    sb = sink.astype(F32)[None, :, :, None, None]
    m = jnp.maximum(jnp.max(s, axis=-1, keepdims=True), sb)
    e = jnp.exp(s - m)
    return e / (jnp.sum(e, axis=-1, keepdims=True) + jnp.exp(sb - m))


def swa_dense(q, k, v, sink):
    B, L = q.shape[:2]
    nb = L // QBLK
    qb = jnp.moveaxis(q.reshape(B, nb, QBLK, *q.shape[2:]), 1, 0)

    def one(qblk):
        p = sink_softmax(jnp.einsum('bqkgd,bnkd->bkgqn', qblk, k, preferred_element_type=F32), sink)
        return jnp.einsum('bkgqn,bnkd->bqkgd', p.astype(v.dtype), v)

    o = lax.map(one, qb)
    return jnp.moveaxis(o, 0, 1).reshape(B, L, SWA_HEADS * HEAD_DIM)


def swa_banded(q, k, v, k_ctx, v_ctx, sink):
    B, L = q.shape[:2]
    nb = L // WINDOW
    pad = lambda t: jnp.pad(t, ((0, 0), (WINDOW, WINDOW), (0, 0), (0, 0))).reshape(B, nb + 2, WINDOW, *t.shape[2:])
    kb, vb = pad(k), pad(v)
    kwin = jnp.concatenate([kb[:, :nb], kb[:, 1:nb + 1], kb[:, 2:]], axis=2)
    vwin = jnp.concatenate([vb[:, :nb], vb[:, 1:nb + 1], vb[:, 2:]], axis=2)
    qi = jnp.arange(WINDOW)[:, None]
    kj = jnp.arange(3 * WINDOW)[None, :]
    rel = kj - qi
    band = (rel >= 0) & (rel <= 2 * WINDOW)
    qb = jnp.moveaxis(q.reshape(B, nb, WINDOW, *q.shape[2:]), 1, 0)

    def one(args):
        b, qblk, kw, vw = args
        kpos = (b - 1) * WINDOW + kj
        mask = band & (kpos >= 0) & (kpos < L)
        s_loc = jnp.where(mask, jnp.einsum('bqkgd,bnkd->bkgqn', qblk, kw, preferred_element_type=F32), NEG_INF)
        s_ctx = jnp.einsum('bqkgd,bnkd->bkgqn', qblk, k_ctx, preferred_element_type=F32)
        p = sink_softmax(jnp.concatenate([s_loc, s_ctx], axis=-1), sink).astype(v.dtype)
        return (jnp.einsum('bkgqn,bnkd->bqkgd', p[..., :3 * WINDOW], vw)
                + jnp.einsum('bkgqn,bnkd->bqkgd', p[..., 3 * WINDOW:], v_ctx))

    o = lax.map(one, (jnp.arange(nb), qb, jnp.moveaxis(kwin, 1, 0), jnp.moveaxis(vwin, 1, 0)))
    return jnp.moveaxis(o, 0, 1).reshape(B, L, SWA_HEADS * HEAD_DIM)


def hyena_filters(L, w1, b1, w2, b2, w3, freq, log_rate):
    tn = jnp.arange(L, dtype=F32) / L
    ang = 2.0 * math.pi * tn[:, None] * jnp.arange(1, HY_BANDS + 1, dtype=F32)
    z = jnp.concatenate([tn[:, None], jnp.sin(ang), jnp.cos(ang)], axis=-1)
    f = jnp.sin(freq[0].astype(F32) * (z @ w1.astype(F32) + b1.astype(F32)))
    f = jnp.sin(freq[1].astype(F32) * (f @ w2.astype(F32) + b2.astype(F32)))
    f = (f @ w3.astype(F32)).reshape(L, HY_ORDER, 2, HY_W)
    f = f * jnp.exp(-tn[:, None, None, None] * jnp.exp(log_rate.astype(F32)))
    return f * lax.rsqrt(jnp.sum(f * f, axis=(0, 2), keepdims=True) + EPS)


def bidir_fftconv(u, kf, kb):
    L, C = u.shape[1], u.shape[2]
    kfull = jnp.concatenate([kf.at[0].add(kb[0]), jnp.zeros((1, C), F32), kb[1:][::-1]], axis=0)
    Kf = jnp.fft.rfft(kfull, axis=0)
    Uf = jnp.fft.rfft(u, n=2 * L, axis=1)
    return jnp.fft.irfft(Uf * Kf[None], n=2 * L, axis=1)[:, :L]


def hyena_mix(u, p):
    L = u.shape[1]
    u = dwconv(u, p['hy_conv'], HY_SHORT // 2)
    parts = jnp.split(u, HY_ORDER + 1, axis=-1)
    filt = hyena_filters(L, p['hy_w1'], p['hy_b1'], p['hy_w2'], p['hy_b2'], p['hy_w3'], p['hy_freq'], p['hy_log_rate'])
    skip = p['hy_skip'].astype(F32)
    z = parts[0].astype(F32)
    for o in range(HY_ORDER):
        z = parts[o + 1].astype(F32) * (bidir_fftconv(z, filt[:, o, 0], filt[:, o, 1]) + skip[o] * z)
    return z.astype(u.dtype)


def _lin_combine(e1, e2):
    a1, b1 = e1
    a2, b2 = e2
    return a1 * a2, a2 * b1 + b2


def rglru_bidir(x, wa, ba, wx, bx, lam, h0_f, h0_b):
    B, L, W = x.shape
    xb = x.reshape(B, L, LRU_BLOCKS, LRU_BD)
    hs = []
    for d, (h0, rev) in enumerate(((h0_f, False), (h0_b, True))):
        r = jax.nn.sigmoid(jnp.einsum('blnd,nde->blne', xb, wa[d]).reshape(B, L, W) + ba[d])
        i = jax.nn.sigmoid(jnp.einsum('blnd,nde->blne', xb, wx[d]).reshape(B, L, W) + bx[d])
        log_a = LRU_C * r.astype(F32) * jax.nn.log_sigmoid(lam[d].astype(F32))
        a = jnp.exp(log_a)
        bterm = jnp.sqrt(-jnp.expm1(2.0 * log_a)) * (i * x).astype(F32)
        first = L - 1 if rev else 0
        bterm = bterm.at[:, first].add(a[:, first] * h0.astype(F32))
        _, h = lax.associative_scan(_lin_combine, (a, bterm), axis=1, reverse=rev)
        hs.append(h)
    return hs[0] + hs[1], hs[0][:, -1], hs[1][:, 0]


BF16 = jnp.bfloat16
T_PROMPT = BATCH * SEQ
T_SAMPLE = DEC_BATCH * DEC_SEQ
T_ALL = T_PROMPT + T_SAMPLE
N_COND = 1 + DEC_BATCH
TM = 1024
TM_SH = 256
TN_IN = 256
TN_OUT = 512
TN_MOD = 1024
TM_E = 256
N_ASSIGN = T_ALL * TOP_K
N_BLK_MAX = N_ASSIGN // TM_E + N_EXPERTS
LANES = 128
VMEM_LIMIT = 56 * 1024 * 1024


def _cparams(*sem):
    return pltpu.CompilerParams(dimension_semantics=sem, vmem_limit_bytes=VMEM_LIMIT)


def _cond_row(i, tm):
    return jnp.where(i < T_PROMPT // tm, 0, 1 + (i - T_PROMPT // tm) // (DEC_SEQ // tm))


def _mod_spec(which, tm, tn=D_MODEL):
    if tn == D_MODEL:
        return pl.BlockSpec((1, 1, D_MODEL), lambda i, *_: (_cond_row(i, tm) * 6 + which, 0, 0))
    return pl.BlockSpec((1, 1, tn), lambda i, j: (_cond_row(i, tm) * 6 + which, 0, j))


def _mod_body(c_ref, w_ref, b_ref, o_ref):
    c = c_ref[...]
    a = (c * jax.nn.sigmoid(c)).astype(BF16)
    o_ref[...] = jnp.dot(a, w_ref[...].astype(BF16), preferred_element_type=F32) + b_ref[0]


def modulation(cond, w_mod, b_mod, l):
    n = 6 * D_MODEL
    return pl.pallas_call(
        _mod_body,
        grid=(n // TN_MOD,),
        in_specs=[pl.BlockSpec((8, D_MODEL), lambda j: (0, 0)),
                  pl.BlockSpec((None, D_MODEL, TN_MOD), lambda j: (l, 0, j)),
                  pl.BlockSpec((None, 1, TN_MOD), lambda j: (l, 0, j))],
        out_specs=pl.BlockSpec((8, TN_MOD), lambda j: (0, j)),
        out_shape=jax.ShapeDtypeStruct((8, n), F32),
        compiler_params=_cparams("arbitrary"),
        name="modulation",
    )(cond, w_mod, b_mod.reshape(DEPTH, 1, n))


def _norm_mod(x, g, sc, sh):
    y = x * lax.rsqrt(jnp.mean(x * x, axis=-1, keepdims=True) + EPS) * g
    return y * (1.0 + sc) + sh


def _in_proj_body(x_ref, g_ref, sc_ref, sh_ref, w_ref, o_ref, hb_ref):
    @pl.when(pl.program_id(1) == 0)
    def _():
        hb_ref[...] = _norm_mod(x_ref[...], g_ref[0], sc_ref[0], sh_ref[0]).astype(BF16)

    o_ref[...] = jnp.dot(hb_ref[...], w_ref[...].astype(BF16), preferred_element_type=F32)


def in_proj(x, norm1, mod, w_in, l):
    return pl.pallas_call(
        _in_proj_body,
        grid=(T_ALL // TM, D_IN // TN_IN),
        in_specs=[pl.BlockSpec((TM, D_MODEL), lambda i, j: (i, 0)),
                  pl.BlockSpec((None, 1, D_MODEL), lambda i, j: (l, 0, 0)),
                  _mod_spec(1, TM), _mod_spec(0, TM),
                  pl.BlockSpec((None, D_MODEL, TN_IN), lambda i, j: (l, 0, j))],
        out_specs=pl.BlockSpec((TM, TN_IN), lambda i, j: (i, j)),
        out_shape=jax.ShapeDtypeStruct((T_ALL, D_IN), F32),
        scratch_shapes=[pltpu.VMEM((TM, D_MODEL), BF16)],
        compiler_params=_cparams("arbitrary", "arbitrary"),
        name="in_proj",
    )(x, norm1.reshape(DEPTH, 1, D_MODEL), mod, mod, w_in)


N_MIX = D_MIX // GROUP_W


def _out_proj_body(*refs):
    m_refs, (w_ref, x_ref, g1_ref, o_ref, mb_ref) = refs[:N_MIX], refs[N_MIX:]

    @pl.when(pl.program_id(1) == 0)
    def _():
        for g in range(N_MIX):
            mb_ref[g] = m_refs[g][...].astype(BF16)

    acc = jnp.dot(mb_ref[0], w_ref[0:GROUP_W, :].astype(BF16), preferred_element_type=F32)
    for g in range(1, N_MIX):
        acc = acc + jnp.dot(mb_ref[g], w_ref[g * GROUP_W:(g + 1) * GROUP_W, :].astype(BF16), preferred_element_type=F32)
    o_ref[...] = x_ref[...] + g1_ref[0] * acc


def out_proj(mixes, w_out, x, mod, l):
    return pl.pallas_call(
        _out_proj_body,
        grid=(T_ALL // TM, D_MODEL // TN_OUT),
        in_specs=[pl.BlockSpec((TM, GROUP_W), lambda i, j: (i, 0))] * N_MIX + [
                  pl.BlockSpec((None, D_MIX, TN_OUT), lambda i, j: (l, 0, j)),
                  pl.BlockSpec((TM, TN_OUT), lambda i, j: (i, j)),
                  _mod_spec(2, TM, TN_OUT)],
        out_specs=pl.BlockSpec((TM, TN_OUT), lambda i, j: (i, j)),
        out_shape=jax.ShapeDtypeStruct((T_ALL, D_MODEL), F32),
        scratch_shapes=[pltpu.VMEM((N_MIX, TM, GROUP_W), BF16)],
        compiler_params=_cparams("arbitrary", "arbitrary"),
        name="out_proj",
    )(*mixes, w_out, x, mod)


def _router_body(x_ref, g_ref, sc_ref, sh_ref, wr_ref, rb_ref, h_ref, idx_ref, gate_ref, rank_ref, cnt_ref, carry_ref):
    i = pl.program_id(0)

    @pl.when(i == 0)
    def _():
        carry_ref[...] = jnp.zeros_like(carry_ref)

    h = _norm_mod(x_ref[...], g_ref[0], sc_ref[0], sh_ref[0])
    h_ref[...] = h.astype(BF16)
    logits = jnp.dot(h, wr_ref[...], preferred_element_type=F32, precision=lax.Precision.HIGHEST)
    s = jax.nn.sigmoid(logits)
    cur = s + rb_ref[0]
    e_iota = lax.broadcasted_iota(jnp.int32, s.shape, 1).astype(F32)
    lane = lax.broadcasted_iota(jnp.int32, (TM, LANES), 1)
    r_iota = lax.broadcasted_iota(jnp.int32, (TM, TM), 0)
    c_iota = lax.broadcasted_iota(jnp.int32, (TM, TM), 1)
    tri = (c_iota <= r_iota).astype(BF16)
    idx_out = jnp.zeros((TM, LANES), F32)
    gate_out = jnp.zeros((TM, LANES), F32)
    rank_out = jnp.zeros((TM, LANES), F32)
    gsum = jnp.zeros((TM, 1), F32)
    carry = carry_ref[...]
    for k in range(TOP_K):
        m = jnp.max(cur, axis=-1, keepdims=True)
        ek = jnp.min(jnp.where(cur == m, e_iota, float(N_EXPERTS)), axis=-1, keepdims=True)
        hit = e_iota == ek
        gk = jnp.sum(jnp.where(hit, s, 0.0), axis=-1, keepdims=True)
        cur = jnp.where(hit, -jnp.inf, cur)
        onehot = hit.astype(BF16)
        cum = jnp.dot(tri, onehot, preferred_element_type=F32)
        rk = jnp.sum(jnp.where(hit, carry + cum, 0.0), axis=-1, keepdims=True) - 1.0
        carry = carry + cum[TM - 1:TM, :]
        gsum = gsum + gk
        idx_out = jnp.where(lane == k, ek, idx_out)
        gate_out = jnp.where(lane == k, gk, gate_out)
        rank_out = jnp.where(lane == k, rk, rank_out)
    carry_ref[...] = carry
    idx_ref[...] = idx_out.astype(jnp.int32)
    gate_ref[...] = ROUTE_SCALE * gate_out / gsum
    rank_ref[...] = rank_out.astype(jnp.int32)
    cnt_ref[...] = carry.astype(jnp.int32)


def router(x1, norm2, mod, router_w, router_bias, l):
    tok_out = lambda dt: jax.ShapeDtypeStruct((T_ALL, LANES), dt)
    tok_spec = pl.BlockSpec((TM, LANES), lambda i: (i, 0))
    return pl.pallas_call(
        _router_body,
        grid=(T_ALL // TM,),
        in_specs=[pl.BlockSpec((TM, D_MODEL), lambda i: (i, 0)),
                  pl.BlockSpec((None, 1, D_MODEL), lambda i: (l, 0, 0)),
                  _mod_spec(4, TM), _mod_spec(3, TM),
                  pl.BlockSpec((None, D_MODEL, N_EXPERTS), lambda i: (l, 0, 0)),
                  pl.BlockSpec((None, 1, N_EXPERTS), lambda i: (l, 0, 0))],
        out_specs=[pl.BlockSpec((TM, D_MODEL), lambda i: (i, 0)), tok_spec, tok_spec, tok_spec,
                   pl.BlockSpec((1, N_EXPERTS), lambda i: (0, 0))],
        out_shape=[jax.ShapeDtypeStruct((T_ALL, D_MODEL), BF16), tok_out(jnp.int32), tok_out(F32), tok_out(jnp.int32),
                   jax.ShapeDtypeStruct((1, N_EXPERTS), jnp.int32)],
        scratch_shapes=[pltpu.VMEM((1, N_EXPERTS), F32)],
        compiler_params=_cparams("arbitrary"),
        name="router",
    )(x1, norm2.reshape(DEPTH, 1, D_MODEL), mod, mod, router_w, router_bias.reshape(DEPTH, 1, N_EXPERTS))


def _experts_body(be_ref, nv_ref, x_ref, wgu_ref, wdn_ref, y_ref, wgu_b, wdn_b):
    i = pl.program_id(0)

    @pl.when(i < nv_ref[0])
    def _():
        prev = be_ref[jnp.maximum(i - 1, 0)]

        @pl.when(jnp.logical_or(i == 0, be_ref[i] != prev))
        def _():
            wgu_b[...] = wgu_ref[...].astype(BF16)
            wdn_b[...] = wdn_ref[...].astype(BF16)

        hmid = jnp.dot(x_ref[...], wgu_b[...], preferred_element_type=F32)
        a, b = hmid[:, :EXPERT_FF], hmid[:, EXPERT_FF:]
        act = (a * jax.nn.sigmoid(a) * b).astype(BF16)
        y_ref[...] = jnp.dot(act, wdn_b[...], preferred_element_type=F32)

    @pl.when(i >= nv_ref[0])
    def _():
        y_ref[...] = jnp.zeros_like(y_ref)


def routed_experts(xbuf, blk_e, n_valid, moe_w_gu, moe_w_dn, l):
    row = lambda i, be, nv: (jnp.minimum(i, nv[0] - 1), 0)
    return pl.pallas_call(
        _experts_body,
        grid_spec=pltpu.PrefetchScalarGridSpec(
            num_scalar_prefetch=2,
            grid=(N_BLK_MAX,),
            in_specs=[pl.BlockSpec((TM_E, D_MODEL), row),
                      pl.BlockSpec((None, None, D_MODEL, 2 * EXPERT_FF), lambda i, be, nv: (l, be[i], 0, 0)),
                      pl.BlockSpec((None, None, EXPERT_FF, D_MODEL), lambda i, be, nv: (l, be[i], 0, 0))],
            out_specs=pl.BlockSpec((TM_E, D_MODEL), lambda i, be, nv: (i, 0)),
            scratch_shapes=[pltpu.VMEM((D_MODEL, 2 * EXPERT_FF), BF16), pltpu.VMEM((EXPERT_FF, D_MODEL), BF16)]),
        out_shape=jax.ShapeDtypeStruct((N_BLK_MAX * TM_E, D_MODEL), F32),
        compiler_params=_cparams("arbitrary"),
        name="routed_experts",
    )(blk_e, n_valid, xbuf, moe_w_gu, moe_w_dn)


def _gathered_rows(ybuf_hbm, rows, sem, n):
    return pltpu.make_async_copy(ybuf_hbm.at[pl.ds(0, n)], rows.at[pl.ds(0, n)], sem)


def _shared_body(dest_ref, gate_ref, h_ref, wgu_ref, wdn_ref, x_ref, g2_ref, ybuf_hbm, o_ref, wgu_b, wdn_b, rows, sem):
    @pl.when(pl.program_id(0) == 0)
    def _():
        wgu_b[...] = wgu_ref[...].astype(BF16)
        wdn_b[...] = wdn_ref[...].astype(BF16)

    def issue(t, carry):
        for k in range(TOP_K):
            src = dest_ref[0, t * TOP_K + k]
            pltpu.make_async_copy(ybuf_hbm.at[pl.ds(src, 1)], rows.at[pl.ds(k * TM_SH + t, 1)], sem).start()
        return carry

    lax.fori_loop(0, TM_SH, issue, 0)
    hmid = jnp.dot(h_ref[...], wgu_b[...], preferred_element_type=F32)
    a, b = hmid[:, :SHARED_FF], hmid[:, SHARED_FF:]
    act = (a * jax.nn.sigmoid(a) * b).astype(BF16)
    y = jnp.dot(act, wdn_b[...], preferred_element_type=F32)
    _gathered_rows(ybuf_hbm, rows, sem, TOP_K * TM_SH).wait()
    gates = gate_ref[...]
    for k in range(TOP_K):
        y = y + gates[:, k:k + 1] * rows[k * TM_SH:(k + 1) * TM_SH, :]
    o_ref[...] = x_ref[...] + g2_ref[0] * y


def shared_combine(h2, sh_w_gu, sh_w_dn, ybuf, dest, gates, x1, mod, l):
    tile = pl.BlockSpec((TM_SH, D_MODEL), lambda i: (i, 0))
    once = pl.Buffered(1)
    n_tiles = T_ALL // TM_SH
    return pl.pallas_call(
        _shared_body,
        grid=(n_tiles,),
        in_specs=[pl.BlockSpec((None, 1, TM_SH * TOP_K), lambda i: (i, 0, 0), memory_space=pltpu.SMEM),
                  pl.BlockSpec((TM_SH, LANES), lambda i: (i, 0)),
                  tile,
                  pl.BlockSpec((None, D_MODEL, 2 * SHARED_FF), lambda i: (l, 0, 0), pipeline_mode=once),
                  pl.BlockSpec((None, SHARED_FF, D_MODEL), lambda i: (l, 0, 0), pipeline_mode=once),
                  tile, _mod_spec(5, TM_SH),
                  pl.BlockSpec(memory_space=pl.ANY)],
        out_specs=tile,
        out_shape=jax.ShapeDtypeStruct((T_ALL, D_MODEL), F32),
        scratch_shapes=[pltpu.VMEM((D_MODEL, 2 * SHARED_FF), BF16), pltpu.VMEM((SHARED_FF, D_MODEL), BF16),
                        pltpu.VMEM((TOP_K * TM_SH, D_MODEL), F32), pltpu.SemaphoreType.DMA(())],
        compiler_params=_cparams("arbitrary"),
        name="shared_combine",
    )(dest.reshape(n_tiles, 1, TM_SH * TOP_K), gates, h2, sh_w_gu, sh_w_dn, x1, mod, ybuf)


def _final_norm_body(x_ref, g_ref, o_ref):
    x = x_ref[...]
    o_ref[...] = x * lax.rsqrt(jnp.mean(x * x, axis=-1, keepdims=True) + EPS) * g_ref[...]


def final_rmsnorm(x, g):
    return pl.pallas_call(
        _final_norm_body,
        grid=(T_ALL // TM,),
        in_specs=[pl.BlockSpec((TM, D_MODEL), lambda i: (i, 0)), pl.BlockSpec((1, D_MODEL), lambda i: (0, 0))],
        out_specs=pl.BlockSpec((TM, D_MODEL), lambda i: (i, 0)),
        out_shape=jax.ShapeDtypeStruct(x.shape, x.dtype),
        compiler_params=_cparams("arbitrary"),
        name="final_norm",
    )(x, g.reshape(1, D_MODEL))


TQ_DIFF = 256
HALF = LANES // 2
NT_DIMS = (((1,), (1,)), ((), ()))


def rope_tables(L):
    pos = jnp.arange(L)
    d = jnp.arange(LANES) % HEAD_DIM
    p = jnp.where(d // (2 * ROPE_PAIRS) == 0, (pos // GRID_W)[:, None], (pos % GRID_W)[:, None]).astype(F32)
    inv = ROPE_BASE ** (-(d % ROPE_PAIRS).astype(F32) / ROPE_PAIRS)
    ang = p * inv
    return jnp.cos(ang), jnp.where((d // ROPE_PAIRS) % 2 == 0, -jnp.sin(ang), jnp.sin(ang))


def _rope(x, cos, sin_signed):
    lane = lax.broadcasted_iota(jnp.int32, x.shape, 1)
    is_x1 = (lane // ROPE_PAIRS) % 2 == 0
    partner = jnp.where(is_x1, pltpu.roll(x, LANES - ROPE_PAIRS, 1), pltpu.roll(x, ROPE_PAIRS, 1))
    return x * cos + partner * sin_signed


def _softmax(s):
    e = jnp.exp(s - jnp.max(s, axis=-1, keepdims=True))
    return e / jnp.sum(e, axis=-1, keepdims=True)


def _diff_body(*refs, lk, ctx, lam_init):
    *ins, _aliased_dst, o_ref, kb, vb = refs
    if ctx:
        q_ref, k_ref, v_ref, dl_ref, g_ref, ck_ref, cv_ref, cq_ref, sq_ref, ckk_ref, skk_ref = ins
    else:
        q_ref, k_ref, v_ref, dl_ref, g_ref = ins

    @pl.when(pl.program_id(2) == 0)
    def _():
        k = k_ref[...]
        if ctx:
            k = _rope(k, ckk_ref[...], skk_ref[...])
            kb[lk:, :] = ck_ref[...].astype(BF16)
            vb[lk:, :] = cv_ref[...].astype(BF16)
        kb[0:lk, :] = k.astype(BF16)
        vb[0:lk, :] = v_ref[...].astype(BF16)

    q = q_ref[...] * DIFF_SCALE
    if ctx:
        q = _rope(q, cq_ref[...], sq_ref[...])
    lane = lax.broadcasted_iota(jnp.int32, q.shape, 1)
    q1 = jnp.where(lane < HALF, q, 0.0).astype(BF16)
    q2 = jnp.where(lane >= HALF, q, 0.0).astype(BF16)
    keys = kb[...]
    p1 = _softmax(lax.dot_general(q1, keys, NT_DIMS, preferred_element_type=F32))
    p2 = _softmax(lax.dot_general(q2, keys, NT_DIMS, preferred_element_type=F32))
    dl = dl_ref[...]
    lam = (jnp.exp(jnp.sum(dl[0:1] * dl[1:2], axis=-1, keepdims=True))
           - jnp.exp(jnp.sum(dl[2:3] * dl[3:4], axis=-1, keepdims=True)) + lam_init)
    o = jnp.dot((p1 - lam * p2).astype(BF16), vb[...], preferred_element_type=F32)
    o = o * lax.rsqrt(jnp.mean(o * o, axis=-1, keepdims=True) + EPS) * g_ref[...]
    o_ref[...] = o * (1.0 - lam_init)


def diff_attention(proj, dst, diff_lam, diff_subln, l, lam_init, cache_k=None, cache_v=None, tables=None):
    ctx = cache_k is not None
    nh = DIFF_HEADS
    if ctx:
        nb, lk, tq, row0 = DEC_BATCH, DEC_SEQ, TQ_DIFF, T_PROMPT
    else:
        nb, lk, tq, row0 = BATCH, SEQ, SEQ, 0
    in_specs = [pl.BlockSpec((tq, LANES), lambda b, h, qi: ((row0 + b * lk) // tq + qi, h)),
                pl.BlockSpec((lk, LANES), lambda b, h, qi: (row0 // lk + b, nh + h)),
                pl.BlockSpec((lk, LANES), lambda b, h, qi: (row0 // lk + b, 2 * nh + h)),
                pl.BlockSpec((None, 4, DIFF_QK), lambda b, h, qi: (l, 0, 0)),
                pl.BlockSpec((None, 1, DIFF_VD), lambda b, h, qi: (l, 0, 0))]
    args = [proj, proj, proj, diff_lam, diff_subln.reshape(DEPTH, 1, DIFF_VD)]
    if ctx:
        cos, sin = tables
        ctx_spec = pl.BlockSpec((None, None, PAST_LEN, LANES), lambda b, h, qi: (b, l, 0, h))
        q_tab = pl.BlockSpec((tq, LANES), lambda b, h, qi: (qi, 0))
        k_tab = pl.BlockSpec((lk, LANES), lambda b, h, qi: (0, 0))
        in_specs += [ctx_spec, ctx_spec, q_tab, q_tab, k_tab, k_tab]
        args += [cache_k.reshape(DEC_BATCH, DEPTH, PAST_LEN, GROUP_W), cache_v.reshape(DEC_BATCH, DEPTH, PAST_LEN, GROUP_W),
                 cos, sin, cos, sin]
    n_keys = lk + (PAST_LEN if ctx else 0)
    return pl.pallas_call(
        partial(_diff_body, lk=lk, ctx=ctx, lam_init=lam_init),
        grid=(nb, nh, lk // tq),
        in_specs=in_specs + [pl.BlockSpec(memory_space=pl.ANY)],
        out_specs=pl.BlockSpec((tq, LANES), lambda b, h, qi: ((row0 + b * lk) // tq + qi, h)),
        out_shape=jax.ShapeDtypeStruct((T_ALL, GROUP_W), F32),
        input_output_aliases={len(args): 0},
        scratch_shapes=[pltpu.VMEM((n_keys, LANES), BF16), pltpu.VMEM((n_keys, LANES), BF16)],
        compiler_params=_cparams("arbitrary", "arbitrary", "arbitrary"),
        name="diff_attention_ctx" if ctx else "diff_attention",
    )(*args, dst)


def _swa_body(*refs, lk, tq, banded):
    *ins, _aliased_dst, o_ref, kb, vb = refs
    if banded:
        q_ref, k_ref, v_ref, sk_ref, ck_ref, cv_ref, cq_ref, sq_ref, ckk_ref, skk_ref = ins
    else:
        q_ref, k_ref, v_ref, sk_ref = ins
    qi = pl.program_id(1)

    @pl.when(qi == 0)
    def _():
        k = k_ref[...]
        if banded:
            k = _rope(k, ckk_ref[...], skk_ref[...])
            zeros = jnp.zeros((WINDOW, LANES), BF16)
            kb[0:WINDOW, :] = zeros
            vb[0:WINDOW, :] = zeros
            kb[WINDOW + lk:2 * WINDOW + lk, :] = zeros
            vb[WINDOW + lk:2 * WINDOW + lk, :] = zeros
            kb[2 * WINDOW + lk:, :] = ck_ref[...].astype(BF16)
            vb[2 * WINDOW + lk:, :] = cv_ref[...].astype(BF16)
            kb[WINDOW:WINDOW + lk, :] = k.astype(BF16)
            vb[WINDOW:WINDOW + lk, :] = v_ref[...].astype(BF16)
        else:
            kb[...] = k.astype(BF16)
            vb[...] = v_ref[...].astype(BF16)

    lane = lax.broadcasted_iota(jnp.int32, (tq, LANES), 1)
    sinks = sk_ref[...]
    if banded:
        start = pl.multiple_of(qi * WINDOW, WINDOW)
        k_loc, v_loc = kb[pl.ds(start, 3 * WINDOW), :], vb[pl.ds(start, 3 * WINDOW), :]
        k_ctx, v_ctx = kb[2 * WINDOW + lk:, :], vb[2 * WINDOW + lk:, :]
        row = lax.broadcasted_iota(jnp.int32, (SWA_GROUP * tq, 3 * WINDOW), 0) % tq
        col = lax.broadcasted_iota(jnp.int32, (SWA_GROUP * tq, 3 * WINDOW), 1)
        kpos = (qi - 1) * WINDOW + col
        visible = (col >= row) & (col - row <= 2 * WINDOW) & (kpos >= 0) & (kpos < lk)
    else:
        k_loc, v_loc = kb[...], vb[...]
    outs = []
    for kh in range(SWA_KV_HEADS):
        in_half = (lane >= HALF) if kh else (lane < HALF)
        qs, sink_rows = [], []
        for g in range(SWA_GROUP):
            head = kh * SWA_GROUP + g
            t = q_ref[:, (head // 2) * LANES:(head // 2 + 1) * LANES] * SWA_SCALE
            if banded:
                t = _rope(t, cq_ref[...], sq_ref[...])
            if head % 2 != kh:
                t = pltpu.roll(t, HALF, 1)
            qs.append(jnp.where(in_half, t, 0.0).astype(BF16))
            sink_rows.append(jnp.broadcast_to(sinks[:, head:head + 1], (tq, 1)))
        qs = jnp.concatenate(qs, axis=0)
        sink = jnp.concatenate(sink_rows, axis=0)
        s_loc = lax.dot_general(qs, k_loc, NT_DIMS, preferred_element_type=F32)
        m = sink
        if banded:
            s_loc = jnp.where(visible, s_loc, NEG_INF)
            s_ctx = lax.dot_general(qs, k_ctx, NT_DIMS, preferred_element_type=F32)
            m = jnp.maximum(m, jnp.max(s_ctx, axis=-1, keepdims=True))
        m = jnp.maximum(m, jnp.max(s_loc, axis=-1, keepdims=True))
        e_loc = jnp.exp(s_loc - m)
        den = jnp.sum(e_loc, axis=-1, keepdims=True) + jnp.exp(sink - m)
        if banded:
            e_ctx = jnp.exp(s_ctx - m)
            den = den + jnp.sum(e_ctx, axis=-1, keepdims=True)
        o = jnp.dot((e_loc / den).astype(BF16), v_loc, preferred_element_type=F32)
        if banded:
            o = o + jnp.dot((e_ctx / den).astype(BF16), v_ctx, preferred_element_type=F32)
        outs.append(o)
    for t in range(SWA_HEADS // 2):
        kh, g0 = (2 * t) // SWA_GROUP, (2 * t) % SWA_GROUP
        a = outs[kh][g0 * tq:(g0 + 1) * tq]
        b = outs[kh][(g0 + 1) * tq:(g0 + 2) * tq]
        a = pltpu.roll(a, HALF, 1) if kh == 1 else a
        b = pltpu.roll(b, HALF, 1) if kh == 0 else b
        o_ref[:, t * LANES:(t + 1) * LANES] = jnp.where(lane < HALF, a, b)


def swa_attention(proj, dst, swa_sink, l, cache_k=None, cache_v=None, tables=None):
    banded = cache_k is not None
    q_col, k_col = 3, (4 * GROUP_W) // LANES
    if banded:
        nb, lk, tq, row0 = DEC_BATCH, DEC_SEQ, WINDOW, T_PROMPT
    else:
        nb, lk, tq, row0 = BATCH, SEQ, SEQ, 0
    in_specs = [pl.BlockSpec((tq, GROUP_W), lambda b, qi: ((row0 + b * lk) // tq + qi, q_col)),
                pl.BlockSpec((lk, LANES), lambda b, qi: (row0 // lk + b, k_col)),
                pl.BlockSpec((lk, LANES), lambda b, qi: (row0 // lk + b, k_col + 1)),
                pl.BlockSpec((None, 1, SWA_HEADS), lambda b, qi: (l, 0, 0))]
    args = [proj, proj, proj, swa_sink.reshape(DEPTH, 1, SWA_HEADS)]
    if banded:
        cos, sin = tables
        ctx_spec = pl.BlockSpec((None, None, PAST_LEN, LANES), lambda b, qi: (b, l, 0, 0))
        q_tab = pl.BlockSpec((tq, LANES), lambda b, qi: (qi, 0))
        k_tab = pl.BlockSpec((lk, LANES), lambda b, qi: (0, 0))
        in_specs += [ctx_spec, ctx_spec, q_tab, q_tab, k_tab, k_tab]
        args += [cache_k.reshape(DEC_BATCH, DEPTH, PAST_LEN, LANES), cache_v.reshape(DEC_BATCH, DEPTH, PAST_LEN, LANES),
                 cos, sin, cos, sin]
    n_keys = lk + 2 * WINDOW + PAST_LEN if banded else lk
    return pl.pallas_call(
        partial(_swa_body, lk=lk, tq=tq, banded=banded),
        grid=(nb, lk // tq),
        in_specs=in_specs + [pl.BlockSpec(memory_space=pl.ANY)],
        out_specs=pl.BlockSpec((tq, GROUP_W), lambda b, qi: ((row0 + b * lk) // tq + qi, 0)),
        out_shape=jax.ShapeDtypeStruct((T_ALL, GROUP_W), F32),
        input_output_aliases={len(args): 0},
        scratch_shapes=[pltpu.VMEM((n_keys, LANES), BF16), pltpu.VMEM((n_keys, LANES), BF16)],
        compiler_params=_cparams("arbitrary", "arbitrary"),
        name="swa_attention_banded" if banded else "swa_attention",
    )(*args, dst)


LRU_GROUPS = LRU_W // LANES


def lru_gate_params(wa, ba, wx, bx):
    def tile_blocks(w):
        w = w.reshape(LRU_GROUPS, 2, LRU_BD, LRU_BD)
        z = jnp.zeros((LRU_GROUPS, LRU_BD, LRU_BD), F32)
        return jnp.concatenate([jnp.concatenate([w[:, 0], z], axis=-1), jnp.concatenate([z, w[:, 1]], axis=-1)], axis=-2)

    w = jnp.concatenate([tile_blocks(wa[0]), tile_blocks(wx[0]), tile_blocks(wa[1]), tile_blocks(wx[1])], axis=-1)
    b = jnp.stack([ba[0], bx[0], ba[1], bx[1]], axis=0).reshape(4, LRU_GROUPS, LANES)
    return w, jnp.moveaxis(b, 0, 1).reshape(LRU_GROUPS, 1, 4 * LANES)


def _shift_rows(x, s, row, n):
    if s == 0:
        return x
    ok = (row >= -s) if s < 0 else (row < n - s)
    return jnp.where(ok, pltpu.roll(x, (-s) % n, 0), 0.0)


def _linear_scan(a, b, row, n, reverse):
    s = 1
    while s < n:
        ok = (row < n - s) if reverse else (row >= s)
        shift = (n - s) if reverse else s
        b = jnp.where(ok, a * pltpu.roll(b, shift, 0) + b, b)
        a = jnp.where(ok, a * pltpu.roll(a, shift, 0), a)
        s *= 2
    return b


def _lru_body(x_ref, lg_ref, cw_ref, w_ref, bias_ref, lam_ref, h0_ref, _aliased_dst, o_ref, st_ref, *, n):
    x = x_ref[...]
    row = lax.broadcasted_iota(jnp.int32, x.shape, 0)
    cw = cw_ref[...]
    xc = _shift_rows(x, -(LRU_CONV // 2), row, n) * cw[0:1]
    for k in range(1, LRU_CONV):
        xc = xc + _shift_rows(x, k - LRU_CONV // 2, row, n) * cw[k:k + 1]
    gates = jnp.dot(xc.astype(BF16), w_ref[...].astype(BF16), preferred_element_type=F32) + bias_ref[...]
    lam = lam_ref[...]
    log_sig = jnp.minimum(lam, 0.0) - jnp.log1p(jnp.exp(-jnp.abs(lam)))
    h0 = h0_ref[...]
    hs = []
    for d in range(2):
        r = jax.nn.sigmoid(gates[:, (2 * d) * LANES:(2 * d + 1) * LANES])
        i = jax.nn.sigmoid(gates[:, (2 * d + 1) * LANES:(2 * d + 2) * LANES])
        log_a = LRU_C * r * log_sig[d:d + 1]
        a = jnp.exp(log_a)
        th = jnp.tanh(log_a)
        b = jnp.sqrt(-2.0 * th / (1.0 - th)) * (i * xc)
        first = n - 1 if d else 0
        b = jnp.where(row == first, b + a * h0[d:d + 1], b)
        hs.append(_linear_scan(a, b, row, n, reverse=bool(d)))
    lg = lg_ref[...]
    gelu = 0.5 * lg * (1.0 + jnp.tanh(math.sqrt(2.0 / math.pi) * (lg + 0.044715 * (lg * lg * lg))))
    o_ref[...] = (hs[0] + hs[1]) * gelu
    st_ref[0:1, :] = hs[0][n - 1:n, :]
    st_ref[1:2, :] = hs[1][0:1, :]


def rglru(proj, dst, gate_w, gate_b, lru_conv, lru_lam, h0, l, h0_l, prompt):
    nb, n, row0 = (BATCH, SEQ, 0) if prompt else (DEC_BATCH, DEC_SEQ, T_PROMPT)
    x_col = (D_IN - 2 * LRU_W) // LANES
    return pl.pallas_call(
        partial(_lru_body, n=n),
        grid=(nb, LRU_GROUPS),
        in_specs=[pl.BlockSpec((n, LANES), lambda s, c: (row0 // n + s, x_col + c)),
                  pl.BlockSpec((n, LANES), lambda s, c: (row0 // n + s, x_col + LRU_GROUPS + c)),
                  pl.BlockSpec((None, LRU_CONV, LANES), lambda s, c: (l, 0, c)),
                  pl.BlockSpec((None, LANES, 4 * LANES), lambda s, c: (c, 0, 0)),
                  pl.BlockSpec((None, 1, 4 * LANES), lambda s, c: (c, 0, 0)),
                  pl.BlockSpec((None, 2, LANES), lambda s, c: (l, 0, c)),
                  pl.BlockSpec((None, None, 2, LANES), lambda s, c: (s, h0_l, 0, c)),
                  pl.BlockSpec(memory_space=pl.ANY)],
        out_specs=[pl.BlockSpec((n, LANES), lambda s, c: (row0 // n + s, c)),
                   pl.BlockSpec((None, 2, LANES), lambda s, c: (s, 0, c))],
        out_shape=[jax.ShapeDtypeStruct((T_ALL, LRU_W), F32), jax.ShapeDtypeStruct((nb, 2, LRU_W), F32)],
        input_output_aliases={7: 0},
        compiler_params=_cparams("arbitrary", "arbitrary"),
        name="rglru_prompt" if prompt else "rglru_sample",
    )(proj, proj, lru_conv, gate_w, gate_b, lru_lam, h0, dst)


def moe_layer(x1, mod, p, l):
    h2, idx, gates, rank, counts = router(x1, p['norm2'], mod, p['router_w'], p['router_bias'], l)
    idx, rank = idx[:, :TOP_K], rank[:, :TOP_K]
    nblk = (counts[0] + TM_E - 1) // TM_E
    blk_end = jnp.cumsum(nblk)
    row_start = (blk_end - nblk) * TM_E
    dest = row_start[idx] + rank
    n_valid = blk_end[-1:].astype(jnp.int32)
    blk = jnp.minimum(jnp.arange(N_BLK_MAX, dtype=jnp.int32), n_valid[0] - 1)
    blk_e = jnp.minimum(jnp.searchsorted(blk_end, blk, side='right'), N_EXPERTS - 1).astype(jnp.int32)
    tok = jnp.broadcast_to(jnp.arange(T_ALL, dtype=jnp.int32)[:, None], dest.shape)
    src = jnp.zeros((N_BLK_MAX * TM_E,), jnp.int32).at[dest.reshape(-1)].set(tok.reshape(-1))
    xbuf = h2[src]
    ybuf = routed_experts(xbuf, blk_e, n_valid, p['moe_w_gu'], p['moe_w_dn'], l)
    return shared_combine(h2, p['sh_w_gu'], p['sh_w_dn'], ybuf, dest, gates, x1, mod, l)


def mixer_heads(proj, p, lam_init, ctx):
    B, L, _ = proj.shape
    dq, dk, dv, sq, sk, sv, hy, lx, lg = jnp.split(proj, np.cumsum(IN_SIZES)[:-1].tolist(), axis=-1)
    dq = dq.reshape(B, L, DIFF_HEADS, 2, DIFF_QK) * DIFF_SCALE
    dk = dk.reshape(B, L, DIFF_HEADS, 2, DIFF_QK)
    dv = dv.reshape(B, L, DIFF_HEADS, DIFF_VD)
    sq = sq.reshape(B, L, SWA_KV_HEADS, SWA_GROUP, HEAD_DIM) * SWA_SCALE
    sk = sk.reshape(B, L, SWA_KV_HEADS, HEAD_DIM)
    sv = sv.reshape(B, L, SWA_KV_HEADS, HEAD_DIM)
    lq1, lk1, lq2, lk2 = p['diff_lam'].astype(F32)
    lam = jnp.exp(jnp.sum(lq1 * lk1)) - jnp.exp(jnp.sum(lq2 * lk2)) + lam_init
    sink = p['swa_sink'].reshape(SWA_KV_HEADS, SWA_GROUP)
    if ctx is None:
        d_out = diff_attend(dq, dk, dv, lam)
        s_out = swa_dense(sq, sk, sv, sink)
        h0_f = jnp.zeros((B, LRU_W), F32)
        h0_b = jnp.zeros((B, LRU_W), F32)
    else:
        ck_d, cv_d, ck_s, cv_s, st = ctx
        cos, sin = axial_rope(L)
        dq_r = apply_rope(dq.reshape(B, L, 2 * DIFF_HEADS, DIFF_QK), cos, sin).reshape(dq.shape)
        dk_r = apply_rope(dk.reshape(B, L, 2 * DIFF_HEADS, DIFF_QK), cos, sin).reshape(dk.shape)
        d_out = diff_attend(dq_r, jnp.concatenate([dk_r, ck_d], axis=1), jnp.concatenate([dv, cv_d], axis=1), lam)
        sq_r = apply_rope(sq.reshape(B, L, SWA_HEADS, HEAD_DIM), cos, sin).reshape(sq.shape)
        s_out = swa_banded(sq_r, apply_rope(sk, cos, sin), sv, ck_s, cv_s, sink)
        h0_f, h0_b = st[:, 0], st[:, 1]
    d_out = (rmsnorm(d_out, p['diff_subln']) * (1.0 - lam_init)).reshape(B, L, DIFF_HEADS * DIFF_VD)
    hy_out = hyena_mix(hy, p)
    lru_in = dwconv(lx, p['lru_conv'], LRU_CONV // 2)
    hl, hf_last, hb_first = rglru_bidir(lru_in, p['lru_wa'], p['lru_ba'], p['lru_wx'], p['lru_bx'], p['lru_lam'], h0_f, h0_b)
    lru_out = hl.astype(proj.dtype) * jax.nn.gelu(lg)
    mix = jnp.concatenate([d_out, s_out, hy_out, lru_out], axis=-1).reshape(B * L, D_MIX)
    if ctx is None:
        return mix, (dk, dv, sk, sv, jnp.stack([hf_last, hb_first], axis=1).astype(proj.dtype))
    return mix, None


def kernel(x_prompt, x_sample, cache_diff_k, cache_diff_v, cache_swa_k, cache_swa_v, state_lru, c, c_ctx,
           w_mod, b_mod, norm1, norm2, w_in, w_out, diff_lam, diff_subln, swa_sink, hy_conv, hy_w1, hy_b1,
           hy_w2, hy_b2, hy_w3, hy_freq, hy_log_rate, hy_skip, lru_conv, lru_wa, lru_ba, lru_wx, lru_bx, lru_lam,
           router_w, router_bias, moe_w_gu, moe_w_dn, sh_w_gu, sh_w_dn, final_norm):
    x = jnp.concatenate([x_prompt.reshape(T_PROMPT, D_MODEL), x_sample.reshape(T_SAMPLE, D_MODEL)], axis=0)
    cond = jnp.concatenate([c_ctx[None, :], c, jnp.zeros((8 - N_COND, D_MODEL), F32)], axis=0)
    dk_l, dv_l, sk_l, sv_l, st_l = [], [], [], [], []
    tables = rope_tables(DEC_SEQ)
    h0_prompt = jnp.zeros((BATCH, 1, 2, LRU_W), F32)
    col = np.cumsum((0,) + IN_SIZES).tolist()
    empty = lambda: jnp.zeros((T_ALL, GROUP_W), F32)
    for l in range(DEPTH):
        p = {'hy_conv': hy_conv[l], 'hy_w1': hy_w1[l], 'hy_b1': hy_b1[l], 'hy_w2': hy_w2[l], 'hy_b2': hy_b2[l],
             'hy_w3': hy_w3[l], 'hy_freq': hy_freq[l], 'hy_log_rate': hy_log_rate[l], 'hy_skip': hy_skip[l],
             'norm2': norm2, 'router_w': router_w, 'router_bias': router_bias, 'moe_w_gu': moe_w_gu,
             'moe_w_dn': moe_w_dn, 'sh_w_gu': sh_w_gu, 'sh_w_dn': sh_w_dn}
        lam_init = 0.8 - 0.6 * math.exp(-0.3 * l)
        mod = modulation(cond, w_mod, b_mod, l)[:N_COND].reshape(N_COND * 6, 1, D_MODEL)
        proj = in_proj(x, norm1, mod, w_in, l)
        proj_p = proj[:T_PROMPT].reshape(BATCH, SEQ, D_IN)
        dk_l.append(proj_p[..., col[1]:col[2]].reshape(BATCH, SEQ, DIFF_HEADS, 2, DIFF_QK))
        dv_l.append(proj_p[..., col[2]:col[3]].reshape(BATCH, SEQ, DIFF_HEADS, DIFF_VD))
        sk_l.append(proj_p[..., col[4]:col[5]].reshape(BATCH, SEQ, SWA_KV_HEADS, HEAD_DIM))
        sv_l.append(proj_p[..., col[5]:col[6]].reshape(BATCH, SEQ, SWA_KV_HEADS, HEAD_DIM))
        d_out = diff_attention(proj, empty(), diff_lam, diff_subln, l, lam_init)
        d_out = diff_attention(proj, d_out, diff_lam, diff_subln, l, lam_init, cache_diff_k, cache_diff_v, tables)
        s_out = swa_attention(proj, empty(), swa_sink, l)
        s_out = swa_attention(proj, s_out, swa_sink, l, cache_swa_k, cache_swa_v, tables)
        hy_out = jnp.concatenate(
            [hyena_mix(proj_p[..., col[6]:col[7]], p).reshape(T_PROMPT, HY_W),
             hyena_mix(proj[T_PROMPT:, col[6]:col[7]].reshape(DEC_BATCH, DEC_SEQ, -1), p).reshape(T_SAMPLE, HY_W)], axis=0)
        gate_w, gate_b = lru_gate_params(lru_wa[l], lru_ba[l], lru_wx[l], lru_bx[l])
        lru_out, st = rglru(proj, empty(), gate_w, gate_b, lru_conv, lru_lam, h0_prompt, l, 0, True)
        lru_out, _ = rglru(proj, lru_out, gate_w, gate_b, lru_conv, lru_lam, state_lru, l, l, False)
        st_l.append(st)
        x1 = out_proj([d_out, s_out, hy_out, lru_out], w_out, x, mod, l)
        x = moe_layer(x1, mod, p, l)
    y = final_rmsnorm(x, final_norm)
    y_prompt = y[:T_PROMPT].reshape(BATCH, SEQ, D_MODEL)
    y_sample = y[T_PROMPT:].reshape(DEC_BATCH, DEC_SEQ, D_MODEL)
    return (y_prompt, y_sample, jnp.stack(dk_l, axis=1), jnp.stack(dv_l, axis=1), jnp.stack(sk_l, axis=1),
            jnp.stack(sv_l, axis=1), jnp.stack(st_l, axis=1))
```

```python
import math
from functools import partial
import jax, jax.numpy as jnp
from jax import lax
import numpy as np
from jax.experimental import pallas as pl
from jax.experimental.pallas import tpu as pltpu

D_MODEL = 2048
BATCH = 16
SEQ = 256
DEPTH = 2
DEC_BATCH = 2
DEC_SEQ = 4096
PAST_LEN = 256

GRID_W = 64
EPS = 1e-6
F32 = jnp.float32
QBLK = 128
GROUP_W = D_MODEL // 4
D_MIX = 4 * GROUP_W
DIFF_QK = 64
DIFF_VD = 2 * DIFF_QK
DIFF_HEADS = GROUP_W // DIFF_VD
DIFF_SCALE = DIFF_QK ** -0.5
HEAD_DIM = 64
SWA_HEADS = GROUP_W // HEAD_DIM
SWA_KV_HEADS = SWA_HEADS // 4
SWA_GROUP = SWA_HEADS // SWA_KV_HEADS
SWA_SCALE = HEAD_DIM ** -0.5
WINDOW = 128
ROPE_PAIRS = HEAD_DIM // 4
ROPE_BASE = 10000.0
HY_W = GROUP_W
HY_ORDER = 2
HY_SHORT = 3
HY_BANDS = 16
HY_POS_DIM = 1 + 2 * HY_BANDS
HY_HID = 64
LRU_W = GROUP_W
LRU_BLOCKS = 8
LRU_BD = LRU_W // LRU_BLOCKS
LRU_CONV = 4
LRU_C = 8.0
N_EXPERTS = 64
TOP_K = 6
EXPERT_FF = D_MODEL // 4
SHARED_FF = EXPERT_FF
ROUTE_SCALE = 2.5
MOE_BLK = 128
NEG_INF = -1e30
IN_SIZES = (2 * DIFF_HEADS * DIFF_QK, 2 * DIFF_HEADS * DIFF_QK, DIFF_HEADS * DIFF_VD, SWA_HEADS * HEAD_DIM, SWA_KV_HEADS * HEAD_DIM, SWA_KV_HEADS * HEAD_DIM, (HY_ORDER + 1) * HY_W, LRU_W, LRU_W)
D_IN = sum(IN_SIZES)


def rmsnorm(x, g):
    xf = x.astype(F32)
    y = xf * lax.rsqrt(jnp.mean(xf * xf, axis=-1, keepdims=True) + EPS)
    return (y * g.astype(F32)).astype(x.dtype)


def dwconv(x, w, left):
    K, L = w.shape[0], x.shape[1]
    xp = jnp.pad(x, ((0, 0), (left, K - 1 - left), (0, 0)))
    acc = xp[:, 0:L] * w[0]
    for k in range(1, K):
        acc = acc + xp[:, k:k + L] * w[k]
    return acc


def axial_rope(L):
    rows = L // GRID_W
    row = jnp.repeat(jnp.arange(rows), GRID_W).astype(F32)
    col = jnp.tile(jnp.arange(GRID_W), rows).astype(F32)
    inv = ROPE_BASE ** (-jnp.arange(ROPE_PAIRS, dtype=F32) / ROPE_PAIRS)
    ang = jnp.stack([row[:, None] * inv, col[:, None] * inv], axis=1)
    return jnp.cos(ang), jnp.sin(ang)


def apply_rope(x, cos, sin):
    xs = x.reshape(*x.shape[:-1], 2, 2, ROPE_PAIRS)
    c = cos[:, None, :, None, :].astype(x.dtype)
    s = sin[:, None, :, None, :].astype(x.dtype)
    x1, x2 = xs[..., 0:1, :], xs[..., 1:2, :]
    return jnp.concatenate([x1 * c - x2 * s, x2 * c + x1 * s], axis=-2).reshape(x.shape)


def diff_attend(q, k, v, lam):
    B, Lq = q.shape[:2]
    nb = Lq // QBLK
    qb = jnp.moveaxis(q.reshape(B, nb, QBLK, *q.shape[2:]), 1, 0)

    def one(qblk):
        s = jnp.einsum('bqhmd,bkhmd->bhmqk', qblk, k, preferred_element_type=F32)
        p = jax.nn.softmax(s, axis=-1)
        w = p[:, :, 0] - lam * p[:, :, 1]
        return jnp.einsum('bhqk,bkhd->bqhd', w.astype(v.dtype), v)

    o = lax.map(one, qb)
    return jnp.moveaxis(o, 0, 1).reshape(B, Lq, q.shape[2], v.shape[-1])


def sink_softmax(s, sink):
    sb = sink.astype(F32)[None, :, :, None, None]
    m = jnp.maximum(jnp.max(s, axis=-1, keepdims=True), sb)
    e = jnp.exp(s - m)
    return e / (jnp.sum(e, axis=-1, keepdims=True) + jnp.exp(sb - m))


def swa_dense(q, k, v, sink):
    B, L = q.shape[:2]
    nb = L // QBLK
    qb = jnp.moveaxis(q.reshape(B, nb, QBLK, *q.shape[2:]), 1, 0)

    def one(qblk):
        p = sink_softmax(jnp.einsum('bqkgd,bnkd->bkgqn', qblk, k, preferred_element_type=F32), sink)
        return jnp.einsum('bkgqn,bnkd->bqkgd', p.astype(v.dtype), v)

    o = lax.map(one, qb)
    return jnp.moveaxis(o, 0, 1).reshape(B, L, SWA_HEADS * HEAD_DIM)


def swa_banded(q, k, v, k_ctx, v_ctx, sink):
    B, L = q.shape[:2]
    nb = L // WINDOW
    pad = lambda t: jnp.pad(t, ((0, 0), (WINDOW, WINDOW), (0, 0), (0, 0))).reshape(B, nb + 2, WINDOW, *t.shape[2:])
    kb, vb = pad(k), pad(v)
    kwin = jnp.concatenate([kb[:, :nb], kb[:, 1:nb + 1], kb[:, 2:]], axis=2)
    vwin = jnp.concatenate([vb[:, :nb], vb[:, 1:nb + 1], vb[:, 2:]], axis=2)
    qi = jnp.arange(WINDOW)[:, None]
    kj = jnp.arange(3 * WINDOW)[None, :]
    rel = kj - qi
    band = (rel >= 0) & (rel <= 2 * WINDOW)
    qb = jnp.moveaxis(q.reshape(B, nb, WINDOW, *q.shape[2:]), 1, 0)

    def one(args):
        b, qblk, kw, vw = args
        kpos = (b - 1) * WINDOW + kj
        mask = band & (kpos >= 0) & (kpos < L)
        s_loc = jnp.where(mask, jnp.einsum('bqkgd,bnkd->bkgqn', qblk, kw, preferred_element_type=F32), NEG_INF)
        s_ctx = jnp.einsum('bqkgd,bnkd->bkgqn', qblk, k_ctx, preferred_element_type=F32)
        p = sink_softmax(jnp.concatenate([s_loc, s_ctx], axis=-1), sink).astype(v.dtype)
        return (jnp.einsum('bkgqn,bnkd->bqkgd', p[..., :3 * WINDOW], vw)
                + jnp.einsum('bkgqn,bnkd->bqkgd', p[..., 3 * WINDOW:], v_ctx))

    o = lax.map(one, (jnp.arange(nb), qb, jnp.moveaxis(kwin, 1, 0), jnp.moveaxis(vwin, 1, 0)))
    return jnp.moveaxis(o, 0, 1).reshape(B, L, SWA_HEADS * HEAD_DIM)


def hyena_filters(L, w1, b1, w2, b2, w3, freq, log_rate):
    tn = jnp.arange(L, dtype=F32) / L
    ang = 2.0 * math.pi * tn[:, None] * jnp.arange(1, HY_BANDS + 1, dtype=F32)
    z = jnp.concatenate([tn[:, None], jnp.sin(ang), jnp.cos(ang)], axis=-1)
    f = jnp.sin(freq[0].astype(F32) * (z @ w1.astype(F32) + b1.astype(F32)))
    f = jnp.sin(freq[1].astype(F32) * (f @ w2.astype(F32) + b2.astype(F32)))
    f = (f @ w3.astype(F32)).reshape(L, HY_ORDER, 2, HY_W)
    f = f * jnp.exp(-tn[:, None, None, None] * jnp.exp(log_rate.astype(F32)))
    return f * lax.rsqrt(jnp.sum(f * f, axis=(0, 2), keepdims=True) + EPS)


def bidir_fftconv(u, kf, kb):
    L, C = u.shape[1], u.shape[2]
    kfull = jnp.concatenate([kf.at[0].add(kb[0]), jnp.zeros((1, C), F32), kb[1:][::-1]], axis=0)
    Kf = jnp.fft.rfft(kfull, axis=0)
    Uf = jnp.fft.rfft(u, n=2 * L, axis=1)
    return jnp.fft.irfft(Uf * Kf[None], n=2 * L, axis=1)[:, :L]


def hyena_mix(u, p):
    L = u.shape[1]
    u = dwconv(u, p['hy_conv'], HY_SHORT // 2)
    parts = jnp.split(u, HY_ORDER + 1, axis=-1)
    filt = hyena_filters(L, p['hy_w1'], p['hy_b1'], p['hy_w2'], p['hy_b2'], p['hy_w3'], p['hy_freq'], p['hy_log_rate'])
    skip = p['hy_skip'].astype(F32)
    z = parts[0].astype(F32)
    for o in range(HY_ORDER):
        z = parts[o + 1].astype(F32) * (bidir_fftconv(z, filt[:, o, 0], filt[:, o, 1]) + skip[o] * z)
    return z.astype(u.dtype)


def _lin_combine(e1, e2):
    a1, b1 = e1
    a2, b2 = e2
    return a1 * a2, a2 * b1 + b2


def rglru_bidir(x, wa, ba, wx, bx, lam, h0_f, h0_b):
    B, L, W = x.shape
    xb = x.reshape(B, L, LRU_BLOCKS, LRU_BD)
    hs = []
    for d, (h0, rev) in enumerate(((h0_f, False), (h0_b, True))):
        r = jax.nn.sigmoid(jnp.einsum('blnd,nde->blne', xb, wa[d]).reshape(B, L, W) + ba[d])
        i = jax.nn.sigmoid(jnp.einsum('blnd,nde->blne', xb, wx[d]).reshape(B, L, W) + bx[d])
        log_a = LRU_C * r.astype(F32) * jax.nn.log_sigmoid(lam[d].astype(F32))
        a = jnp.exp(log_a)
        bterm = jnp.sqrt(-jnp.expm1(2.0 * log_a)) * (i * x).astype(F32)
        first = L - 1 if rev else 0
        bterm = bterm.at[:, first].add(a[:, first] * h0.astype(F32))
        _, h = lax.associative_scan(_lin_combine, (a, bterm), axis=1, reverse=rev)
        hs.append(h)
    return hs[0] + hs[1], hs[0][:, -1], hs[1][:, 0]


BF16 = jnp.bfloat16
T_PROMPT = BATCH * SEQ
T_SAMPLE = DEC_BATCH * DEC_SEQ
T_ALL = T_PROMPT + T_SAMPLE
N_COND = 1 + DEC_BATCH
TM = 1024
TM_SH = 256
TN_IN = 256
TN_OUT = 512
TN_MOD = 1024
TM_E = 256
N_ASSIGN = T_ALL * TOP_K
N_BLK_MAX = N_ASSIGN // TM_E + N_EXPERTS
LANES = 128
VMEM_LIMIT = 56 * 1024 * 1024


def _cparams(*sem):
    return pltpu.CompilerParams(dimension_semantics=sem, vmem_limit_bytes=VMEM_LIMIT)


def _cond_row(i, tm):
    return jnp.where(i < T_PROMPT // tm, 0, 1 + (i - T_PROMPT // tm) // (DEC_SEQ // tm))


def _mod_spec(which, tm, tn=D_MODEL):
    if tn == D_MODEL:
        return pl.BlockSpec((1, 1, D_MODEL), lambda i, *_: (_cond_row(i, tm) * 6 + which, 0, 0))
    return pl.BlockSpec((1, 1, tn), lambda i, j: (_cond_row(i, tm) * 6 + which, 0, j))


def _mod_body(c_ref, w_ref, b_ref, o_ref):
    c = c_ref[...]
    a = (c * jax.nn.sigmoid(c)).astype(BF16)
    o_ref[...] = jnp.dot(a, w_ref[...].astype(BF16), preferred_element_type=F32) + b_ref[0]


def modulation(cond, w_mod, b_mod, l):
    n = 6 * D_MODEL
    return pl.pallas_call(
        _mod_body,
        grid=(n // TN_MOD,),
        in_specs=[pl.BlockSpec((8, D_MODEL), lambda j: (0, 0)),
                  pl.BlockSpec((None, D_MODEL, TN_MOD), lambda j: (l, 0, j)),
                  pl.BlockSpec((None, 1, TN_MOD), lambda j: (l, 0, j))],
        out_specs=pl.BlockSpec((8, TN_MOD), lambda j: (0, j)),
        out_shape=jax.ShapeDtypeStruct((8, n), F32),
        compiler_params=_cparams("arbitrary"),
        name="modulation",
    )(cond, w_mod, b_mod.reshape(DEPTH, 1, n))


def _norm_mod(x, g, sc, sh):
    y = x * lax.rsqrt(jnp.mean(x * x, axis=-1, keepdims=True) + EPS) * g
    return y * (1.0 + sc) + sh


def _in_proj_body(x_ref, g_ref, sc_ref, sh_ref, w_ref, o_ref, hb_ref):
    @pl.when(pl.program_id(1) == 0)
    def _():
        hb_ref[...] = _norm_mod(x_ref[...], g_ref[0], sc_ref[0], sh_ref[0]).astype(BF16)

    o_ref[...] = jnp.dot(hb_ref[...], w_ref[...].astype(BF16), preferred_element_type=F32)


def in_proj(x, norm1, mod, w_in, l):
    return pl.pallas_call(
        _in_proj_body,
        grid=(T_ALL // TM, D_IN // TN_IN),
        in_specs=[pl.BlockSpec((TM, D_MODEL), lambda i, j: (i, 0)),
                  pl.BlockSpec((None, 1, D_MODEL), lambda i, j: (l, 0, 0)),
                  _mod_spec(1, TM), _mod_spec(0, TM),
                  pl.BlockSpec((None, D_MODEL, TN_IN), lambda i, j: (l, 0, j))],
        out_specs=pl.BlockSpec((TM, TN_IN), lambda i, j: (i, j)),
        out_shape=jax.ShapeDtypeStruct((T_ALL, D_IN), F32),
        scratch_shapes=[pltpu.VMEM((TM, D_MODEL), BF16)],
        compiler_params=_cparams("arbitrary", "arbitrary"),
        name="in_proj",
    )(x, norm1.reshape(DEPTH, 1, D_MODEL), mod, mod, w_in)


N_MIX = D_MIX // GROUP_W


def _out_proj_body(*refs):
    m_refs, (w_ref, x_ref, g1_ref, o_ref, mb_ref) = refs[:N_MIX], refs[N_MIX:]

    @pl.when(pl.program_id(1) == 0)
    def _():
        for g in range(N_MIX):
            mb_ref[g] = m_refs[g][...].astype(BF16)

    acc = jnp.dot(mb_ref[0], w_ref[0:GROUP_W, :].astype(BF16), preferred_element_type=F32)
    for g in range(1, N_MIX):
        acc = acc + jnp.dot(mb_ref[g], w_ref[g * GROUP_W:(g + 1) * GROUP_W, :].astype(BF16), preferred_element_type=F32)
    o_ref[...] = x_ref[...] + g1_ref[0] * acc


def out_proj(mixes, w_out, x, mod, l):
    return pl.pallas_call(
        _out_proj_body,
        grid=(T_ALL // TM, D_MODEL // TN_OUT),
        in_specs=[pl.BlockSpec((TM, GROUP_W), lambda i, j: (i, 0))] * N_MIX + [
                  pl.BlockSpec((None, D_MIX, TN_OUT), lambda i, j: (l, 0, j)),
                  pl.BlockSpec((TM, TN_OUT), lambda i, j: (i, j)),
                  _mod_spec(2, TM, TN_OUT)],
        out_specs=pl.BlockSpec((TM, TN_OUT), lambda i, j: (i, j)),
        out_shape=jax.ShapeDtypeStruct((T_ALL, D_MODEL), F32),
        scratch_shapes=[pltpu.VMEM((N_MIX, TM, GROUP_W), BF16)],
        compiler_params=_cparams("arbitrary", "arbitrary"),
        name="out_proj",
    )(*mixes, w_out, x, mod)


def _router_body(x_ref, g_ref, sc_ref, sh_ref, wr_ref, rb_ref, h_ref, idx_ref, gate_ref, rank_ref, cnt_ref, carry_ref):
    i = pl.program_id(0)

    @pl.when(i == 0)
    def _():
        carry_ref[...] = jnp.zeros_like(carry_ref)

    h = _norm_mod(x_ref[...], g_ref[0], sc_ref[0], sh_ref[0])
    h_ref[...] = h
    logits = jnp.dot(h, wr_ref[...], preferred_element_type=F32, precision=lax.Precision.HIGHEST)
    s = jax.nn.sigmoid(logits)
    cur = s + rb_ref[0]
    e_iota = lax.broadcasted_iota(jnp.int32, s.shape, 1).astype(F32)
    lane = lax.broadcasted_iota(jnp.int32, (TM, LANES), 1)
    r_iota = lax.broadcasted_iota(jnp.int32, (TM, TM), 0)
    c_iota = lax.broadcasted_iota(jnp.int32, (TM, TM), 1)
    tri = (c_iota <= r_iota).astype(BF16)
    idx_out = jnp.zeros((TM, LANES), F32)
    gate_out = jnp.zeros((TM, LANES), F32)
    rank_out = jnp.zeros((TM, LANES), F32)
    gsum = jnp.zeros((TM, 1), F32)
    carry = carry_ref[...]
    for k in range(TOP_K):
        m = jnp.max(cur, axis=-1, keepdims=True)
        ek = jnp.min(jnp.where(cur == m, e_iota, float(N_EXPERTS)), axis=-1, keepdims=True)
        hit = e_iota == ek
        gk = jnp.sum(jnp.where(hit, s, 0.0), axis=-1, keepdims=True)
        cur = jnp.where(hit, -jnp.inf, cur)
        onehot = hit.astype(BF16)
        cum = jnp.dot(tri, onehot, preferred_element_type=F32)
        rk = jnp.sum(jnp.where(hit, carry + cum, 0.0), axis=-1, keepdims=True) - 1.0
        carry = carry + cum[TM - 1:TM, :]
        gsum = gsum + gk
        idx_out = jnp.where(lane == k, ek, idx_out)
        gate_out = jnp.where(lane == k, gk, gate_out)
        rank_out = jnp.where(lane == k, rk, rank_out)
    carry_ref[...] = carry
    idx_ref[...] = idx_out.astype(jnp.int32)
    gate_ref[...] = ROUTE_SCALE * gate_out / gsum
    rank_ref[...] = rank_out.astype(jnp.int32)
    cnt_ref[...] = carry.astype(jnp.int32)


def router(x1, norm2, mod, router_w, router_bias, l):
    tok_out = lambda dt: jax.ShapeDtypeStruct((T_ALL, LANES), dt)
    tok_spec = pl.BlockSpec((TM, LANES), lambda i: (i, 0))
    return pl.pallas_call(
        _router_body,
        grid=(T_ALL // TM,),
        in_specs=[pl.BlockSpec((TM, D_MODEL), lambda i: (i, 0)),
                  pl.BlockSpec((None, 1, D_MODEL), lambda i: (l, 0, 0)),
                  _mod_spec(4, TM), _mod_spec(3, TM),
                  pl.BlockSpec((None, D_MODEL, N_EXPERTS), lambda i: (l, 0, 0)),
                  pl.BlockSpec((None, 1, N_EXPERTS), lambda i: (l, 0, 0))],
        out_specs=[pl.BlockSpec((TM, D_MODEL), lambda i: (i, 0)), tok_spec, tok_spec, tok_spec,
                   pl.BlockSpec((1, N_EXPERTS), lambda i: (0, 0))],
        out_shape=[jax.ShapeDtypeStruct((T_ALL, D_MODEL), F32), tok_out(jnp.int32), tok_out(F32), tok_out(jnp.int32),
                   jax.ShapeDtypeStruct((1, N_EXPERTS), jnp.int32)],
        scratch_shapes=[pltpu.VMEM((1, N_EXPERTS), F32)],
        compiler_params=_cparams("arbitrary"),
        name="router",
    )(x1, norm2.reshape(DEPTH, 1, D_MODEL), mod, mod, router_w, router_bias.reshape(DEPTH, 1, N_EXPERTS))


def _experts_body(be_ref, nv_ref, src_ref, nxt_ref, h_hbm, wgu_ref, wdn_ref, y_ref, wgu_b, wdn_b, xg, sems):
    i = pl.program_id(0)
    n_valid = nv_ref[0]

    def gather(idx_ref, slot):
        def issue(t, carry):
            pltpu.make_async_copy(h_hbm.at[pl.ds(idx_ref[0, t], 1)], xg.at[slot, pl.ds(t, 1)], sems.at[slot]).start()
            return carry

        lax.fori_loop(0, TM_E, issue, 0, unroll=8)

    @pl.when(i == 0)
    def _():
        gather(src_ref, 0)

    @pl.when(i + 1 < n_valid)
    def _():
        gather(nxt_ref, (i + 1) % 2)

    @pl.when(i < n_valid)
    def _():
        prev = be_ref[jnp.maximum(i - 1, 0)]

        @pl.when(jnp.logical_or(i == 0, be_ref[i] != prev))
        def _():
            wgu_b[...] = wgu_ref[...].astype(BF16)
            wdn_b[...] = wdn_ref[...].astype(BF16)

        slot = i % 2
        pltpu.make_async_copy(h_hbm.at[pl.ds(0, TM_E)], xg.at[slot], sems.at[slot]).wait()
        hmid = jnp.dot(xg[slot].astype(BF16), wgu_b[...], preferred_element_type=F32)
        a, b = hmid[:, :EXPERT_FF], hmid[:, EXPERT_FF:]
        act = (a * jax.nn.sigmoid(a) * b).astype(BF16)
        y_ref[...] = jnp.dot(act, wdn_b[...], preferred_element_type=F32)

    @pl.when(i >= n_valid)
    def _():
        y_ref[...] = jnp.zeros_like(y_ref)


def routed_experts(h2, src, blk_e, n_valid, moe_w_gu, moe_w_dn, l):
    idx_spec = lambda off: pl.BlockSpec((None, 1, TM_E), lambda i, be, nv: (jnp.minimum(i + off, nv[0] - 1), 0, 0),
                                        memory_space=pltpu.SMEM)
    return pl.pallas_call(
        _experts_body,
        grid_spec=pltpu.PrefetchScalarGridSpec(
            num_scalar_prefetch=2,
            grid=(N_BLK_MAX,),
            in_specs=[idx_spec(0), idx_spec(1),
                      pl.BlockSpec(memory_space=pl.ANY),
                      pl.BlockSpec((None, None, D_MODEL, 2 * EXPERT_FF), lambda i, be, nv: (l, be[i], 0, 0)),
                      pl.BlockSpec((None, None, EXPERT_FF, D_MODEL), lambda i, be, nv: (l, be[i], 0, 0))],
            out_specs=pl.BlockSpec((TM_E, D_MODEL), lambda i, be, nv: (i, 0)),
            scratch_shapes=[pltpu.VMEM((D_MODEL, 2 * EXPERT_FF), BF16), pltpu.VMEM((EXPERT_FF, D_MODEL), BF16),
                            pltpu.VMEM((2, TM_E, D_MODEL), F32), pltpu.SemaphoreType.DMA((2,))]),
        out_shape=jax.ShapeDtypeStruct((N_BLK_MAX * TM_E, D_MODEL), F32),
        compiler_params=_cparams("arbitrary"),
        name="routed_experts",
    )(blk_e, n_valid, src.reshape(N_BLK_MAX, 1, TM_E), src.reshape(N_BLK_MAX, 1, TM_E), h2, moe_w_gu, moe_w_dn)


def _gathered_rows(ybuf_hbm, rows, sem, n):
    return pltpu.make_async_copy(ybuf_hbm.at[pl.ds(0, n)], rows.at[pl.ds(0, n)], sem)


def _shared_body(dest_ref, gate_ref, h_ref, wgu_ref, wdn_ref, x_ref, g2_ref, ybuf_hbm, o_ref, wgu_b, wdn_b, rows, sem):
    @pl.when(pl.program_id(0) == 0)
    def _():
        wgu_b[...] = wgu_ref[...].astype(BF16)
        wdn_b[...] = wdn_ref[...].astype(BF16)

    def issue(t, carry):
        for k in range(TOP_K):
            src = dest_ref[0, t * TOP_K + k]
            pltpu.make_async_copy(ybuf_hbm.at[pl.ds(src, 1)], rows.at[pl.ds(k * TM_SH + t, 1)], sem).start()
        return carry

    lax.fori_loop(0, TM_SH, issue, 0)
    hmid = jnp.dot(h_ref[...].astype(BF16), wgu_b[...], preferred_element_type=F32)
    a, b = hmid[:, :SHARED_FF], hmid[:, SHARED_FF:]
    act = (a * jax.nn.sigmoid(a) * b).astype(BF16)
    y = jnp.dot(act, wdn_b[...], preferred_element_type=F32)
    _gathered_rows(ybuf_hbm, rows, sem, TOP_K * TM_SH).wait()
    gates = gate_ref[...]
    for k in range(TOP_K):
        y = y + gates[:, k:k + 1] * rows[k * TM_SH:(k + 1) * TM_SH, :]
    o_ref[...] = x_ref[...] + g2_ref[0] * y


def shared_combine(h2, sh_w_gu, sh_w_dn, ybuf, dest, gates, x1, mod, l):
    tile = pl.BlockSpec((TM_SH, D_MODEL), lambda i: (i, 0))
    once = pl.Buffered(1)
    n_tiles = T_ALL // TM_SH
    return pl.pallas_call(
        _shared_body,
        grid=(n_tiles,),
        in_specs=[pl.BlockSpec((None, 1, TM_SH * TOP_K), lambda i: (i, 0, 0), memory_space=pltpu.SMEM),
                  pl.BlockSpec((TM_SH, LANES), lambda i: (i, 0)),
                  tile,
                  pl.BlockSpec((None, D_MODEL, 2 * SHARED_FF), lambda i: (l, 0, 0), pipeline_mode=once),
                  pl.BlockSpec((None, SHARED_FF, D_MODEL), lambda i: (l, 0, 0), pipeline_mode=once),
                  tile, _mod_spec(5, TM_SH),
                  pl.BlockSpec(memory_space=pl.ANY)],
        out_specs=tile,
        out_shape=jax.ShapeDtypeStruct((T_ALL, D_MODEL), F32),
        scratch_shapes=[pltpu.VMEM((D_MODEL, 2 * SHARED_FF), BF16), pltpu.VMEM((SHARED_FF, D_MODEL), BF16),
                        pltpu.VMEM((TOP_K * TM_SH, D_MODEL), F32), pltpu.SemaphoreType.DMA(())],
        compiler_params=_cparams("arbitrary"),
        name="shared_combine",
    )(dest.reshape(n_tiles, 1, TM_SH * TOP_K), gates, h2, sh_w_gu, sh_w_dn, x1, mod, ybuf)


def _final_norm_body(x_ref, g_ref, o_ref):
    x = x_ref[...]
    o_ref[...] = x * lax.rsqrt(jnp.mean(x * x, axis=-1, keepdims=True) + EPS) * g_ref[...]


def final_rmsnorm(x, g):
    return pl.pallas_call(
        _final_norm_body,
        grid=(T_ALL // TM,),
        in_specs=[pl.BlockSpec((TM, D_MODEL), lambda i: (i, 0)), pl.BlockSpec((1, D_MODEL), lambda i: (0, 0))],
        out_specs=pl.BlockSpec((TM, D_MODEL), lambda i: (i, 0)),
        out_shape=jax.ShapeDtypeStruct(x.shape, x.dtype),
        compiler_params=_cparams("arbitrary"),
        name="final_norm",
    )(x, g.reshape(1, D_MODEL))


TQ_DIFF = 256
HALF = LANES // 2
NT_DIMS = (((1,), (1,)), ((), ()))


def rope_tables(L):
    pos = jnp.arange(L)
    d = jnp.arange(LANES) % HEAD_DIM
    p = jnp.where(d // (2 * ROPE_PAIRS) == 0, (pos // GRID_W)[:, None], (pos % GRID_W)[:, None]).astype(F32)
    inv = ROPE_BASE ** (-(d % ROPE_PAIRS).astype(F32) / ROPE_PAIRS)
    ang = p * inv
    return jnp.cos(ang), jnp.where((d // ROPE_PAIRS) % 2 == 0, -jnp.sin(ang), jnp.sin(ang))


def _rope(x, cos, sin_signed):
    lane = lax.broadcasted_iota(jnp.int32, x.shape, 1)
    is_x1 = (lane // ROPE_PAIRS) % 2 == 0
    partner = jnp.where(is_x1, pltpu.roll(x, LANES - ROPE_PAIRS, 1), pltpu.roll(x, ROPE_PAIRS, 1))
    return x * cos + partner * sin_signed


def _softmax(s):
    e = jnp.exp(s - jnp.max(s, axis=-1, keepdims=True))
    return e / jnp.sum(e, axis=-1, keepdims=True)


def _diff_body(*refs, lk, ctx, lam_init):
    *ins, _aliased_dst, o_ref, kb, vb = refs
    if ctx:
        q_ref, k_ref, v_ref, dl_ref, g_ref, ck_ref, cv_ref, cq_ref, sq_ref, ckk_ref, skk_ref = ins
    else:
        q_ref, k_ref, v_ref, dl_ref, g_ref = ins

    @pl.when(pl.program_id(2) == 0)
    def _():
        k = k_ref[...]
        if ctx:
            k = _rope(k, ckk_ref[...], skk_ref[...])
            kb[lk:, :] = ck_ref[...].astype(BF16)
            vb[lk:, :] = cv_ref[...].astype(BF16)
        kb[0:lk, :] = k.astype(BF16)
        vb[0:lk, :] = v_ref[...].astype(BF16)

    q = q_ref[...] * DIFF_SCALE
    if ctx:
        q = _rope(q, cq_ref[...], sq_ref[...])
    lane = lax.broadcasted_iota(jnp.int32, q.shape, 1)
    q1 = jnp.where(lane < HALF, q, 0.0).astype(BF16)
    q2 = jnp.where(lane >= HALF, q, 0.0).astype(BF16)
    keys = kb[...]
    p1 = _softmax(lax.dot_general(q1, keys, NT_DIMS, preferred_element_type=F32))
    p2 = _softmax(lax.dot_general(q2, keys, NT_DIMS, preferred_element_type=F32))
    dl = dl_ref[...]
    lam = (jnp.exp(jnp.sum(dl[0:1] * dl[1:2], axis=-1, keepdims=True))
           - jnp.exp(jnp.sum(dl[2:3] * dl[3:4], axis=-1, keepdims=True)) + lam_init)
    o = jnp.dot((p1 - lam * p2).astype(BF16), vb[...], preferred_element_type=F32)
    o = o * lax.rsqrt(jnp.mean(o * o, axis=-1, keepdims=True) + EPS) * g_ref[...]
    o_ref[...] = o * (1.0 - lam_init)


def diff_attention(proj, dst, diff_lam, diff_subln, l, lam_init, cache_k=None, cache_v=None, tables=None):
    ctx = cache_k is not None
    nh = DIFF_HEADS
    if ctx:
        nb, lk, tq, row0 = DEC_BATCH, DEC_SEQ, TQ_DIFF, T_PROMPT
    else:
        nb, lk, tq, row0 = BATCH, SEQ, SEQ, 0
    in_specs = [pl.BlockSpec((tq, LANES), lambda b, h, qi: ((row0 + b * lk) // tq + qi, h)),
                pl.BlockSpec((lk, LANES), lambda b, h, qi: (row0 // lk + b, nh + h)),
                pl.BlockSpec((lk, LANES), lambda b, h, qi: (row0 // lk + b, 2 * nh + h)),
                pl.BlockSpec((None, 4, DIFF_QK), lambda b, h, qi: (l, 0, 0)),
                pl.BlockSpec((None, 1, DIFF_VD), lambda b, h, qi: (l, 0, 0))]
    args = [proj, proj, proj, diff_lam, diff_subln.reshape(DEPTH, 1, DIFF_VD)]
    if ctx:
        cos, sin = tables
        ctx_spec = pl.BlockSpec((None, None, PAST_LEN, LANES), lambda b, h, qi: (b, l, 0, h))
        q_tab = pl.BlockSpec((tq, LANES), lambda b, h, qi: (qi, 0))
        k_tab = pl.BlockSpec((lk, LANES), lambda b, h, qi: (0, 0))
        in_specs += [ctx_spec, ctx_spec, q_tab, q_tab, k_tab, k_tab]
        args += [cache_k.reshape(DEC_BATCH, DEPTH, PAST_LEN, GROUP_W), cache_v.reshape(DEC_BATCH, DEPTH, PAST_LEN, GROUP_W),
                 cos, sin, cos, sin]
    n_keys = lk + (PAST_LEN if ctx else 0)
    return pl.pallas_call(
        partial(_diff_body, lk=lk, ctx=ctx, lam_init=lam_init),
        grid=(nb, nh, lk // tq),
        in_specs=in_specs + [pl.BlockSpec(memory_space=pl.ANY)],
        out_specs=pl.BlockSpec((tq, LANES), lambda b, h, qi: ((row0 + b * lk) // tq + qi, h)),
        out_shape=jax.ShapeDtypeStruct((T_ALL, GROUP_W), F32),
        input_output_aliases={len(args): 0},
        scratch_shapes=[pltpu.VMEM((n_keys, LANES), BF16), pltpu.VMEM((n_keys, LANES), BF16)],
        compiler_params=_cparams("arbitrary", "arbitrary", "arbitrary"),
        name="diff_attention_ctx" if ctx else "diff_attention",
    )(*args, dst)


def _swa_body(*refs, lk, tq, banded):
    *ins, _aliased_dst, o_ref, kb, vb = refs
    if banded:
        q_ref, k_ref, v_ref, sk_ref, ck_ref, cv_ref, cq_ref, sq_ref, ckk_ref, skk_ref = ins
    else:
        q_ref, k_ref, v_ref, sk_ref = ins
    qi = pl.program_id(1)

    @pl.when(qi == 0)
    def _():
        k = k_ref[...]
        if banded:
            k = _rope(k, ckk_ref[...], skk_ref[...])
            zeros = jnp.zeros((WINDOW, LANES), BF16)
            kb[0:WINDOW, :] = zeros
            vb[0:WINDOW, :] = zeros
            kb[WINDOW + lk:2 * WINDOW + lk, :] = zeros
            vb[WINDOW + lk:2 * WINDOW + lk, :] = zeros
            kb[2 * WINDOW + lk:, :] = ck_ref[...].astype(BF16)
            vb[2 * WINDOW + lk:, :] = cv_ref[...].astype(BF16)
            kb[WINDOW:WINDOW + lk, :] = k.astype(BF16)
            vb[WINDOW:WINDOW + lk, :] = v_ref[...].astype(BF16)
        else:
            kb[...] = k.astype(BF16)
            vb[...] = v_ref[...].astype(BF16)

    lane = lax.broadcasted_iota(jnp.int32, (tq, LANES), 1)
    sinks = sk_ref[...]
    if banded:
        start = pl.multiple_of(qi * WINDOW, WINDOW)
        k_loc, v_loc = kb[pl.ds(start, 3 * WINDOW), :], vb[pl.ds(start, 3 * WINDOW), :]
        k_ctx, v_ctx = kb[2 * WINDOW + lk:, :], vb[2 * WINDOW + lk:, :]
        row = lax.broadcasted_iota(jnp.int32, (SWA_GROUP * tq, 3 * WINDOW), 0) % tq
        col = lax.broadcasted_iota(jnp.int32, (SWA_GROUP * tq, 3 * WINDOW), 1)
        kpos = (qi - 1) * WINDOW + col
        visible = (col >= row) & (col - row <= 2 * WINDOW) & (kpos >= 0) & (kpos < lk)
    else:
        k_loc, v_loc = kb[...], vb[...]
    outs = []
    for kh in range(SWA_KV_HEADS):
        in_half = (lane >= HALF) if kh else (lane < HALF)
        qs, sink_rows = [], []
        for g in range(SWA_GROUP):
            head = kh * SWA_GROUP + g
            t = q_ref[:, (head // 2) * LANES:(head // 2 + 1) * LANES] * SWA_SCALE
            if banded:
                t = _rope(t, cq_ref[...], sq_ref[...])
            if head % 2 != kh:
                t = pltpu.roll(t, HALF, 1)
            qs.append(jnp.where(in_half, t, 0.0).astype(BF16))
            sink_rows.append(jnp.broadcast_to(sinks[:, head:head + 1], (tq, 1)))
        qs = jnp.concatenate(qs, axis=0)
        sink = jnp.concatenate(sink_rows, axis=0)
        s_loc = lax.dot_general(qs, k_loc, NT_DIMS, preferred_element_type=F32)
        m = sink
        if banded:
            s_loc = jnp.where(visible, s_loc, NEG_INF)
            s_ctx = lax.dot_general(qs, k_ctx, NT_DIMS, preferred_element_type=F32)
            m = jnp.maximum(m, jnp.max(s_ctx, axis=-1, keepdims=True))
        m = jnp.maximum(m, jnp.max(s_loc, axis=-1, keepdims=True))
        e_loc = jnp.exp(s_loc - m)
        den = jnp.sum(e_loc, axis=-1, keepdims=True) + jnp.exp(sink - m)
        if banded:
            e_ctx = jnp.exp(s_ctx - m)
            den = den + jnp.sum(e_ctx, axis=-1, keepdims=True)
        o = jnp.dot((e_loc / den).astype(BF16), v_loc, preferred_element_type=F32)
        if banded:
            o = o + jnp.dot((e_ctx / den).astype(BF16), v_ctx, preferred_element_type=F32)
        outs.append(o)
    for t in range(SWA_HEADS // 2):
        kh, g0 = (2 * t) // SWA_GROUP, (2 * t) % SWA_GROUP
        a = outs[kh][g0 * tq:(g0 + 1) * tq]
        b = outs[kh][(g0 + 1) * tq:(g0 + 2) * tq]
        a = pltpu.roll(a, HALF, 1) if kh == 1 else a
        b = pltpu.roll(b, HALF, 1) if kh == 0 else b
        o_ref[:, t * LANES:(t + 1) * LANES] = jnp.where(lane < HALF, a, b)


def swa_attention(proj, dst, swa_sink, l, cache_k=None, cache_v=None, tables=None):
    banded = cache_k is not None
    q_col, k_col = 3, (4 * GROUP_W) // LANES
    if banded:
        nb, lk, tq, row0 = DEC_BATCH, DEC_SEQ, WINDOW, T_PROMPT
    else:
        nb, lk, tq, row0 = BATCH, SEQ, SEQ, 0
    in_specs = [pl.BlockSpec((tq, GROUP_W), lambda b, qi: ((row0 + b * lk) // tq + qi, q_col)),
                pl.BlockSpec((lk, LANES), lambda b, qi: (row0 // lk + b, k_col)),
                pl.BlockSpec((lk, LANES), lambda b, qi: (row0 // lk + b, k_col + 1)),
                pl.BlockSpec((None, 1, SWA_HEADS), lambda b, qi: (l, 0, 0))]
    args = [proj, proj, proj, swa_sink.reshape(DEPTH, 1, SWA_HEADS)]
    if banded:
        cos, sin = tables
        ctx_spec = pl.BlockSpec((None, None, PAST_LEN, LANES), lambda b, qi: (b, l, 0, 0))
        q_tab = pl.BlockSpec((tq, LANES), lambda b, qi: (qi, 0))
        k_tab = pl.BlockSpec((lk, LANES), lambda b, qi: (0, 0))
        in_specs += [ctx_spec, ctx_spec, q_tab, q_tab, k_tab, k_tab]
        args += [cache_k.reshape(DEC_BATCH, DEPTH, PAST_LEN, LANES), cache_v.reshape(DEC_BATCH, DEPTH, PAST_LEN, LANES),
                 cos, sin, cos, sin]
    n_keys = lk + 2 * WINDOW + PAST_LEN if banded else lk
    return pl.pallas_call(
        partial(_swa_body, lk=lk, tq=tq, banded=banded),
        grid=(nb, lk // tq),
        in_specs=in_specs + [pl.BlockSpec(memory_space=pl.ANY)],
        out_specs=pl.BlockSpec((tq, GROUP_W), lambda b, qi: ((row0 + b * lk) // tq + qi, 0)),
        out_shape=jax.ShapeDtypeStruct((T_ALL, GROUP_W), F32),
        input_output_aliases={len(args): 0},
        scratch_shapes=[pltpu.VMEM((n_keys, LANES), BF16), pltpu.VMEM((n_keys, LANES), BF16)],
        compiler_params=_cparams("arbitrary", "arbitrary"),
        name="swa_attention_banded" if banded else "swa_attention",
    )(*args, dst)


LRU_GROUPS = LRU_W // LANES


def lru_gate_params(wa, ba, wx, bx):
    def tile_blocks(w):
        w = w.reshape(LRU_GROUPS, 2, LRU_BD, LRU_BD)
        z = jnp.zeros((LRU_GROUPS, LRU_BD, LRU_BD), F32)
        return jnp.concatenate([jnp.concatenate([w[:, 0], z], axis=-1), jnp.concatenate([z, w[:, 1]], axis=-1)], axis=-2)

    w = jnp.concatenate([tile_blocks(wa[0]), tile_blocks(wx[0]), tile_blocks(wa[1]), tile_blocks(wx[1])], axis=-1)
    b = jnp.stack([ba[0], bx[0], ba[1], bx[1]], axis=0).reshape(4, LRU_GROUPS, LANES)
    return w, jnp.moveaxis(b, 0, 1).reshape(LRU_GROUPS, 1, 4 * LANES)


def _shift_rows(x, s, row, n):
    if s == 0:
        return x
    ok = (row >= -s) if s < 0 else (row < n - s)
    return jnp.where(ok, pltpu.roll(x, (-s) % n, 0), 0.0)


def _linear_scan(a, b, row, n, reverse):
    s = 1
    while s < n:
        ok = (row < n - s) if reverse else (row >= s)
        shift = (n - s) if reverse else s
        b = jnp.where(ok, a * pltpu.roll(b, shift, 0) + b, b)
        a = jnp.where(ok, a * pltpu.roll(a, shift, 0), a)
        s *= 2
    return b


def _lru_body(x_ref, lg_ref, cw_ref, w_ref, bias_ref, lam_ref, h0_ref, _aliased_dst, o_ref, st_ref, *, n):
    x = x_ref[...]
    row = lax.broadcasted_iota(jnp.int32, x.shape, 0)
    cw = cw_ref[...]
    xc = _shift_rows(x, -(LRU_CONV // 2), row, n) * cw[0:1]
    for k in range(1, LRU_CONV):
        xc = xc + _shift_rows(x, k - LRU_CONV // 2, row, n) * cw[k:k + 1]
    gates = jnp.dot(xc.astype(BF16), w_ref[...].astype(BF16), preferred_element_type=F32) + bias_ref[...]
    lam = lam_ref[...]
    log_sig = jnp.minimum(lam, 0.0) - jnp.log1p(jnp.exp(-jnp.abs(lam)))
    h0 = h0_ref[...]
    hs = []
    for d in range(2):
        r = jax.nn.sigmoid(gates[:, (2 * d) * LANES:(2 * d + 1) * LANES])
        i = jax.nn.sigmoid(gates[:, (2 * d + 1) * LANES:(2 * d + 2) * LANES])
        log_a = LRU_C * r * log_sig[d:d + 1]
        a = jnp.exp(log_a)
        th = jnp.tanh(log_a)
        b = jnp.sqrt(-2.0 * th / (1.0 - th)) * (i * xc)
        first = n - 1 if d else 0
        b = jnp.where(row == first, b + a * h0[d:d + 1], b)
        hs.append(_linear_scan(a, b, row, n, reverse=bool(d)))
    lg = lg_ref[...]
    gelu = 0.5 * lg * (1.0 + jnp.tanh(math.sqrt(2.0 / math.pi) * (lg + 0.044715 * (lg * lg * lg))))
    o_ref[...] = (hs[0] + hs[1]) * gelu
    st_ref[0:1, :] = hs[0][n - 1:n, :]
    st_ref[1:2, :] = hs[1][0:1, :]


def rglru(proj, dst, gate_w, gate_b, lru_conv, lru_lam, h0, l, h0_l, prompt):
    nb, n, row0 = (BATCH, SEQ, 0) if prompt else (DEC_BATCH, DEC_SEQ, T_PROMPT)
    x_col = (D_IN - 2 * LRU_W) // LANES
    return pl.pallas_call(
        partial(_lru_body, n=n),
        grid=(nb, LRU_GROUPS),
        in_specs=[pl.BlockSpec((n, LANES), lambda s, c: (row0 // n + s, x_col + c)),
                  pl.BlockSpec((n, LANES), lambda s, c: (row0 // n + s, x_col + LRU_GROUPS + c)),
                  pl.BlockSpec((None, LRU_CONV, LANES), lambda s, c: (l, 0, c)),
                  pl.BlockSpec((None, LANES, 4 * LANES), lambda s, c: (c, 0, 0)),
                  pl.BlockSpec((None, 1, 4 * LANES), lambda s, c: (c, 0, 0)),
                  pl.BlockSpec((None, 2, LANES), lambda s, c: (l, 0, c)),
                  pl.BlockSpec((None, None, 2, LANES), lambda s, c: (s, h0_l, 0, c)),
                  pl.BlockSpec(memory_space=pl.ANY)],
        out_specs=[pl.BlockSpec((n, LANES), lambda s, c: (row0 // n + s, c)),
                   pl.BlockSpec((None, 2, LANES), lambda s, c: (s, 0, c))],
        out_shape=[jax.ShapeDtypeStruct((T_ALL, LRU_W), F32), jax.ShapeDtypeStruct((nb, 2, LRU_W), F32)],
        input_output_aliases={7: 0},
        compiler_params=_cparams("arbitrary", "arbitrary"),
        name="rglru_prompt" if prompt else "rglru_sample",
    )(proj, proj, lru_conv, gate_w, gate_b, lru_lam, h0, dst)


HY_P = 256
HY_F = 2 * HY_P


def _split_bf16(x):
    hi = x.astype(BF16)
    return hi, (x - hi.astype(F32)).astype(BF16)


def _dot3(c_hi, c_lo, d):
    d_hi, d_lo = _split_bf16(d)
    return (jnp.dot(c_hi, d_hi, preferred_element_type=F32) + jnp.dot(c_hi, d_lo, preferred_element_type=F32)
            + jnp.dot(c_lo, d_hi, preferred_element_type=F32))


def dft_constants():
    f = np.arange(HY_F)[:, None].astype(np.float64)
    ang = 2.0 * np.pi * f * np.arange(HY_F)[None, :] / HY_F
    c, s = np.cos(ang), np.sin(ang)
    ch, sh = c[:, :HY_P], s[:, :HY_P]
    fwd_data = np.block([[ch, sh], [-sh, ch]])
    fwd_filter = np.concatenate([c, -s], axis=0)
    inverse = np.block([[ch.T, -sh.T], [sh.T, ch.T]]) / HY_F
    out = []
    for m in (fwd_data, fwd_filter, inverse):
        hi = m.astype(np.float32).astype(BF16)
        lo = (m - hi.astype(np.float64)).astype(np.float32).astype(BF16)
        out += [jnp.asarray(hi), jnp.asarray(lo)]
    return out


def _filter_spectra_body(cur_ref, prev_ref, fh_ref, fl_ref, g_ref):
    row = lax.broadcasted_iota(jnp.int32, prev_ref.shape, 0)
    prev = jnp.where((row == 0) | (pl.program_id(1) == 0), 0.0, prev_ref[...])
    g_ref[...] = _dot3(fh_ref[...], fl_ref[...], jnp.concatenate([cur_ref[...], prev], axis=0))


def filter_spectra(filt, fk_hi, fk_lo):
    nq, n, _ = filt.shape
    nblk = n // HY_P
    const = pl.BlockSpec((2 * HY_F, HY_F), lambda q, m: (0, 0))
    return pl.pallas_call(
        _filter_spectra_body,
        grid=(nq, nblk),
        in_specs=[pl.BlockSpec((None, HY_P, HY_W), lambda q, m: (q, m, 0)),
                  pl.BlockSpec((None, HY_P, HY_W), lambda q, m: (q, jnp.maximum(m - 1, 0), 0)),
                  const, const],
        out_specs=pl.BlockSpec((None, None, 2 * HY_F, HY_W), lambda q, m: (q, m, 0, 0)),
        out_shape=jax.ShapeDtypeStruct((nq, nblk, 2 * HY_F, HY_W), F32),
        compiler_params=_cparams("arbitrary", "arbitrary"),
        name="hyena_filter_spectra",
    )(filt, filt, fk_hi, fk_lo)


def _hyena_body(z0_ref, z1_ref, x0_ref, x1_ref, cwz_ref, cwx_ref, skip_ref, gf_ref, gb_ref, fdh_ref, fdl_ref,
                gih_ref, gil_ref, o_ref, zs, xs, us, *, n, conv_z):
    nblk = n // HY_P
    row = lax.broadcasted_iota(jnp.int32, z0_ref.shape, 0)

    def short_conv(x, cw):
        acc = _shift_rows(x, -(HY_SHORT // 2), row, n) * cw[0:1]
        for k in range(1, HY_SHORT):
            acc = acc + _shift_rows(x, k - HY_SHORT // 2, row, n) * cw[k:k + 1]
        return acc

    for b, (z_ref, x_ref) in enumerate(((z0_ref, x0_ref), (z1_ref, x1_ref))):
        zs[b] = short_conv(z_ref[...], cwz_ref[...]) if conv_z else z_ref[...]
        xs[b] = short_conv(x_ref[...], cwx_ref[...])

    def forward(j, carry):
        r0 = pl.multiple_of(j * HY_P, HY_P)
        d = jnp.concatenate([zs[0, pl.ds(r0, HY_P), :], zs[1, pl.ds(r0, HY_P), :]], axis=0)
        us[j] = _dot3(fdh_ref[...], fdl_ref[...], d)
        return carry

    lax.fori_loop(0, nblk, forward, 0)

    def out_block(i, carry):
        def causal(j, acc):
            g, u = gf_ref[i - j], us[j]
            g_re, g_im, u_re, u_im = g[:HY_F], g[HY_F:], u[:HY_F], u[HY_F:]
            return acc[0] + (g_re * u_re - g_im * u_im), acc[1] + (g_re * u_im + g_im * u_re)

        def anticausal(j, acc):
            g, u = gb_ref[j - i], us[j]
            g_re, g_im, u_re, u_im = g[:HY_F], g[HY_F:], u[:HY_F], u[HY_F:]
            return acc[0] + (g_re * u_re + g_im * u_im), acc[1] + (g_re * u_im - g_im * u_re)

        zero = jnp.zeros((HY_F, z0_ref.shape[1]), F32)
        acc = lax.fori_loop(0, i + 1, causal, (zero, zero))
        acc = lax.fori_loop(i, nblk, anticausal, acc)
        y = _dot3(gih_ref[...], gil_ref[...], jnp.concatenate(acc, axis=0))
        r0 = pl.multiple_of(i * HY_P, HY_P)
        for b in range(2):
            conv = y[b * HY_P:(b + 1) * HY_P]
            o_ref[pl.ds(b * n + r0, HY_P), :] = xs[b, pl.ds(r0, HY_P), :] * (conv + skip_ref[...] * zs[b, pl.ds(r0, HY_P), :])
        return carry

    lax.fori_loop(0, nblk, out_block, 0)


def hyena_order(zsrc, z_col0, proj, x_col0, hy_conv, hy_skip, spectra, consts, l, order, prompt):
    fd_hi, fd_lo, _, _, gi_hi, gi_lo = consts
    if prompt:
        n, cw, npair, z_row0 = SEQ, 2 * LANES, BATCH // 2, 0
    else:
        n, cw, npair, z_row0 = DEC_SEQ, LANES, DEC_BATCH // 2, (T_PROMPT // DEC_SEQ if order == 0 else 0)
    x_row0 = 0 if prompt else T_PROMPT // DEC_SEQ
    ncg = HY_W // cw
    nblk = n // HY_P
    hy_col = sum(IN_SIZES[:6])
    once = pl.Buffered(1)
    seq = lambda row0, col0, b: pl.BlockSpec((n, cw), lambda p, c: (row0 + 2 * p + b, col0 // cw + c), pipeline_mode=once)
    taps = lambda col0: pl.BlockSpec((None, HY_SHORT, cw), lambda p, c: (l, 0, col0 // cw + c))
    spec_blk = lambda q: pl.BlockSpec((None, nblk, 2 * HY_F, cw), lambda p, c: (q, 0, 0, c), pipeline_mode=once)
    const = lambda shape: pl.BlockSpec(shape, lambda p, c: (0, 0))
    return pl.pallas_call(
        partial(_hyena_body, n=n, conv_z=(order == 0)),
        grid=(npair, ncg),
        in_specs=[seq(z_row0, z_col0, 0), seq(z_row0, z_col0, 1), seq(x_row0, x_col0, 0), seq(x_row0, x_col0, 1),
                  taps(0), taps(x_col0 - hy_col),
                  pl.BlockSpec((None, 1, cw), lambda p, c: (l * HY_ORDER + order, 0, c)),
                  spec_blk(2 * order), spec_blk(2 * order + 1),
                  const((2 * HY_F, HY_F)), const((2 * HY_F, HY_F)), const((HY_F, 2 * HY_F)), const((HY_F, 2 * HY_F))],
        out_specs=pl.BlockSpec((2 * n, cw), lambda p, c: (p, c)),
        out_shape=jax.ShapeDtypeStruct((npair * 2 * n, HY_W), F32),
        scratch_shapes=[pltpu.VMEM((2, n, cw), F32), pltpu.VMEM((2, n, cw), F32), pltpu.VMEM((nblk, 2 * HY_F, cw), F32)],
        compiler_params=_cparams("arbitrary", "arbitrary"),
        name=f"hyena_{'prompt' if prompt else 'sample'}_order{order}",
    )(zsrc, zsrc, proj, proj, hy_conv, hy_conv, hy_skip.reshape(DEPTH * HY_ORDER, 1, HY_W), spectra, spectra,
      fd_hi, fd_lo, gi_hi, gi_lo)


def hyena_group(proj, filt, hy_conv, hy_skip, consts, l, prompt):
    fwd = filt[:, :, 0].at[0].add(filt[0, :, 1])
    bwd = filt[:, :, 1].at[0].set(0.0)
    filt4 = jnp.moveaxis(jnp.stack([fwd, bwd], axis=2), 0, 2).reshape(2 * HY_ORDER, filt.shape[0], HY_W)
    spectra = filter_spectra(filt4, consts[2], consts[3])
    hy_col = sum(IN_SIZES[:6])
    z = hyena_order(proj, hy_col, proj, hy_col + HY_W, hy_conv, hy_skip, spectra, consts, l, 0, prompt)
    return hyena_order(z, 0, proj, hy_col + 2 * HY_W, hy_conv, hy_skip, spectra, consts, l, 1, prompt)


def moe_layer(x1, mod, p, l):
    h2, idx, gates, rank, counts = router(x1, p['norm2'], mod, p['router_w'], p['router_bias'], l)
    idx, rank = idx[:, :TOP_K], rank[:, :TOP_K]
    nblk = (counts[0] + TM_E - 1) // TM_E
    blk_end = jnp.cumsum(nblk)
    row_start = (blk_end - nblk) * TM_E
    dest = row_start[idx] + rank
    n_valid = blk_end[-1:].astype(jnp.int32)
    blk = jnp.minimum(jnp.arange(N_BLK_MAX, dtype=jnp.int32), n_valid[0] - 1)
    blk_e = jnp.minimum(jnp.searchsorted(blk_end, blk, side='right'), N_EXPERTS - 1).astype(jnp.int32)
    tok = jnp.broadcast_to(jnp.arange(T_ALL, dtype=jnp.int32)[:, None], dest.shape)
    src = jnp.zeros((N_BLK_MAX * TM_E,), jnp.int32).at[dest.reshape(-1)].set(tok.reshape(-1))
    ybuf = routed_experts(h2, src, blk_e, n_valid, p['moe_w_gu'], p['moe_w_dn'], l)
    return shared_combine(h2, p['sh_w_gu'], p['sh_w_dn'], ybuf, dest, gates, x1, mod, l)


def mixer_heads(proj, p, lam_init, ctx):
    B, L, _ = proj.shape
    dq, dk, dv, sq, sk, sv, hy, lx, lg = jnp.split(proj, np.cumsum(IN_SIZES)[:-1].tolist(), axis=-1)
    dq = dq.reshape(B, L, DIFF_HEADS, 2, DIFF_QK) * DIFF_SCALE
    dk = dk.reshape(B, L, DIFF_HEADS, 2, DIFF_QK)
    dv = dv.reshape(B, L, DIFF_HEADS, DIFF_VD)
    sq = sq.reshape(B, L, SWA_KV_HEADS, SWA_GROUP, HEAD_DIM) * SWA_SCALE
    sk = sk.reshape(B, L, SWA_KV_HEADS, HEAD_DIM)
    sv = sv.reshape(B, L, SWA_KV_HEADS, HEAD_DIM)
    lq1, lk1, lq2, lk2 = p['diff_lam'].astype(F32)
    lam = jnp.exp(jnp.sum(lq1 * lk1)) - jnp.exp(jnp.sum(lq2 * lk2)) + lam_init
    sink = p['swa_sink'].reshape(SWA_KV_HEADS, SWA_GROUP)
    if ctx is None:
        d_out = diff_attend(dq, dk, dv, lam)
        s_out = swa_dense(sq, sk, sv, sink)
        h0_f = jnp.zeros((B, LRU_W), F32)
        h0_b = jnp.zeros((B, LRU_W), F32)
    else:
        ck_d, cv_d, ck_s, cv_s, st = ctx
        cos, sin = axial_rope(L)
        dq_r = apply_rope(dq.reshape(B, L, 2 * DIFF_HEADS, DIFF_QK), cos, sin).reshape(dq.shape)
        dk_r = apply_rope(dk.reshape(B, L, 2 * DIFF_HEADS, DIFF_QK), cos, sin).reshape(dk.shape)
        d_out = diff_attend(dq_r, jnp.concatenate([dk_r, ck_d], axis=1), jnp.concatenate([dv, cv_d], axis=1), lam)
        sq_r = apply_rope(sq.reshape(B, L, SWA_HEADS, HEAD_DIM), cos, sin).reshape(sq.shape)
        s_out = swa_banded(sq_r, apply_rope(sk, cos, sin), sv, ck_s, cv_s, sink)
        h0_f, h0_b = st[:, 0], st[:, 1]
    d_out = (rmsnorm(d_out, p['diff_subln']) * (1.0 - lam_init)).reshape(B, L, DIFF_HEADS * DIFF_VD)
    hy_out = hyena_mix(hy, p)
    lru_in = dwconv(lx, p['lru_conv'], LRU_CONV // 2)
    hl, hf_last, hb_first = rglru_bidir(lru_in, p['lru_wa'], p['lru_ba'], p['lru_wx'], p['lru_bx'], p['lru_lam'], h0_f, h0_b)
    lru_out = hl.astype(proj.dtype) * jax.nn.gelu(lg)
    mix = jnp.concatenate([d_out, s_out, hy_out, lru_out], axis=-1).reshape(B * L, D_MIX)
    if ctx is None:
        return mix, (dk, dv, sk, sv, jnp.stack([hf_last, hb_first], axis=1).astype(proj.dtype))
    return mix, None


def kernel(x_prompt, x_sample, cache_diff_k, cache_diff_v, cache_swa_k, cache_swa_v, state_lru, c, c_ctx,
           w_mod, b_mod, norm1, norm2, w_in, w_out, diff_lam, diff_subln, swa_sink, hy_conv, hy_w1, hy_b1,
           hy_w2, hy_b2, hy_w3, hy_freq, hy_log_rate, hy_skip, lru_conv, lru_wa, lru_ba, lru_wx, lru_bx, lru_lam,
           router_w, router_bias, moe_w_gu, moe_w_dn, sh_w_gu, sh_w_dn, final_norm):
    x = jnp.concatenate([x_prompt.reshape(T_PROMPT, D_MODEL), x_sample.reshape(T_SAMPLE, D_MODEL)], axis=0)
    cond = jnp.concatenate([c_ctx[None, :], c, jnp.zeros((8 - N_COND, D_MODEL), F32)], axis=0)
    dk_l, dv_l, sk_l, sv_l, st_l = [], [], [], [], []
    tables = rope_tables(DEC_SEQ)
    consts = dft_constants()
    h0_prompt = jnp.zeros((BATCH, 1, 2, LRU_W), F32)
    col = np.cumsum((0,) + IN_SIZES).tolist()
    empty = lambda: jnp.zeros((T_ALL, GROUP_W), F32)
    for l in range(DEPTH):
        p = {'hy_conv': hy_conv[l], 'hy_w1': hy_w1[l], 'hy_b1': hy_b1[l], 'hy_w2': hy_w2[l], 'hy_b2': hy_b2[l],
             'hy_w3': hy_w3[l], 'hy_freq': hy_freq[l], 'hy_log_rate': hy_log_rate[l], 'hy_skip': hy_skip[l],
             'norm2': norm2, 'router_w': router_w, 'router_bias': router_bias, 'moe_w_gu': moe_w_gu,
             'moe_w_dn': moe_w_dn, 'sh_w_gu': sh_w_gu, 'sh_w_dn': sh_w_dn}
        lam_init = 0.8 - 0.6 * math.exp(-0.3 * l)
        mod = modulation(cond, w_mod, b_mod, l)[:N_COND].reshape(N_COND * 6, 1, D_MODEL)
        proj = in_proj(x, norm1, mod, w_in, l)
        proj_p = proj[:T_PROMPT].reshape(BATCH, SEQ, D_IN)
        dk_l.append(proj_p[..., col[1]:col[2]].reshape(BATCH, SEQ, DIFF_HEADS, 2, DIFF_QK))
        dv_l.append(proj_p[..., col[2]:col[3]].reshape(BATCH, SEQ, DIFF_HEADS, DIFF_VD))
        sk_l.append(proj_p[..., col[4]:col[5]].reshape(BATCH, SEQ, SWA_KV_HEADS, HEAD_DIM))
        sv_l.append(proj_p[..., col[5]:col[6]].reshape(BATCH, SEQ, SWA_KV_HEADS, HEAD_DIM))
        d_out = diff_attention(proj, empty(), diff_lam, diff_subln, l, lam_init)
        d_out = diff_attention(proj, d_out, diff_lam, diff_subln, l, lam_init, cache_diff_k, cache_diff_v, tables)
        s_out = swa_attention(proj, empty(), swa_sink, l)
        s_out = swa_attention(proj, s_out, swa_sink, l, cache_swa_k, cache_swa_v, tables)
        hy_args = (p['hy_w1'], p['hy_b1'], p['hy_w2'], p['hy_b2'], p['hy_w3'], p['hy_freq'], p['hy_log_rate'])
        hy_out = jnp.concatenate(
            [hyena_group(proj, hyena_filters(SEQ, *hy_args), hy_conv, hy_skip, consts, l, True),
             hyena_group(proj, hyena_filters(DEC_SEQ, *hy_args), hy_conv, hy_skip, consts, l, False)], axis=0)
        gate_w, gate_b = lru_gate_params(lru_wa[l], lru_ba[l], lru_wx[l], lru_bx[l])
        lru_out, st = rglru(proj, empty(), gate_w, gate_b, lru_conv, lru_lam, h0_prompt, l, 0, True)
        lru_out, _ = rglru(proj, lru_out, gate_w, gate_b, lru_conv, lru_lam, state_lru, l, l, False)
        st_l.append(st)
        x1 = out_proj([d_out, s_out, hy_out, lru_out], w_out, x, mod, l)
        x = moe_layer(x1, mod, p, l)
    y = final_rmsnorm(x, final_norm)
    y_prompt = y[:T_PROMPT].reshape(BATCH, SEQ, D_MODEL)
    y_sample = y[T_PROMPT:].reshape(DEC_BATCH, DEC_SEQ, D_MODEL)
    return (y_prompt, y_sample, jnp.stack(dk_l, axis=1), jnp.stack(dv_l, axis=1), jnp.stack(sk_l, axis=1),
            jnp.stack(sv_l, axis=1), jnp.stack(st_l, axis=1))
```

```python
import math
from functools import partial
import jax, jax.numpy as jnp
from jax import lax
import numpy as np
from jax.experimental import pallas as pl
from jax.experimental.pallas import tpu as pltpu

D_MODEL = 2048
BATCH = 16
SEQ = 256
DEPTH = 2
DEC_BATCH = 2
DEC_SEQ = 4096
PAST_LEN = 256

GRID_W = 64
EPS = 1e-6
F32 = jnp.float32
QBLK = 128
GROUP_W = D_MODEL // 4
D_MIX = 4 * GROUP_W
DIFF_QK = 64
DIFF_VD = 2 * DIFF_QK
DIFF_HEADS = GROUP_W // DIFF_VD
DIFF_SCALE = DIFF_QK ** -0.5
HEAD_DIM = 64
SWA_HEADS = GROUP_W // HEAD_DIM
SWA_KV_HEADS = SWA_HEADS // 4
SWA_GROUP = SWA_HEADS // SWA_KV_HEADS
SWA_SCALE = HEAD_DIM ** -0.5
WINDOW = 128
ROPE_PAIRS = HEAD_DIM // 4
ROPE_BASE = 10000.0
HY_W = GROUP_W
HY_ORDER = 2
HY_SHORT = 3
HY_BANDS = 16
HY_POS_DIM = 1 + 2 * HY_BANDS
HY_HID = 64
LRU_W = GROUP_W
LRU_BLOCKS = 8
LRU_BD = LRU_W // LRU_BLOCKS
LRU_CONV = 4
LRU_C = 8.0
N_EXPERTS = 64
TOP_K = 6
EXPERT_FF = D_MODEL // 4
SHARED_FF = EXPERT_FF
ROUTE_SCALE = 2.5
MOE_BLK = 128
NEG_INF = -1e30
IN_SIZES = (2 * DIFF_HEADS * DIFF_QK, 2 * DIFF_HEADS * DIFF_QK, DIFF_HEADS * DIFF_VD, SWA_HEADS * HEAD_DIM, SWA_KV_HEADS * HEAD_DIM, SWA_KV_HEADS * HEAD_DIM, (HY_ORDER + 1) * HY_W, LRU_W, LRU_W)
D_IN = sum(IN_SIZES)


def rmsnorm(x, g):
    xf = x.astype(F32)
    y = xf * lax.rsqrt(jnp.mean(xf * xf, axis=-1, keepdims=True) + EPS)
    return (y * g.astype(F32)).astype(x.dtype)


def dwconv(x, w, left):
    K, L = w.shape[0], x.shape[1]
    xp = jnp.pad(x, ((0, 0), (left, K - 1 - left), (0, 0)))
    acc = xp[:, 0:L] * w[0]
    for k in range(1, K):
        acc = acc + xp[:, k:k + L] * w[k]
    return acc


def axial_rope(L):
    rows = L // GRID_W
    row = jnp.repeat(jnp.arange(rows), GRID_W).astype(F32)
    col = jnp.tile(jnp.arange(GRID_W), rows).astype(F32)
    inv = ROPE_BASE ** (-jnp.arange(ROPE_PAIRS, dtype=F32) / ROPE_PAIRS)
    ang = jnp.stack([row[:, None] * inv, col[:, None] * inv], axis=1)
    return jnp.cos(ang), jnp.sin(ang)


def apply_rope(x, cos, sin):
    xs = x.reshape(*x.shape[:-1], 2, 2, ROPE_PAIRS)
    c = cos[:, None, :, None, :].astype(x.dtype)
    s = sin[:, None, :, None, :].astype(x.dtype)
    x1, x2 = xs[..., 0:1, :], xs[..., 1:2, :]
    return jnp.concatenate([x1 * c - x2 * s, x2 * c + x1 * s], axis=-2).reshape(x.shape)


def diff_attend(q, k, v, lam):
    B, Lq = q.shape[:2]
    nb = Lq // QBLK
    qb = jnp.moveaxis(q.reshape(B, nb, QBLK, *q.shape[2:]), 1, 0)

    def one(qblk):
        s = jnp.einsum('bqhmd,bkhmd->bhmqk', qblk, k, preferred_element_type=F32)
        p = jax.nn.softmax(s, axis=-1)
        w = p[:, :, 0] - lam * p[:, :, 1]
        return jnp.einsum('bhqk,bkhd->bqhd', w.astype(v.dtype), v)

    o = lax.map(one, qb)
    return jnp.moveaxis(o, 0, 1).reshape(B, Lq, q.shape[2], v.shape[-1])


def sink_softmax(s, sink):
    sb = sink.astype(F32)[None, :, :, None, None]
    m = jnp.maximum(jnp.max(s, axis=-1, keepdims=True), sb)
    e = jnp.exp(s - m)
    return e / (jnp.sum(e, axis=-1, keepdims=True) + jnp.exp(sb - m))


def swa_dense(q, k, v, sink):
    B, L = q.shape[:2]
    nb = L // QBLK
    qb = jnp.moveaxis(q.reshape(B, nb, QBLK, *q.shape[2:]), 1, 0)

    def one(qblk):
        p = sink_softmax(jnp.einsum('bqkgd,bnkd->bkgqn', qblk, k, preferred_element_type=F32), sink)
        return jnp.einsum('bkgqn,bnkd->bqkgd', p.astype(v.dtype), v)

    o = lax.map(one, qb)
    return jnp.moveaxis(o, 0, 1).reshape(B, L, SWA_HEADS * HEAD_DIM)


def swa_banded(q, k, v, k_ctx, v_ctx, sink):
    B, L = q.shape[:2]
    nb = L // WINDOW
    pad = lambda t: jnp.pad(t, ((0, 0), (WINDOW, WINDOW), (0, 0), (0, 0))).reshape(B, nb + 2, WINDOW, *t.shape[2:])
    kb, vb = pad(k), pad(v)
    kwin = jnp.concatenate([kb[:, :nb], kb[:, 1:nb + 1], kb[:, 2:]], axis=2)
    vwin = jnp.concatenate([vb[:, :nb], vb[:, 1:nb + 1], vb[:, 2:]], axis=2)
    qi = jnp.arange(WINDOW)[:, None]
    kj = jnp.arange(3 * WINDOW)[None, :]
    rel = kj - qi
    band = (rel >= 0) & (rel <= 2 * WINDOW)
    qb = jnp.moveaxis(q.reshape(B, nb, WINDOW, *q.shape[2:]), 1, 0)

    def one(args):
        b, qblk, kw, vw = args
        kpos = (b - 1) * WINDOW + kj
        mask = band & (kpos >= 0) & (kpos < L)
        s_loc = jnp.where(mask, jnp.einsum('bqkgd,bnkd->bkgqn', qblk, kw, preferred_element_type=F32), NEG_INF)
        s_ctx = jnp.einsum('bqkgd,bnkd->bkgqn', qblk, k_ctx, preferred_element_type=F32)
        p = sink_softmax(jnp.concatenate([s_loc, s_ctx], axis=-1), sink).astype(v.dtype)
        return (jnp.einsum('bkgqn,bnkd->bqkgd', p[..., :3 * WINDOW], vw)
                + jnp.einsum('bkgqn,bnkd->bqkgd', p[..., 3 * WINDOW:], v_ctx))

    o = lax.map(one, (jnp.arange(nb), qb, jnp.moveaxis(kwin, 1, 0), jnp.moveaxis(vwin, 1, 0)))
    return jnp.moveaxis(o, 0, 1).reshape(B, L, SWA_HEADS * HEAD_DIM)


def hyena_filters(L, w1, b1, w2, b2, w3, freq, log_rate):
    tn = jnp.arange(L, dtype=F32) / L
    ang = 2.0 * math.pi * tn[:, None] * jnp.arange(1, HY_BANDS + 1, dtype=F32)
    z = jnp.concatenate([tn[:, None], jnp.sin(ang), jnp.cos(ang)], axis=-1)
    f = jnp.sin(freq[0].astype(F32) * (z @ w1.astype(F32) + b1.astype(F32)))
    f = jnp.sin(freq[1].astype(F32) * (f @ w2.astype(F32) + b2.astype(F32)))
    f = (f @ w3.astype(F32)).reshape(L, HY_ORDER, 2, HY_W)
    f = f * jnp.exp(-tn[:, None, None, None] * jnp.exp(log_rate.astype(F32)))
    return f * lax.rsqrt(jnp.sum(f * f, axis=(0, 2), keepdims=True) + EPS)


def bidir_fftconv(u, kf, kb):
    L, C = u.shape[1], u.shape[2]
    kfull = jnp.concatenate([kf.at[0].add(kb[0]), jnp.zeros((1, C), F32), kb[1:][::-1]], axis=0)
    Kf = jnp.fft.rfft(kfull, axis=0)
    Uf = jnp.fft.rfft(u, n=2 * L, axis=1)
    return jnp.fft.irfft(Uf * Kf[None], n=2 * L, axis=1)[:, :L]


def hyena_mix(u, p):
    L = u.shape[1]
    u = dwconv(u, p['hy_conv'], HY_SHORT // 2)
    parts = jnp.split(u, HY_ORDER + 1, axis=-1)
    filt = hyena_filters(L, p['hy_w1'], p['hy_b1'], p['hy_w2'], p['hy_b2'], p['hy_w3'], p['hy_freq'], p['hy_log_rate'])
    skip = p['hy_skip'].astype(F32)
    z = parts[0].astype(F32)
    for o in range(HY_ORDER):
        z = parts[o + 1].astype(F32) * (bidir_fftconv(z, filt[:, o, 0], filt[:, o, 1]) + skip[o] * z)
    return z.astype(u.dtype)


def _lin_combine(e1, e2):
    a1, b1 = e1
    a2, b2 = e2
    return a1 * a2, a2 * b1 + b2


def rglru_bidir(x, wa, ba, wx, bx, lam, h0_f, h0_b):
    B, L, W = x.shape
    xb = x.reshape(B, L, LRU_BLOCKS, LRU_BD)
    hs = []
    for d, (h0, rev) in enumerate(((h0_f, False), (h0_b, True))):
        r = jax.nn.sigmoid(jnp.einsum('blnd,nde->blne', xb, wa[d]).reshape(B, L, W) + ba[d])
        i = jax.nn.sigmoid(jnp.einsum('blnd,nde->blne', xb, wx[d]).reshape(B, L, W) + bx[d])
        log_a = LRU_C * r.astype(F32) * jax.nn.log_sigmoid(lam[d].astype(F32))
        a = jnp.exp(log_a)
        bterm = jnp.sqrt(-jnp.expm1(2.0 * log_a)) * (i * x).astype(F32)
        first = L - 1 if rev else 0
        bterm = bterm.at[:, first].add(a[:, first] * h0.astype(F32))
        _, h = lax.associative_scan(_lin_combine, (a, bterm), axis=1, reverse=rev)
        hs.append(h)
    return hs[0] + hs[1], hs[0][:, -1], hs[1][:, 0]


BF16 = jnp.bfloat16
T_PROMPT = BATCH * SEQ
T_SAMPLE = DEC_BATCH * DEC_SEQ
T_ALL = T_PROMPT + T_SAMPLE
N_COND = 1 + DEC_BATCH
TM = 1024
TM_SH = 256
TN_IN = 256
TN_OUT = 512
TN_MOD = 1024
TM_E = 256
N_ASSIGN = T_ALL * TOP_K
N_BLK_MAX = N_ASSIGN // TM_E + N_EXPERTS
LANES = 128
VMEM_LIMIT = 56 * 1024 * 1024


def _cparams(*sem):
    return pltpu.CompilerParams(dimension_semantics=sem, vmem_limit_bytes=VMEM_LIMIT)


def _cond_row(i, tm):
    return jnp.where(i < T_PROMPT // tm, 0, 1 + (i - T_PROMPT // tm) // (DEC_SEQ // tm))


def _mod_spec(which, tm, tn=D_MODEL):
    if tn == D_MODEL:
        return pl.BlockSpec((1, 1, D_MODEL), lambda i, *_: (_cond_row(i, tm) * 6 + which, 0, 0))
    return pl.BlockSpec((1, 1, tn), lambda i, j: (_cond_row(i, tm) * 6 + which, 0, j))


def _mod_body(c_ref, w_ref, b_ref, o_ref):
    c = c_ref[...]
    a = (c * jax.nn.sigmoid(c)).astype(BF16)
    o_ref[...] = jnp.dot(a, w_ref[...].astype(BF16), preferred_element_type=F32) + b_ref[0]


def modulation(cond, w_mod, b_mod, l):
    n = 6 * D_MODEL
    return pl.pallas_call(
        _mod_body,
        grid=(n // TN_MOD,),
        in_specs=[pl.BlockSpec((8, D_MODEL), lambda j: (0, 0)),
                  pl.BlockSpec((None, D_MODEL, TN_MOD), lambda j: (l, 0, j)),
                  pl.BlockSpec((None, 1, TN_MOD), lambda j: (l, 0, j))],
        out_specs=pl.BlockSpec((8, TN_MOD), lambda j: (0, j)),
        out_shape=jax.ShapeDtypeStruct((8, n), F32),
        compiler_params=_cparams("arbitrary"),
        name="modulation",
    )(cond, w_mod, b_mod.reshape(DEPTH, 1, n))


def _norm_mod(x, g, sc, sh):
    y = x * lax.rsqrt(jnp.mean(x * x, axis=-1, keepdims=True) + EPS) * g
    return y * (1.0 + sc) + sh


def _in_proj_body(x_ref, g_ref, sc_ref, sh_ref, w_ref, o_ref, hb_ref):
    @pl.when(pl.program_id(1) == 0)
    def _():
        hb_ref[...] = _norm_mod(x_ref[...], g_ref[0], sc_ref[0], sh_ref[0]).astype(BF16)

    o_ref[...] = jnp.dot(hb_ref[...], w_ref[...].astype(BF16), preferred_element_type=F32)


def in_proj(x, norm1, mod, w_in, l):
    return pl.pallas_call(
        _in_proj_body,
        grid=(T_ALL // TM, D_IN // TN_IN),
        in_specs=[pl.BlockSpec((TM, D_MODEL), lambda i, j: (i, 0)),
                  pl.BlockSpec((None, 1, D_MODEL), lambda i, j: (l, 0, 0)),
                  _mod_spec(1, TM), _mod_spec(0, TM),
                  pl.BlockSpec((None, D_MODEL, TN_IN), lambda i, j: (l, 0, j))],
        out_specs=pl.BlockSpec((TM, TN_IN), lambda i, j: (i, j)),
        out_shape=jax.ShapeDtypeStruct((T_ALL, D_IN), F32),
        scratch_shapes=[pltpu.VMEM((TM, D_MODEL), BF16)],
        compiler_params=_cparams("arbitrary", "arbitrary"),
        name="in_proj",
    )(x, norm1.reshape(DEPTH, 1, D_MODEL), mod, mod, w_in)


N_MIX = D_MIX // GROUP_W


def _out_proj_body(*refs):
    m_refs, (w_ref, x_ref, g1_ref, o_ref, mb_ref) = refs[:N_MIX], refs[N_MIX:]

    @pl.when(pl.program_id(1) == 0)
    def _():
        for g in range(N_MIX):
            mb_ref[g] = m_refs[g][...].astype(BF16)

    acc = jnp.dot(mb_ref[0], w_ref[0:GROUP_W, :].astype(BF16), preferred_element_type=F32)
    for g in range(1, N_MIX):
        acc = acc + jnp.dot(mb_ref[g], w_ref[g * GROUP_W:(g + 1) * GROUP_W, :].astype(BF16), preferred_element_type=F32)
    o_ref[...] = x_ref[...] + g1_ref[0] * acc


def out_proj(mixes, w_out, x, mod, l):
    return pl.pallas_call(
        _out_proj_body,
        grid=(T_ALL // TM, D_MODEL // TN_OUT),
        in_specs=[pl.BlockSpec((TM, GROUP_W), lambda i, j: (i, 0))] * N_MIX + [
                  pl.BlockSpec((None, D_MIX, TN_OUT), lambda i, j: (l, 0, j)),
                  pl.BlockSpec((TM, TN_OUT), lambda i, j: (i, j)),
                  _mod_spec(2, TM, TN_OUT)],
        out_specs=pl.BlockSpec((TM, TN_OUT), lambda i, j: (i, j)),
        out_shape=jax.ShapeDtypeStruct((T_ALL, D_MODEL), F32),
        scratch_shapes=[pltpu.VMEM((N_MIX, TM, GROUP_W), BF16)],
        compiler_params=_cparams("arbitrary", "arbitrary"),
        name="out_proj",
    )(*mixes, w_out, x, mod)


def _router_body(x_ref, g_ref, sc_ref, sh_ref, wr_ref, rb_ref, h_ref, idx_ref, gate_ref, rank_ref, cnt_ref, carry_ref):
    i = pl.program_id(0)

    @pl.when(i == 0)
    def _():
        carry_ref[...] = jnp.zeros_like(carry_ref)

    h = _norm_mod(x_ref[...], g_ref[0], sc_ref[0], sh_ref[0])
    h_ref[...] = h
    h_hi, h_lo = _split_bf16(h)
    w_hi, w_lo = _split_bf16(wr_ref[...])
    logits = (jnp.dot(h_hi, w_hi, preferred_element_type=F32) + jnp.dot(h_lo, w_hi, preferred_element_type=F32)
              + jnp.dot(h_hi, w_lo, preferred_element_type=F32))
    s = jax.nn.sigmoid(logits)
    cur = s + rb_ref[0]
    e_iota = lax.broadcasted_iota(jnp.int32, s.shape, 1).astype(F32)
    lane = lax.broadcasted_iota(jnp.int32, (TM, LANES), 1)
    r_iota = lax.broadcasted_iota(jnp.int32, (TM, TM), 0)
    c_iota = lax.broadcasted_iota(jnp.int32, (TM, TM), 1)
    tri = (c_iota <= r_iota).astype(BF16)
    idx_out = jnp.zeros((TM, LANES), F32)
    gate_out = jnp.zeros((TM, LANES), F32)
    rank_out = jnp.zeros((TM, LANES), F32)
    gsum = jnp.zeros((TM, 1), F32)
    carry = carry_ref[...]
    for k in range(TOP_K):
        m = jnp.max(cur, axis=-1, keepdims=True)
        ek = jnp.min(jnp.where(cur == m, e_iota, float(N_EXPERTS)), axis=-1, keepdims=True)
        hit = e_iota == ek
        gk = jnp.sum(jnp.where(hit, s, 0.0), axis=-1, keepdims=True)
        cur = jnp.where(hit, -jnp.inf, cur)
        onehot = hit.astype(BF16)
        cum = jnp.dot(tri, onehot, preferred_element_type=F32)
        rk = jnp.sum(jnp.where(hit, carry + cum, 0.0), axis=-1, keepdims=True) - 1.0
        carry = carry + cum[TM - 1:TM, :]
        gsum = gsum + gk
        idx_out = jnp.where(lane == k, ek, idx_out)
        gate_out = jnp.where(lane == k, gk, gate_out)
        rank_out = jnp.where(lane == k, rk, rank_out)
    carry_ref[...] = carry
    idx_ref[...] = idx_out.astype(jnp.int32)
    gate_ref[...] = ROUTE_SCALE * gate_out / gsum
    rank_ref[...] = rank_out.astype(jnp.int32)
    cnt_ref[...] = carry.astype(jnp.int32)


def router(x1, norm2, mod, router_w, router_bias, l):
    tok_out = lambda dt: jax.ShapeDtypeStruct((T_ALL, LANES), dt)
    tok_spec = pl.BlockSpec((TM, LANES), lambda i: (i, 0))
    return pl.pallas_call(
        _router_body,
        grid=(T_ALL // TM,),
        in_specs=[pl.BlockSpec((TM, D_MODEL), lambda i: (i, 0)),
                  pl.BlockSpec((None, 1, D_MODEL), lambda i: (l, 0, 0)),
                  _mod_spec(4, TM), _mod_spec(3, TM),
                  pl.BlockSpec((None, D_MODEL, N_EXPERTS), lambda i: (l, 0, 0)),
                  pl.BlockSpec((None, 1, N_EXPERTS), lambda i: (l, 0, 0))],
        out_specs=[pl.BlockSpec((TM, D_MODEL), lambda i: (i, 0)), tok_spec, tok_spec, tok_spec,
                   pl.BlockSpec((1, N_EXPERTS), lambda i: (0, 0))],
        out_shape=[jax.ShapeDtypeStruct((T_ALL, D_MODEL), F32), tok_out(jnp.int32), tok_out(F32), tok_out(jnp.int32),
                   jax.ShapeDtypeStruct((1, N_EXPERTS), jnp.int32)],
        scratch_shapes=[pltpu.VMEM((1, N_EXPERTS), F32)],
        compiler_params=_cparams("arbitrary"),
        name="router",
    )(x1, norm2.reshape(DEPTH, 1, D_MODEL), mod, mod, router_w, router_bias.reshape(DEPTH, 1, N_EXPERTS))


def _experts_body(be_ref, nv_ref, src_ref, nxt_ref, h_hbm, wgu_ref, wdn_ref, y_ref, wgu_b, wdn_b, xg0, xg1, sems):
    i = pl.program_id(0)
    n_valid = nv_ref[0]

    def row_copy(idx_ref, t, buf, sem):
        return pltpu.make_async_copy(h_hbm.at[pl.ds(idx_ref[0, t], 1)], buf.at[pl.ds(t, 1)], sem)

    def all_rows(buf, sem):
        return pltpu.make_async_copy(h_hbm.at[pl.ds(0, TM_E)], buf, sem)

    @pl.when(i == 0)
    def _():
        def issue(t, carry):
            row_copy(src_ref, t, xg0, sems.at[0]).start()
            return carry

        lax.fori_loop(0, TM_E, issue, 0, unroll=8)

    @pl.when(i < n_valid)
    def _():
        prev = be_ref[jnp.maximum(i - 1, 0)]

        @pl.when(jnp.logical_or(i == 0, be_ref[i] != prev))
        def _():
            wgu_b[...] = wgu_ref[...].astype(BF16)
            wdn_b[...] = wdn_ref[...].astype(BF16)

        for parity, (cur, nxt) in enumerate(((xg0, xg1), (xg1, xg0))):
            @pl.when(i % 2 == parity)
            def _():
                all_rows(cur, sems.at[parity]).wait()
                for t in range(TM_E):
                    row_copy(nxt_ref, t, nxt, sems.at[1 - parity]).start()
                hmid = jnp.dot(cur[...].astype(BF16), wgu_b[...], preferred_element_type=F32)
                a, b = hmid[:, :EXPERT_FF], hmid[:, EXPERT_FF:]
                act = (a * jax.nn.sigmoid(a) * b).astype(BF16)
                y_ref[...] = jnp.dot(act, wdn_b[...], preferred_element_type=F32)

                @pl.when(i == n_valid - 1)
                def _():
                    all_rows(nxt, sems.at[1 - parity]).wait()

    @pl.when(i >= n_valid)
    def _():
        y_ref[...] = jnp.zeros_like(y_ref)


def routed_experts(h2, src, blk_e, n_valid, moe_w_gu, moe_w_dn, l):
    idx_spec = lambda off: pl.BlockSpec((None, 1, TM_E), lambda i, be, nv: (jnp.minimum(i + off, nv[0] - 1), 0, 0),
                                        memory_space=pltpu.SMEM)
    return pl.pallas_call(
        _experts_body,
        grid_spec=pltpu.PrefetchScalarGridSpec(
            num_scalar_prefetch=2,
            grid=(N_BLK_MAX,),
            in_specs=[idx_spec(0), idx_spec(1),
                      pl.BlockSpec(memory_space=pl.ANY),
                      pl.BlockSpec((None, None, D_MODEL, 2 * EXPERT_FF), lambda i, be, nv: (l, be[i], 0, 0)),
                      pl.BlockSpec((None, None, EXPERT_FF, D_MODEL), lambda i, be, nv: (l, be[i], 0, 0))],
            out_specs=pl.BlockSpec((TM_E, D_MODEL), lambda i, be, nv: (i, 0)),
            scratch_shapes=[pltpu.VMEM((D_MODEL, 2 * EXPERT_FF), BF16), pltpu.VMEM((EXPERT_FF, D_MODEL), BF16),
                            pltpu.VMEM((TM_E, D_MODEL), F32), pltpu.VMEM((TM_E, D_MODEL), F32),
                            pltpu.SemaphoreType.DMA((2,))]),
        out_shape=jax.ShapeDtypeStruct((N_BLK_MAX * TM_E, D_MODEL), F32),
        compiler_params=_cparams("arbitrary"),
        name="routed_experts",
    )(blk_e, n_valid, src.reshape(N_BLK_MAX, 1, TM_E), src.reshape(N_BLK_MAX, 1, TM_E), h2, moe_w_gu, moe_w_dn)


def _gathered_rows(ybuf_hbm, rows, sem, n):
    return pltpu.make_async_copy(ybuf_hbm.at[pl.ds(0, n)], rows.at[pl.ds(0, n)], sem)


def _shared_body(dest_ref, gate_ref, h_ref, wgu_ref, wdn_ref, x_ref, g2_ref, ybuf_hbm, o_ref, wgu_b, wdn_b, rows, sem):
    @pl.when(pl.program_id(0) == 0)
    def _():
        wgu_b[...] = wgu_ref[...].astype(BF16)
        wdn_b[...] = wdn_ref[...].astype(BF16)

    for t in range(TM_SH):
        for k in range(TOP_K):
            src = dest_ref[0, t * TOP_K + k]
            pltpu.make_async_copy(ybuf_hbm.at[pl.ds(src, 1)], rows.at[pl.ds(k * TM_SH + t, 1)], sem).start()
    hmid = jnp.dot(h_ref[...].astype(BF16), wgu_b[...], preferred_element_type=F32)
    a, b = hmid[:, :SHARED_FF], hmid[:, SHARED_FF:]
    act = (a * jax.nn.sigmoid(a) * b).astype(BF16)
    y = jnp.dot(act, wdn_b[...], preferred_element_type=F32)
    _gathered_rows(ybuf_hbm, rows, sem, TOP_K * TM_SH).wait()
    gates = gate_ref[...]
    for k in range(TOP_K):
        y = y + gates[:, k:k + 1] * rows[k * TM_SH:(k + 1) * TM_SH, :]
    o_ref[...] = x_ref[...] + g2_ref[0] * y


def shared_combine(h2, sh_w_gu, sh_w_dn, ybuf, dest, gates, x1, mod, l):
    tile = pl.BlockSpec((TM_SH, D_MODEL), lambda i: (i, 0))
    once = pl.Buffered(1)
    n_tiles = T_ALL // TM_SH
    return pl.pallas_call(
        _shared_body,
        grid=(n_tiles,),
        in_specs=[pl.BlockSpec((None, 1, TM_SH * TOP_K), lambda i: (i, 0, 0), memory_space=pltpu.SMEM),
                  pl.BlockSpec((TM_SH, LANES), lambda i: (i, 0)),
                  tile,
                  pl.BlockSpec((None, D_MODEL, 2 * SHARED_FF), lambda i: (l, 0, 0), pipeline_mode=once),
                  pl.BlockSpec((None, SHARED_FF, D_MODEL), lambda i: (l, 0, 0), pipeline_mode=once),
                  tile, _mod_spec(5, TM_SH),
                  pl.BlockSpec(memory_space=pl.ANY)],
        out_specs=tile,
        out_shape=jax.ShapeDtypeStruct((T_ALL, D_MODEL), F32),
        scratch_shapes=[pltpu.VMEM((D_MODEL, 2 * SHARED_FF), BF16), pltpu.VMEM((SHARED_FF, D_MODEL), BF16),
                        pltpu.VMEM((TOP_K * TM_SH, D_MODEL), F32), pltpu.SemaphoreType.DMA(())],
        compiler_params=_cparams("arbitrary"),
        name="shared_combine",
    )(dest.reshape(n_tiles, 1, TM_SH * TOP_K), gates, h2, sh_w_gu, sh_w_dn, x1, mod, ybuf)


def _final_norm_body(x_ref, g_ref, o_ref):
    x = x_ref[...]
    o_ref[...] = x * lax.rsqrt(jnp.mean(x * x, axis=-1, keepdims=True) + EPS) * g_ref[...]


def final_rmsnorm(x, g):
    return pl.pallas_call(
        _final_norm_body,
        grid=(T_ALL // TM,),
        in_specs=[pl.BlockSpec((TM, D_MODEL), lambda i: (i, 0)), pl.BlockSpec((1, D_MODEL), lambda i: (0, 0))],
        out_specs=pl.BlockSpec((TM, D_MODEL), lambda i: (i, 0)),
        out_shape=jax.ShapeDtypeStruct(x.shape, x.dtype),
        compiler_params=_cparams("arbitrary"),
        name="final_norm",
    )(x, g.reshape(1, D_MODEL))


TQ_DIFF = 256
HALF = LANES // 2
NT_DIMS = (((1,), (1,)), ((), ()))


def rope_tables(L):
    pos = jnp.arange(L)
    d = jnp.arange(LANES) % HEAD_DIM
    p = jnp.where(d // (2 * ROPE_PAIRS) == 0, (pos // GRID_W)[:, None], (pos % GRID_W)[:, None]).astype(F32)
    inv = ROPE_BASE ** (-(d % ROPE_PAIRS).astype(F32) / ROPE_PAIRS)
    ang = p * inv
    return jnp.cos(ang), jnp.where((d // ROPE_PAIRS) % 2 == 0, -jnp.sin(ang), jnp.sin(ang))


def _rope(x, cos, sin_signed):
    lane = lax.broadcasted_iota(jnp.int32, x.shape, 1)
    is_x1 = (lane // ROPE_PAIRS) % 2 == 0
    partner = jnp.where(is_x1, pltpu.roll(x, LANES - ROPE_PAIRS, 1), pltpu.roll(x, ROPE_PAIRS, 1))
    return x * cos + partner * sin_signed


def _exp_and_inv_sum(s):
    e = jnp.exp(s - jnp.max(s, axis=-1, keepdims=True))
    return e, 1.0 / jnp.sum(e, axis=-1, keepdims=True)


def _diff_body(*refs, lk, ctx, lam_init):
    *ins, _aliased_dst, o_ref, kb, vb = refs
    if ctx:
        q_ref, k_ref, v_ref, dl_ref, g_ref, ck_ref, cv_ref, cq_ref, sq_ref, ckk_ref, skk_ref = ins
    else:
        q_ref, k_ref, v_ref, dl_ref, g_ref = ins

    @pl.when(pl.program_id(2) == 0)
    def _():
        k = k_ref[...]
        if ctx:
            k = _rope(k, ckk_ref[...], skk_ref[...])
            kb[lk:, :] = ck_ref[...].astype(BF16)
            vb[lk:, :] = cv_ref[...].astype(BF16)
        kb[0:lk, :] = k.astype(BF16)
        vb[0:lk, :] = v_ref[...].astype(BF16)

    q = q_ref[...] * DIFF_SCALE
    if ctx:
        q = _rope(q, cq_ref[...], sq_ref[...])
    lane = lax.broadcasted_iota(jnp.int32, q.shape, 1)
    q1 = jnp.where(lane < HALF, q, 0.0).astype(BF16)
    q2 = jnp.where(lane >= HALF, q, 0.0).astype(BF16)
    keys = kb[...]
    e1, inv1 = _exp_and_inv_sum(lax.dot_general(q1, keys, NT_DIMS, preferred_element_type=F32))
    e2, inv2 = _exp_and_inv_sum(lax.dot_general(q2, keys, NT_DIMS, preferred_element_type=F32))
    dl = dl_ref[...]
    lam = (jnp.exp(jnp.sum(dl[0:1] * dl[1:2], axis=-1, keepdims=True))
           - jnp.exp(jnp.sum(dl[2:3] * dl[3:4], axis=-1, keepdims=True)) + lam_init)
    o = jnp.dot((e1 * inv1 - e2 * (lam * inv2)).astype(BF16), vb[...], preferred_element_type=F32)
    o = o * lax.rsqrt(jnp.mean(o * o, axis=-1, keepdims=True) + EPS) * g_ref[...]
    o_ref[...] = o * (1.0 - lam_init)


def diff_attention(proj, dst, diff_lam, diff_subln, l, lam_init, cache_k=None, cache_v=None, tables=None):
    ctx = cache_k is not None
    nh = DIFF_HEADS
    if ctx:
        nb, lk, tq, row0 = DEC_BATCH, DEC_SEQ, TQ_DIFF, T_PROMPT
    else:
        nb, lk, tq, row0 = BATCH, SEQ, SEQ, 0
    in_specs = [pl.BlockSpec((tq, LANES), lambda b, h, qi: ((row0 + b * lk) // tq + qi, h)),
                pl.BlockSpec((lk, LANES), lambda b, h, qi: (row0 // lk + b, nh + h)),
                pl.BlockSpec((lk, LANES), lambda b, h, qi: (row0 // lk + b, 2 * nh + h)),
                pl.BlockSpec((None, 4, DIFF_QK), lambda b, h, qi: (l, 0, 0)),
                pl.BlockSpec((None, 1, DIFF_VD), lambda b, h, qi: (l, 0, 0))]
    args = [proj, proj, proj, diff_lam, diff_subln.reshape(DEPTH, 1, DIFF_VD)]
    if ctx:
        cos, sin = tables
        ctx_spec = pl.BlockSpec((None, None, PAST_LEN, LANES), lambda b, h, qi: (b, l, 0, h))
        q_tab = pl.BlockSpec((tq, LANES), lambda b, h, qi: (qi, 0))
        k_tab = pl.BlockSpec((lk, LANES), lambda b, h, qi: (0, 0))
        in_specs += [ctx_spec, ctx_spec, q_tab, q_tab, k_tab, k_tab]
        args += [cache_k.reshape(DEC_BATCH, DEPTH, PAST_LEN, GROUP_W), cache_v.reshape(DEC_BATCH, DEPTH, PAST_LEN, GROUP_W),
                 cos, sin, cos, sin]
    n_keys = lk + (PAST_LEN if ctx else 0)
    return pl.pallas_call(
        partial(_diff_body, lk=lk, ctx=ctx, lam_init=lam_init),
        grid=(nb, nh, lk // tq),
        in_specs=in_specs + [pl.BlockSpec(memory_space=pl.ANY)],
        out_specs=pl.BlockSpec((tq, LANES), lambda b, h, qi: ((row0 + b * lk) // tq + qi, h)),
        out_shape=jax.ShapeDtypeStruct((T_ALL, GROUP_W), F32),
        input_output_aliases={len(args): 0},
        scratch_shapes=[pltpu.VMEM((n_keys, LANES), BF16), pltpu.VMEM((n_keys, LANES), BF16)],
        compiler_params=_cparams("arbitrary", "arbitrary", "arbitrary"),
        name="diff_attention_ctx" if ctx else "diff_attention",
    )(*args, dst)


def _swa_body(*refs, lk, tq, banded):
    *ins, _aliased_dst, o_ref, kb, vb = refs
    if banded:
        q_ref, k_ref, v_ref, sk_ref, ck_ref, cv_ref, cq_ref, sq_ref, ckk_ref, skk_ref = ins
    else:
        q_ref, k_ref, v_ref, sk_ref = ins
    qi = pl.program_id(1)

    @pl.when(qi == 0)
    def _():
        k = k_ref[...]
        if banded:
            k = _rope(k, ckk_ref[...], skk_ref[...])
            zeros = jnp.zeros((WINDOW, LANES), BF16)
            kb[0:WINDOW, :] = zeros
            vb[0:WINDOW, :] = zeros
            kb[WINDOW + lk:2 * WINDOW + lk, :] = zeros
            vb[WINDOW + lk:2 * WINDOW + lk, :] = zeros
            kb[2 * WINDOW + lk:, :] = ck_ref[...].astype(BF16)
            vb[2 * WINDOW + lk:, :] = cv_ref[...].astype(BF16)
            kb[WINDOW:WINDOW + lk, :] = k.astype(BF16)
            vb[WINDOW:WINDOW + lk, :] = v_ref[...].astype(BF16)
        else:
            kb[...] = k.astype(BF16)
            vb[...] = v_ref[...].astype(BF16)

    lane = lax.broadcasted_iota(jnp.int32, (tq, LANES), 1)
    sinks = sk_ref[...]
    if banded:
        start = pl.multiple_of(qi * WINDOW, WINDOW)
        k_loc, v_loc = kb[pl.ds(start, 3 * WINDOW), :], vb[pl.ds(start, 3 * WINDOW), :]
        k_ctx, v_ctx = kb[2 * WINDOW + lk:, :], vb[2 * WINDOW + lk:, :]
        row = lax.broadcasted_iota(jnp.int32, (SWA_GROUP * tq, 3 * WINDOW), 0) % tq
        col = lax.broadcasted_iota(jnp.int32, (SWA_GROUP * tq, 3 * WINDOW), 1)
        kpos = (qi - 1) * WINDOW + col
        visible = (col >= row) & (col - row <= 2 * WINDOW) & (kpos >= 0) & (kpos < lk)
    else:
        k_loc, v_loc = kb[...], vb[...]
    outs = []
    for kh in range(SWA_KV_HEADS):
        in_half = (lane >= HALF) if kh else (lane < HALF)
        qs, sink_rows = [], []
        for g in range(SWA_GROUP):
            head = kh * SWA_GROUP + g
            t = q_ref[:, (head // 2) * LANES:(head // 2 + 1) * LANES] * SWA_SCALE
            if banded:
                t = _rope(t, cq_ref[...], sq_ref[...])
            if head % 2 != kh:
                t = pltpu.roll(t, HALF, 1)
            qs.append(jnp.where(in_half, t, 0.0).astype(BF16))
            sink_rows.append(jnp.broadcast_to(sinks[:, head:head + 1], (tq, 1)))
        qs = jnp.concatenate(qs, axis=0)
        sink = jnp.concatenate(sink_rows, axis=0)
        s_loc = lax.dot_general(qs, k_loc, NT_DIMS, preferred_element_type=F32)
        m = sink
        if banded:
            s_loc = jnp.where(visible, s_loc, NEG_INF)
            s_ctx = lax.dot_general(qs, k_ctx, NT_DIMS, preferred_element_type=F32)
            m = jnp.maximum(m, jnp.max(s_ctx, axis=-1, keepdims=True))
        m = jnp.maximum(m, jnp.max(s_loc, axis=-1, keepdims=True))
        e_loc = jnp.exp(s_loc - m)
        den = jnp.sum(e_loc, axis=-1, keepdims=True) + jnp.exp(sink - m)
        if banded:
            e_ctx = jnp.exp(s_ctx - m)
            den = den + jnp.sum(e_ctx, axis=-1, keepdims=True)
        inv = 1.0 / den
        o = jnp.dot((e_loc * inv).astype(BF16), v_loc, preferred_element_type=F32)
        if banded:
            o = o + jnp.dot((e_ctx * inv).astype(BF16), v_ctx, preferred_element_type=F32)
        outs.append(o)
    for t in range(SWA_HEADS // 2):
        kh, g0 = (2 * t) // SWA_GROUP, (2 * t) % SWA_GROUP
        a = outs[kh][g0 * tq:(g0 + 1) * tq]
        b = outs[kh][(g0 + 1) * tq:(g0 + 2) * tq]
        a = pltpu.roll(a, HALF, 1) if kh == 1 else a
        b = pltpu.roll(b, HALF, 1) if kh == 0 else b
        o_ref[:, t * LANES:(t + 1) * LANES] = jnp.where(lane < HALF, a, b)


def swa_attention(proj, dst, swa_sink, l, cache_k=None, cache_v=None, tables=None):
    banded = cache_k is not None
    q_col, k_col = 3, (4 * GROUP_W) // LANES
    if banded:
        nb, lk, tq, row0 = DEC_BATCH, DEC_SEQ, WINDOW, T_PROMPT
    else:
        nb, lk, tq, row0 = BATCH, SEQ, SEQ, 0
    in_specs = [pl.BlockSpec((tq, GROUP_W), lambda b, qi: ((row0 + b * lk) // tq + qi, q_col)),
                pl.BlockSpec((lk, LANES), lambda b, qi: (row0 // lk + b, k_col)),
                pl.BlockSpec((lk, LANES), lambda b, qi: (row0 // lk + b, k_col + 1)),
                pl.BlockSpec((None, 1, SWA_HEADS), lambda b, qi: (l, 0, 0))]
    args = [proj, proj, proj, swa_sink.reshape(DEPTH, 1, SWA_HEADS)]
    if banded:
        cos, sin = tables
        ctx_spec = pl.BlockSpec((None, None, PAST_LEN, LANES), lambda b, qi: (b, l, 0, 0))
        q_tab = pl.BlockSpec((tq, LANES), lambda b, qi: (qi, 0))
        k_tab = pl.BlockSpec((lk, LANES), lambda b, qi: (0, 0))
        in_specs += [ctx_spec, ctx_spec, q_tab, q_tab, k_tab, k_tab]
        args += [cache_k.reshape(DEC_BATCH, DEPTH, PAST_LEN, LANES), cache_v.reshape(DEC_BATCH, DEPTH, PAST_LEN, LANES),
                 cos, sin, cos, sin]
    n_keys = lk + 2 * WINDOW + PAST_LEN if banded else lk
    return pl.pallas_call(
        partial(_swa_body, lk=lk, tq=tq, banded=banded),
        grid=(nb, lk // tq),
        in_specs=in_specs + [pl.BlockSpec(memory_space=pl.ANY)],
        out_specs=pl.BlockSpec((tq, GROUP_W), lambda b, qi: ((row0 + b * lk) // tq + qi, 0)),
        out_shape=jax.ShapeDtypeStruct((T_ALL, GROUP_W), F32),
        input_output_aliases={len(args): 0},
        scratch_shapes=[pltpu.VMEM((n_keys, LANES), BF16), pltpu.VMEM((n_keys, LANES), BF16)],
        compiler_params=_cparams("arbitrary", "arbitrary"),
        name="swa_attention_banded" if banded else "swa_attention",
    )(*args, dst)


LRU_GROUPS = LRU_W // LANES


def lru_gate_params(wa, ba, wx, bx):
    def tile_blocks(w):
        w = w.reshape(LRU_GROUPS, 2, LRU_BD, LRU_BD)
        z = jnp.zeros((LRU_GROUPS, LRU_BD, LRU_BD), F32)
        return jnp.concatenate([jnp.concatenate([w[:, 0], z], axis=-1), jnp.concatenate([z, w[:, 1]], axis=-1)], axis=-2)

    w = jnp.concatenate([tile_blocks(wa[0]), tile_blocks(wx[0]), tile_blocks(wa[1]), tile_blocks(wx[1])], axis=-1)
    b = jnp.stack([ba[0], bx[0], ba[1], bx[1]], axis=0).reshape(4, LRU_GROUPS, LANES)
    return w, jnp.moveaxis(b, 0, 1).reshape(LRU_GROUPS, 1, 4 * LANES)


def _shift_rows(x, s, row, n):
    if s == 0:
        return x
    ok = (row >= -s) if s < 0 else (row < n - s)
    return jnp.where(ok, pltpu.roll(x, (-s) % n, 0), 0.0)


def _linear_scan(a, b, row, n, reverse):
    s = 1
    while s < n:
        ok = (row < n - s) if reverse else (row >= s)
        shift = (n - s) if reverse else s
        b = jnp.where(ok, a * pltpu.roll(b, shift, 0) + b, b)
        a = jnp.where(ok, a * pltpu.roll(a, shift, 0), a)
        s *= 2
    return b


def _lru_body(x_ref, lg_ref, cw_ref, w_ref, bias_ref, lam_ref, h0_ref, _aliased_dst, o_ref, st_ref, *, n):
    x = x_ref[...]
    row = lax.broadcasted_iota(jnp.int32, x.shape, 0)
    cw = cw_ref[...]
    xc = _shift_rows(x, -(LRU_CONV // 2), row, n) * cw[0:1]
    for k in range(1, LRU_CONV):
        xc = xc + _shift_rows(x, k - LRU_CONV // 2, row, n) * cw[k:k + 1]
    gates = jnp.dot(xc.astype(BF16), w_ref[...].astype(BF16), preferred_element_type=F32) + bias_ref[...]
    lam = lam_ref[...]
    log_sig = jnp.minimum(lam, 0.0) - jnp.log1p(jnp.exp(-jnp.abs(lam)))
    h0 = h0_ref[...]
    hs = []
    for d in range(2):
        r = jax.nn.sigmoid(gates[:, (2 * d) * LANES:(2 * d + 1) * LANES])
        i = jax.nn.sigmoid(gates[:, (2 * d + 1) * LANES:(2 * d + 2) * LANES])
        log_a = LRU_C * r * log_sig[d:d + 1]
        a = jnp.exp(log_a)
        th = jnp.tanh(log_a)
        b = jnp.sqrt(-2.0 * th / (1.0 - th)) * (i * xc)
        first = n - 1 if d else 0
        b = jnp.where(row == first, b + a * h0[d:d + 1], b)
        hs.append(_linear_scan(a, b, row, n, reverse=bool(d)))
    lg = lg_ref[...]
    gelu = 0.5 * lg * (1.0 + jnp.tanh(math.sqrt(2.0 / math.pi) * (lg + 0.044715 * (lg * lg * lg))))
    o_ref[...] = (hs[0] + hs[1]) * gelu
    st_ref[0:1, :] = hs[0][n - 1:n, :]
    st_ref[1:2, :] = hs[1][0:1, :]


def rglru(proj, dst, gate_w, gate_b, lru_conv, lru_lam, h0, l, h0_l, prompt):
    nb, n, row0 = (BATCH, SEQ, 0) if prompt else (DEC_BATCH, DEC_SEQ, T_PROMPT)
    x_col = (D_IN - 2 * LRU_W) // LANES
    return pl.pallas_call(
        partial(_lru_body, n=n),
        grid=(nb, LRU_GROUPS),
        in_specs=[pl.BlockSpec((n, LANES), lambda s, c: (row0 // n + s, x_col + c)),
                  pl.BlockSpec((n, LANES), lambda s, c: (row0 // n + s, x_col + LRU_GROUPS + c)),
                  pl.BlockSpec((None, LRU_CONV, LANES), lambda s, c: (l, 0, c)),
                  pl.BlockSpec((None, LANES, 4 * LANES), lambda s, c: (c, 0, 0)),
                  pl.BlockSpec((None, 1, 4 * LANES), lambda s, c: (c, 0, 0)),
                  pl.BlockSpec((None, 2, LANES), lambda s, c: (l, 0, c)),
                  pl.BlockSpec((None, None, 2, LANES), lambda s, c: (s, h0_l, 0, c)),
                  pl.BlockSpec(memory_space=pl.ANY)],
        out_specs=[pl.BlockSpec((n, LANES), lambda s, c: (row0 // n + s, c)),
                   pl.BlockSpec((None, 2, LANES), lambda s, c: (s, 0, c))],
        out_shape=[jax.ShapeDtypeStruct((T_ALL, LRU_W), F32), jax.ShapeDtypeStruct((nb, 2, LRU_W), F32)],
        input_output_aliases={7: 0},
        compiler_params=_cparams("arbitrary", "arbitrary"),
        name="rglru_prompt" if prompt else "rglru_sample",
    )(proj, proj, lru_conv, gate_w, gate_b, lru_lam, h0, dst)


HY_P = 256
HY_F = 2 * HY_P
HY_ACC_VREGS = 8


def _split_bf16(x):
    hi = x.astype(BF16)
    return hi, (x - hi.astype(F32)).astype(BF16)


def _dot3(c_hi, c_lo, d):
    d_hi, d_lo = _split_bf16(d)
    return (jnp.dot(c_hi, d_hi, preferred_element_type=F32) + jnp.dot(c_hi, d_lo, preferred_element_type=F32)
            + jnp.dot(c_lo, d_hi, preferred_element_type=F32))


def dft_constants():
    f = np.arange(HY_F)[:, None].astype(np.float64)
    ang = 2.0 * np.pi * f * np.arange(HY_F)[None, :] / HY_F
    c, s = np.cos(ang), np.sin(ang)
    ch, sh = c[:, :HY_P], s[:, :HY_P]
    fwd_data = np.block([[ch, sh], [-sh, ch]])
    fwd_filter = np.concatenate([c, -s], axis=0)
    inverse = np.block([[ch.T, -sh.T], [sh.T, ch.T]]) / HY_F
    out = []
    for m in (fwd_data, fwd_filter, inverse):
        hi = m.astype(np.float32).astype(BF16)
        lo = (m - hi.astype(np.float64)).astype(np.float32).astype(BF16)
        out += [jnp.asarray(hi), jnp.asarray(lo)]
    return out


def _filter_spectra_body(cur_ref, prev_ref, fh_ref, fl_ref, g_ref):
    row = lax.broadcasted_iota(jnp.int32, prev_ref.shape, 0)
    prev = jnp.where((row == 0) | (pl.program_id(1) == 0), 0.0, prev_ref[...])
    g_ref[...] = _dot3(fh_ref[...], fl_ref[...], jnp.concatenate([cur_ref[...], prev], axis=0))


def filter_spectra(filt, fk_hi, fk_lo):
    nq, n, _ = filt.shape
    nblk = n // HY_P
    const = pl.BlockSpec((2 * HY_F, HY_F), lambda q, m: (0, 0))
    return pl.pallas_call(
        _filter_spectra_body,
        grid=(nq, nblk),
        in_specs=[pl.BlockSpec((None, HY_P, HY_W), lambda q, m: (q, m, 0)),
                  pl.BlockSpec((None, HY_P, HY_W), lambda q, m: (q, jnp.maximum(m - 1, 0), 0)),
                  const, const],
        out_specs=pl.BlockSpec((None, None, 2 * HY_F, HY_W), lambda q, m: (q, m, 0, 0)),
        out_shape=jax.ShapeDtypeStruct((nq, nblk, 2 * HY_F, HY_W), F32),
        compiler_params=_cparams("arbitrary", "arbitrary"),
        name="hyena_filter_spectra",
    )(filt, filt, fk_hi, fk_lo)


def _hyena_body(z0_ref, z1_ref, x0_ref, x1_ref, cwz_ref, cwx_ref, skip_ref, gf_ref, gb_ref, fdh_ref, fdl_ref,
                gih_ref, gil_ref, o_ref, zs, xs, us, ys, *, n, conv_z):
    nblk = n // HY_P
    row = lax.broadcasted_iota(jnp.int32, z0_ref.shape, 0)

    def short_conv(x, cw):
        acc = _shift_rows(x, -(HY_SHORT // 2), row, n) * cw[0:1]
        for k in range(1, HY_SHORT):
            acc = acc + _shift_rows(x, k - HY_SHORT // 2, row, n) * cw[k:k + 1]
        return acc

    for b, (z_ref, x_ref) in enumerate(((z0_ref, x0_ref), (z1_ref, x1_ref))):
        zs[b] = short_conv(z_ref[...], cwz_ref[...]) if conv_z else z_ref[...]
        xs[b] = short_conv(x_ref[...], cwx_ref[...])

    def forward(j, carry):
        r0 = pl.multiple_of(j * HY_P, HY_P)
        d = jnp.concatenate([zs[0, pl.ds(r0, HY_P), :], zs[1, pl.ds(r0, HY_P), :]], axis=0)
        us[j] = _dot3(fdh_ref[...], fdl_ref[...], d)
        return carry

    lax.fori_loop(0, nblk, forward, 0)

    lanes = z0_ref.shape[1]
    fc = HY_ACC_VREGS * 8 * LANES // lanes

    def out_block(i, carry):
        for c0 in range(0, HY_F, fc):
            re, im = pl.ds(c0, fc), pl.ds(HY_F + c0, fc)

            def causal(j, acc):
                g_re, g_im, u_re, u_im = gf_ref[i - j, re, :], gf_ref[i - j, im, :], us[j, re, :], us[j, im, :]
                return acc[0] + (g_re * u_re - g_im * u_im), acc[1] + (g_re * u_im + g_im * u_re)

            def anticausal(j, acc):
                g_re, g_im, u_re, u_im = gb_ref[j - i, re, :], gb_ref[j - i, im, :], us[j, re, :], us[j, im, :]
                return acc[0] + (g_re * u_re + g_im * u_im), acc[1] + (g_re * u_im - g_im * u_re)

            zero = jnp.zeros((fc, lanes), F32)
            acc = lax.fori_loop(0, i + 1, causal, (zero, zero))
            acc = lax.fori_loop(i, nblk, anticausal, acc)
            ys[re, :] = acc[0]
            ys[im, :] = acc[1]
        y = _dot3(gih_ref[...], gil_ref[...], ys[...])
        r0 = pl.multiple_of(i * HY_P, HY_P)
        for b in range(2):
            conv = y[b * HY_P:(b + 1) * HY_P]
            o_ref[pl.ds(b * n + r0, HY_P), :] = xs[b, pl.ds(r0, HY_P), :] * (conv + skip_ref[...] * zs[b, pl.ds(r0, HY_P), :])
        return carry

    lax.fori_loop(0, nblk, out_block, 0)


def hyena_order(zsrc, z_col0, proj, x_col0, hy_conv, hy_skip, spectra, consts, l, order, prompt):
    fd_hi, fd_lo, _, _, gi_hi, gi_lo = consts
    if prompt:
        n, cw, npair, z_row0 = SEQ, 2 * LANES, BATCH // 2, 0
    else:
        n, cw, npair, z_row0 = DEC_SEQ, LANES, DEC_BATCH // 2, (T_PROMPT // DEC_SEQ if order == 0 else 0)
    x_row0 = 0 if prompt else T_PROMPT // DEC_SEQ
    ncg = HY_W // cw
    nblk = n // HY_P
    hy_col = sum(IN_SIZES[:6])
    once = pl.Buffered(1)
    seq = lambda row0, col0, b: pl.BlockSpec((n, cw), lambda p, c: (row0 + 2 * p + b, col0 // cw + c), pipeline_mode=once)
    taps = lambda col0: pl.BlockSpec((None, HY_SHORT, cw), lambda p, c: (l, 0, col0 // cw + c))
    spec_blk = lambda q: pl.BlockSpec((None, nblk, 2 * HY_F, cw), lambda p, c: (q, 0, 0, c), pipeline_mode=once)
    const = lambda shape: pl.BlockSpec(shape, lambda p, c: (0, 0))
    return pl.pallas_call(
        partial(_hyena_body, n=n, conv_z=(order == 0)),
        grid=(npair, ncg),
        in_specs=[seq(z_row0, z_col0, 0), seq(z_row0, z_col0, 1), seq(x_row0, x_col0, 0), seq(x_row0, x_col0, 1),
                  taps(0), taps(x_col0 - hy_col),
                  pl.BlockSpec((None, 1, cw), lambda p, c: (l * HY_ORDER + order, 0, c)),
                  spec_blk(2 * order), spec_blk(2 * order + 1),
                  const((2 * HY_F, HY_F)), const((2 * HY_F, HY_F)), const((HY_F, 2 * HY_F)), const((HY_F, 2 * HY_F))],
        out_specs=pl.BlockSpec((2 * n, cw), lambda p, c: (p, c)),
        out_shape=jax.ShapeDtypeStruct((npair * 2 * n, HY_W), F32),
        scratch_shapes=[pltpu.VMEM((2, n, cw), F32), pltpu.VMEM((2, n, cw), F32), pltpu.VMEM((nblk, 2 * HY_F, cw), F32),
                        pltpu.VMEM((2 * HY_F, cw), F32)],
        compiler_params=_cparams("arbitrary", "arbitrary"),
        name=f"hyena_{'prompt' if prompt else 'sample'}_order{order}",
    )(zsrc, zsrc, proj, proj, hy_conv, hy_conv, hy_skip.reshape(DEPTH * HY_ORDER, 1, HY_W), spectra, spectra,
      fd_hi, fd_lo, gi_hi, gi_lo)


def hyena_group(proj, filt, hy_conv, hy_skip, consts, l, prompt):
    fwd = filt[:, :, 0].at[0].add(filt[0, :, 1])
    bwd = filt[:, :, 1].at[0].set(0.0)
    filt4 = jnp.moveaxis(jnp.stack([fwd, bwd], axis=2), 0, 2).reshape(2 * HY_ORDER, filt.shape[0], HY_W)
    spectra = filter_spectra(filt4, consts[2], consts[3])
    hy_col = sum(IN_SIZES[:6])
    z = hyena_order(proj, hy_col, proj, hy_col + HY_W, hy_conv, hy_skip, spectra, consts, l, 0, prompt)
    return hyena_order(z, 0, proj, hy_col + 2 * HY_W, hy_conv, hy_skip, spectra, consts, l, 1, prompt)


def moe_layer(x1, mod, p, l):
    h2, idx, gates, rank, counts = router(x1, p['norm2'], mod, p['router_w'], p['router_bias'], l)
    idx, rank = idx[:, :TOP_K], rank[:, :TOP_K]
    nblk = (counts[0] + TM_E - 1) // TM_E
    blk_end = jnp.cumsum(nblk)
    row_start = (blk_end - nblk) * TM_E
    dest = row_start[idx] + rank
    n_valid = blk_end[-1:].astype(jnp.int32)
    blk = jnp.minimum(jnp.arange(N_BLK_MAX, dtype=jnp.int32), n_valid[0] - 1)
    blk_e = jnp.minimum(jnp.sum(blk_end[None, :] <= blk[:, None], axis=-1), N_EXPERTS - 1).astype(jnp.int32)
    tok = jnp.broadcast_to(jnp.arange(T_ALL, dtype=jnp.int32)[:, None], dest.shape)
    src = jnp.zeros((N_BLK_MAX * TM_E,), jnp.int32).at[dest.reshape(-1)].set(tok.reshape(-1))
    ybuf = routed_experts(h2, src, blk_e, n_valid, p['moe_w_gu'], p['moe_w_dn'], l)
    return shared_combine(h2, p['sh_w_gu'], p['sh_w_dn'], ybuf, dest, gates, x1, mod, l)


def mixer_heads(proj, p, lam_init, ctx):
    B, L, _ = proj.shape
    dq, dk, dv, sq, sk, sv, hy, lx, lg = jnp.split(proj, np.cumsum(IN_SIZES)[:-1].tolist(), axis=-1)
    dq = dq.reshape(B, L, DIFF_HEADS, 2, DIFF_QK) * DIFF_SCALE
    dk = dk.reshape(B, L, DIFF_HEADS, 2, DIFF_QK)
    dv = dv.reshape(B, L, DIFF_HEADS, DIFF_VD)
    sq = sq.reshape(B, L, SWA_KV_HEADS, SWA_GROUP, HEAD_DIM) * SWA_SCALE
    sk = sk.reshape(B, L, SWA_KV_HEADS, HEAD_DIM)
    sv = sv.reshape(B, L, SWA_KV_HEADS, HEAD_DIM)
    lq1, lk1, lq2, lk2 = p['diff_lam'].astype(F32)
    lam = jnp.exp(jnp.sum(lq1 * lk1)) - jnp.exp(jnp.sum(lq2 * lk2)) + lam_init
    sink = p['swa_sink'].reshape(SWA_KV_HEADS, SWA_GROUP)
    if ctx is None:
        d_out = diff_attend(dq, dk, dv, lam)
        s_out = swa_dense(sq, sk, sv, sink)
        h0_f = jnp.zeros((B, LRU_W), F32)
        h0_b = jnp.zeros((B, LRU_W), F32)
    else:
        ck_d, cv_d, ck_s, cv_s, st = ctx
        cos, sin = axial_rope(L)
        dq_r = apply_rope(dq.reshape(B, L, 2 * DIFF_HEADS, DIFF_QK), cos, sin).reshape(dq.shape)
        dk_r = apply_rope(dk.reshape(B, L, 2 * DIFF_HEADS, DIFF_QK), cos, sin).reshape(dk.shape)
        d_out = diff_attend(dq_r, jnp.concatenate([dk_r, ck_d], axis=1), jnp.concatenate([dv, cv_d], axis=1), lam)
        sq_r = apply_rope(sq.reshape(B, L, SWA_HEADS, HEAD_DIM), cos, sin).reshape(sq.shape)
        s_out = swa_banded(sq_r, apply_rope(sk, cos, sin), sv, ck_s, cv_s, sink)
        h0_f, h0_b = st[:, 0], st[:, 1]
    d_out = (rmsnorm(d_out, p['diff_subln']) * (1.0 - lam_init)).reshape(B, L, DIFF_HEADS * DIFF_VD)
    hy_out = hyena_mix(hy, p)
    lru_in = dwconv(lx, p['lru_conv'], LRU_CONV // 2)
    hl, hf_last, hb_first = rglru_bidir(lru_in, p['lru_wa'], p['lru_ba'], p['lru_wx'], p['lru_bx'], p['lru_lam'], h0_f, h0_b)
    lru_out = hl.astype(proj.dtype) * jax.nn.gelu(lg)
    mix = jnp.concatenate([d_out, s_out, hy_out, lru_out], axis=-1).reshape(B * L, D_MIX)
    if ctx is None:
        return mix, (dk, dv, sk, sv, jnp.stack([hf_last, hb_first], axis=1).astype(proj.dtype))
    return mix, None


def kernel(x_prompt, x_sample, cache_diff_k, cache_diff_v, cache_swa_k, cache_swa_v, state_lru, c, c_ctx,
           w_mod, b_mod, norm1, norm2, w_in, w_out, diff_lam, diff_subln, swa_sink, hy_conv, hy_w1, hy_b1,
           hy_w2, hy_b2, hy_w3, hy_freq, hy_log_rate, hy_skip, lru_conv, lru_wa, lru_ba, lru_wx, lru_bx, lru_lam,
           router_w, router_bias, moe_w_gu, moe_w_dn, sh_w_gu, sh_w_dn, final_norm):
    x = jnp.concatenate([x_prompt.reshape(T_PROMPT, D_MODEL), x_sample.reshape(T_SAMPLE, D_MODEL)], axis=0)
    cond = jnp.concatenate([c_ctx[None, :], c, jnp.zeros((8 - N_COND, D_MODEL), F32)], axis=0)
    dk_l, dv_l, sk_l, sv_l, st_l = [], [], [], [], []
    tables = rope_tables(DEC_SEQ)
    consts = dft_constants()
    h0_prompt = jnp.zeros((BATCH, 1, 2, LRU_W), F32)
    col = np.cumsum((0,) + IN_SIZES).tolist()
    empty = lambda: jnp.zeros((T_ALL, GROUP_W), F32)
    for l in range(DEPTH):
        p = {'hy_conv': hy_conv[l], 'hy_w1': hy_w1[l], 'hy_b1': hy_b1[l], 'hy_w2': hy_w2[l], 'hy_b2': hy_b2[l],
             'hy_w3': hy_w3[l], 'hy_freq': hy_freq[l], 'hy_log_rate': hy_log_rate[l], 'hy_skip': hy_skip[l],
             'norm2': norm2, 'router_w': router_w, 'router_bias': router_bias, 'moe_w_gu': moe_w_gu,
             'moe_w_dn': moe_w_dn, 'sh_w_gu': sh_w_gu, 'sh_w_dn': sh_w_dn}
        lam_init = 0.8 - 0.6 * math.exp(-0.3 * l)
        mod = modulation(cond, w_mod, b_mod, l)[:N_COND].reshape(N_COND * 6, 1, D_MODEL)
        proj = in_proj(x, norm1, mod, w_in, l)
        proj_p = proj[:T_PROMPT].reshape(BATCH, SEQ, D_IN)
        dk_l.append(proj_p[..., col[1]:col[2]].reshape(BATCH, SEQ, DIFF_HEADS, 2, DIFF_QK))
        dv_l.append(proj_p[..., col[2]:col[3]].reshape(BATCH, SEQ, DIFF_HEADS, DIFF_VD))
        sk_l.append(proj_p[..., col[4]:col[5]].reshape(BATCH, SEQ, SWA_KV_HEADS, HEAD_DIM))
        sv_l.append(proj_p[..., col[5]:col[6]].reshape(BATCH, SEQ, SWA_KV_HEADS, HEAD_DIM))
        d_out = diff_attention(proj, empty(), diff_lam, diff_subln, l, lam_init)
        d_out = diff_attention(proj, d_out, diff_lam, diff_subln, l, lam_init, cache_diff_k, cache_diff_v, tables)
        s_out = swa_attention(proj, empty(), swa_sink, l)
        s_out = swa_attention(proj, s_out, swa_sink, l, cache_swa_k, cache_swa_v, tables)
        hy_args = (p['hy_w1'], p['hy_b1'], p['hy_w2'], p['hy_b2'], p['hy_w3'], p['hy_freq'], p['hy_log_rate'])
        hy_out = jnp.concatenate(
            [hyena_group(proj, hyena_filters(SEQ, *hy_args), hy_conv, hy_skip, consts, l, True),
             hyena_group(proj, hyena_filters(DEC_SEQ, *hy_args), hy_conv, hy_skip, consts, l, False)], axis=0)
        gate_w, gate_b = lru_gate_params(lru_wa[l], lru_ba[l], lru_wx[l], lru_bx[l])
        lru_out, st = rglru(proj, empty(), gate_w, gate_b, lru_conv, lru_lam, h0_prompt, l, 0, True)
        lru_out, _ = rglru(proj, lru_out, gate_w, gate_b, lru_conv, lru_lam, state_lru, l, l, False)
        st_l.append(st)
        x1 = out_proj([d_out, s_out, hy_out, lru_out], w_out, x, mod, l)
        x = moe_layer(x1, mod, p, l)
    y = final_rmsnorm(x, final_norm)
    y_prompt = y[:T_PROMPT].reshape(BATCH, SEQ, D_MODEL)
    y_sample = y[T_PROMPT:].reshape(DEC_BATCH, DEC_SEQ, D_MODEL)
    return (y_prompt, y_sample, jnp.stack(dk_l, axis=1), jnp.stack(dv_l, axis=1), jnp.stack(sk_l, axis=1),
            jnp.stack(sv_l, axis=1), jnp.stack(st_l, axis=1))
```

```python
import math
from functools import partial
import jax, jax.numpy as jnp
from jax import lax
import numpy as np
from jax.experimental import pallas as pl
from jax.experimental.pallas import tpu as pltpu

D_MODEL = 2048
BATCH = 16
SEQ = 256
DEPTH = 2
DEC_BATCH = 2
DEC_SEQ = 4096
PAST_LEN = 256

GRID_W = 64
EPS = 1e-6
F32 = jnp.float32
QBLK = 128
GROUP_W = D_MODEL // 4
D_MIX = 4 * GROUP_W
DIFF_QK = 64
DIFF_VD = 2 * DIFF_QK
DIFF_HEADS = GROUP_W // DIFF_VD
DIFF_SCALE = DIFF_QK ** -0.5
HEAD_DIM = 64
SWA_HEADS = GROUP_W // HEAD_DIM
SWA_KV_HEADS = SWA_HEADS // 4
SWA_GROUP = SWA_HEADS // SWA_KV_HEADS
SWA_SCALE = HEAD_DIM ** -0.5
WINDOW = 128
ROPE_PAIRS = HEAD_DIM // 4
ROPE_BASE = 10000.0
HY_W = GROUP_W
HY_ORDER = 2
HY_SHORT = 3
HY_BANDS = 16
HY_POS_DIM = 1 + 2 * HY_BANDS
HY_HID = 64
LRU_W = GROUP_W
LRU_BLOCKS = 8
LRU_BD = LRU_W // LRU_BLOCKS
LRU_CONV = 4
LRU_C = 8.0
N_EXPERTS = 64
TOP_K = 6
EXPERT_FF = D_MODEL // 4
SHARED_FF = EXPERT_FF
ROUTE_SCALE = 2.5
MOE_BLK = 128
NEG_INF = -1e30
IN_SIZES = (2 * DIFF_HEADS * DIFF_QK, 2 * DIFF_HEADS * DIFF_QK, DIFF_HEADS * DIFF_VD, SWA_HEADS * HEAD_DIM, SWA_KV_HEADS * HEAD_DIM, SWA_KV_HEADS * HEAD_DIM, (HY_ORDER + 1) * HY_W, LRU_W, LRU_W)
D_IN = sum(IN_SIZES)


def rmsnorm(x, g):
    xf = x.astype(F32)
    y = xf * lax.rsqrt(jnp.mean(xf * xf, axis=-1, keepdims=True) + EPS)
    return (y * g.astype(F32)).astype(x.dtype)


def dwconv(x, w, left):
    K, L = w.shape[0], x.shape[1]
    xp = jnp.pad(x, ((0, 0), (left, K - 1 - left), (0, 0)))
    acc = xp[:, 0:L] * w[0]
    for k in range(1, K):
        acc = acc + xp[:, k:k + L] * w[k]
    return acc


def axial_rope(L):
    rows = L // GRID_W
    row = jnp.repeat(jnp.arange(rows), GRID_W).astype(F32)
    col = jnp.tile(jnp.arange(GRID_W), rows).astype(F32)
    inv = ROPE_BASE ** (-jnp.arange(ROPE_PAIRS, dtype=F32) / ROPE_PAIRS)
    ang = jnp.stack([row[:, None] * inv, col[:, None] * inv], axis=1)
    return jnp.cos(ang), jnp.sin(ang)


def apply_rope(x, cos, sin):
    xs = x.reshape(*x.shape[:-1], 2, 2, ROPE_PAIRS)
    c = cos[:, None, :, None, :].astype(x.dtype)
    s = sin[:, None, :, None, :].astype(x.dtype)
    x1, x2 = xs[..., 0:1, :], xs[..., 1:2, :]
    return jnp.concatenate([x1 * c - x2 * s, x2 * c + x1 * s], axis=-2).reshape(x.shape)


def diff_attend(q, k, v, lam):
    B, Lq = q.shape[:2]
    nb = Lq // QBLK
    qb = jnp.moveaxis(q.reshape(B, nb, QBLK, *q.shape[2:]), 1, 0)

    def one(qblk):
        s = jnp.einsum('bqhmd,bkhmd->bhmqk', qblk, k, preferred_element_type=F32)
        p = jax.nn.softmax(s, axis=-1)
        w = p[:, :, 0] - lam * p[:, :, 1]
        return jnp.einsum('bhqk,bkhd->bqhd', w.astype(v.dtype), v)

    o = lax.map(one, qb)
    return jnp.moveaxis(o, 0, 1).reshape(B, Lq, q.shape[2], v.shape[-1])


def sink_softmax(s, sink):
    sb = sink.astype(F32)[None, :, :, None, None]
    m = jnp.maximum(jnp.max(s, axis=-1, keepdims=True), sb)
    e = jnp.exp(s - m)
    return e / (jnp.sum(e, axis=-1, keepdims=True) + jnp.exp(sb - m))


def swa_dense(q, k, v, sink):
    B, L = q.shape[:2]
    nb = L // QBLK
    qb = jnp.moveaxis(q.reshape(B, nb, QBLK, *q.shape[2:]), 1, 0)

    def one(qblk):
        p = sink_softmax(jnp.einsum('bqkgd,bnkd->bkgqn', qblk, k, preferred_element_type=F32), sink)
        return jnp.einsum('bkgqn,bnkd->bqkgd', p.astype(v.dtype), v)

    o = lax.map(one, qb)
    return jnp.moveaxis(o, 0, 1).reshape(B, L, SWA_HEADS * HEAD_DIM)


def swa_banded(q, k, v, k_ctx, v_ctx, sink):
    B, L = q.shape[:2]
    nb = L // WINDOW
    pad = lambda t: jnp.pad(t, ((0, 0), (WINDOW, WINDOW), (0, 0), (0, 0))).reshape(B, nb + 2, WINDOW, *t.shape[2:])
    kb, vb = pad(k), pad(v)
    kwin = jnp.concatenate([kb[:, :nb], kb[:, 1:nb + 1], kb[:, 2:]], axis=2)
    vwin = jnp.concatenate([vb[:, :nb], vb[:, 1:nb + 1], vb[:, 2:]], axis=2)
    qi = jnp.arange(WINDOW)[:, None]
    kj = jnp.arange(3 * WINDOW)[None, :]
    rel = kj - qi
    band = (rel >= 0) & (rel <= 2 * WINDOW)
    qb = jnp.moveaxis(q.reshape(B, nb, WINDOW, *q.shape[2:]), 1, 0)

    def one(args):
        b, qblk, kw, vw = args
        kpos = (b - 1) * WINDOW + kj
        mask = band & (kpos >= 0) & (kpos < L)
        s_loc = jnp.where(mask, jnp.einsum('bqkgd,bnkd->bkgqn', qblk, kw, preferred_element_type=F32), NEG_INF)
        s_ctx = jnp.einsum('bqkgd,bnkd->bkgqn', qblk, k_ctx, preferred_element_type=F32)
        p = sink_softmax(jnp.concatenate([s_loc, s_ctx], axis=-1), sink).astype(v.dtype)
        return (jnp.einsum('bkgqn,bnkd->bqkgd', p[..., :3 * WINDOW], vw)
                + jnp.einsum('bkgqn,bnkd->bqkgd', p[..., 3 * WINDOW:], v_ctx))

    o = lax.map(one, (jnp.arange(nb), qb, jnp.moveaxis(kwin, 1, 0), jnp.moveaxis(vwin, 1, 0)))
    return jnp.moveaxis(o, 0, 1).reshape(B, L, SWA_HEADS * HEAD_DIM)


def hyena_filters(L, w1, b1, w2, b2, w3, freq, log_rate):
    tn = jnp.arange(L, dtype=F32) / L
    ang = 2.0 * math.pi * tn[:, None] * jnp.arange(1, HY_BANDS + 1, dtype=F32)
    z = jnp.concatenate([tn[:, None], jnp.sin(ang), jnp.cos(ang)], axis=-1)
    f = jnp.sin(freq[0].astype(F32) * (z @ w1.astype(F32) + b1.astype(F32)))
    f = jnp.sin(freq[1].astype(F32) * (f @ w2.astype(F32) + b2.astype(F32)))
    f = (f @ w3.astype(F32)).reshape(L, HY_ORDER, 2, HY_W)
    f = f * jnp.exp(-tn[:, None, None, None] * jnp.exp(log_rate.astype(F32)))
    return f * lax.rsqrt(jnp.sum(f * f, axis=(0, 2), keepdims=True) + EPS)


def bidir_fftconv(u, kf, kb):
    L, C = u.shape[1], u.shape[2]
    kfull = jnp.concatenate([kf.at[0].add(kb[0]), jnp.zeros((1, C), F32), kb[1:][::-1]], axis=0)
    Kf = jnp.fft.rfft(kfull, axis=0)
    Uf = jnp.fft.rfft(u, n=2 * L, axis=1)
    return jnp.fft.irfft(Uf * Kf[None], n=2 * L, axis=1)[:, :L]


def hyena_mix(u, p):
    L = u.shape[1]
    u = dwconv(u, p['hy_conv'], HY_SHORT // 2)
    parts = jnp.split(u, HY_ORDER + 1, axis=-1)
    filt = hyena_filters(L, p['hy_w1'], p['hy_b1'], p['hy_w2'], p['hy_b2'], p['hy_w3'], p['hy_freq'], p['hy_log_rate'])
    skip = p['hy_skip'].astype(F32)
    z = parts[0].astype(F32)
    for o in range(HY_ORDER):
        z = parts[o + 1].astype(F32) * (bidir_fftconv(z, filt[:, o, 0], filt[:, o, 1]) + skip[o] * z)
    return z.astype(u.dtype)


def _lin_combine(e1, e2):
    a1, b1 = e1
    a2, b2 = e2
    return a1 * a2, a2 * b1 + b2


def rglru_bidir(x, wa, ba, wx, bx, lam, h0_f, h0_b):
    B, L, W = x.shape
    xb = x.reshape(B, L, LRU_BLOCKS, LRU_BD)
    hs = []
    for d, (h0, rev) in enumerate(((h0_f, False), (h0_b, True))):
        r = jax.nn.sigmoid(jnp.einsum('blnd,nde->blne', xb, wa[d]).reshape(B, L, W) + ba[d])
        i = jax.nn.sigmoid(jnp.einsum('blnd,nde->blne', xb, wx[d]).reshape(B, L, W) + bx[d])
        log_a = LRU_C * r.astype(F32) * jax.nn.log_sigmoid(lam[d].astype(F32))
        a = jnp.exp(log_a)
        bterm = jnp.sqrt(-jnp.expm1(2.0 * log_a)) * (i * x).astype(F32)
        first = L - 1 if rev else 0
        bterm = bterm.at[:, first].add(a[:, first] * h0.astype(F32))
        _, h = lax.associative_scan(_lin_combine, (a, bterm), axis=1, reverse=rev)
        hs.append(h)
    return hs[0] + hs[1], hs[0][:, -1], hs[1][:, 0]


BF16 = jnp.bfloat16
T_PROMPT = BATCH * SEQ
T_SAMPLE = DEC_BATCH * DEC_SEQ
T_ALL = T_PROMPT + T_SAMPLE
N_COND = 1 + DEC_BATCH
TM = 1024
TM_SH = 256
TN_IN = 256
TN_OUT = 512
TN_MOD = 1024
TM_E = 256
N_ASSIGN = T_ALL * TOP_K
N_BLK_MAX = N_ASSIGN // TM_E + N_EXPERTS
LANES = 128
VMEM_LIMIT = 56 * 1024 * 1024


def _cparams(*sem):
    return pltpu.CompilerParams(dimension_semantics=sem, vmem_limit_bytes=VMEM_LIMIT)


def _cond_row(i, tm):
    return jnp.where(i < T_PROMPT // tm, 0, 1 + (i - T_PROMPT // tm) // (DEC_SEQ // tm))


def _mod_spec(which, tm, tn=D_MODEL):
    if tn == D_MODEL:
        return pl.BlockSpec((1, 1, D_MODEL), lambda i, *_: (_cond_row(i, tm) * 6 + which, 0, 0))
    return pl.BlockSpec((1, 1, tn), lambda i, j: (_cond_row(i, tm) * 6 + which, 0, j))


def _mod_body(c_ref, *rest):
    w_refs, (b_ref, o_ref) = rest[:W_STREAMS], rest[W_STREAMS:]
    c = c_ref[...]
    a = (c * jax.nn.sigmoid(c)).astype(BF16)
    o_ref[...] = _dot_k_slabs(a, w_refs) + b_ref[0]


def modulation(cond, w_mod, b_mod, l):
    n = 6 * D_MODEL
    return pl.pallas_call(
        _mod_body,
        grid=(n // TN_MOD,),
        in_specs=[pl.BlockSpec((8, D_MODEL), lambda j: (0, 0))]
                 + [pl.BlockSpec((None, D_MODEL // W_STREAMS, TN_MOD), lambda j, s=s: (l, s, j)) for s in range(W_STREAMS)]
                 + [pl.BlockSpec((None, 1, TN_MOD), lambda j: (l, 0, j))],
        out_specs=pl.BlockSpec((8, TN_MOD), lambda j: (0, j)),
        out_shape=jax.ShapeDtypeStruct((8, n), F32),
        compiler_params=_cparams("arbitrary"),
        name="modulation",
    )(cond, *([w_mod] * W_STREAMS), b_mod.reshape(DEPTH, 1, n))


def _norm_mod(x, g, sc, sh):
    y = x * lax.rsqrt(jnp.mean(x * x, axis=-1, keepdims=True) + EPS) * g
    return y * (1.0 + sc) + sh


W_STREAMS = 4


def _k_slab_specs(rows, cols, index):
    return [pl.BlockSpec((None, rows // W_STREAMS, cols), lambda i, j, s=s: (index(i, j)[0], s, index(i, j)[1]))
            for s in range(W_STREAMS)]


def _dot_k_slabs(a, w_refs):
    k = a.shape[1] // len(w_refs)
    acc = jnp.dot(a[:, 0:k], w_refs[0][...].astype(BF16), preferred_element_type=F32)
    for s in range(1, len(w_refs)):
        acc = acc + jnp.dot(a[:, s * k:(s + 1) * k], w_refs[s][...].astype(BF16), preferred_element_type=F32)
    return acc


def _in_proj_body(x_ref, g_ref, sc_ref, sh_ref, *rest):
    w_refs, (o_ref, hb_ref) = rest[:W_STREAMS], rest[W_STREAMS:]

    @pl.when(pl.program_id(1) == 0)
    def _():
        hb_ref[...] = _norm_mod(x_ref[...], g_ref[0], sc_ref[0], sh_ref[0]).astype(BF16)

    o_ref[...] = _dot_k_slabs(hb_ref[...], w_refs)


def in_proj(x, norm1, mod, w_in, l):
    return pl.pallas_call(
        _in_proj_body,
        grid=(T_ALL // TM, D_IN // TN_IN),
        in_specs=[pl.BlockSpec((TM, D_MODEL), lambda i, j: (i, 0)),
                  pl.BlockSpec((None, 1, D_MODEL), lambda i, j: (l, 0, 0)),
                  _mod_spec(1, TM), _mod_spec(0, TM)] + _k_slab_specs(D_MODEL, TN_IN, lambda i, j: (l, j)),
        out_specs=pl.BlockSpec((TM, TN_IN), lambda i, j: (i, j)),
        out_shape=jax.ShapeDtypeStruct((T_ALL, D_IN), F32),
        scratch_shapes=[pltpu.VMEM((TM, D_MODEL), BF16)],
        compiler_params=_cparams("arbitrary", "arbitrary"),
        name="in_proj",
    )(x, norm1.reshape(DEPTH, 1, D_MODEL), mod, mod, *([w_in] * W_STREAMS))


N_MIX = D_MIX // GROUP_W


def _out_proj_body(*refs):
    m_refs, (w_ref, x_ref, g1_ref, o_ref, mb_ref) = refs[:N_MIX], refs[N_MIX:]

    @pl.when(pl.program_id(1) == 0)
    def _():
        for g in range(N_MIX):
            mb_ref[g] = m_refs[g][...].astype(BF16)

    acc = jnp.dot(mb_ref[0], w_ref[0:GROUP_W, :].astype(BF16), preferred_element_type=F32)
    for g in range(1, N_MIX):
        acc = acc + jnp.dot(mb_ref[g], w_ref[g * GROUP_W:(g + 1) * GROUP_W, :].astype(BF16), preferred_element_type=F32)
    o_ref[...] = x_ref[...] + g1_ref[0] * acc


def out_proj(mixes, w_out, x, mod, l):
    return pl.pallas_call(
        _out_proj_body,
        grid=(T_ALL // TM, D_MODEL // TN_OUT),
        in_specs=[pl.BlockSpec((TM, GROUP_W), lambda i, j: (i, 0))] * N_MIX + [
                  pl.BlockSpec((None, D_MIX, TN_OUT), lambda i, j: (l, 0, j)),
                  pl.BlockSpec((TM, TN_OUT), lambda i, j: (i, j)),
                  _mod_spec(2, TM, TN_OUT)],
        out_specs=pl.BlockSpec((TM, TN_OUT), lambda i, j: (i, j)),
        out_shape=jax.ShapeDtypeStruct((T_ALL, D_MODEL), F32),
        scratch_shapes=[pltpu.VMEM((N_MIX, TM, GROUP_W), BF16)],
        compiler_params=_cparams("arbitrary", "arbitrary"),
        name="out_proj",
    )(*mixes, w_out, x, mod)


def _router_body(x_ref, g_ref, sc_ref, sh_ref, wr_ref, rb_ref, h_ref, idx_ref, gate_ref, rank_ref, cnt_ref, carry_ref):
    i = pl.program_id(0)

    @pl.when(i == 0)
    def _():
        carry_ref[...] = jnp.zeros_like(carry_ref)

    h = _norm_mod(x_ref[...], g_ref[0], sc_ref[0], sh_ref[0])
    h_ref[...] = h
    h_hi, h_lo = _split_bf16(h)
    w_hi, w_lo = _split_bf16(wr_ref[...])
    logits = (jnp.dot(h_hi, w_hi, preferred_element_type=F32) + jnp.dot(h_lo, w_hi, preferred_element_type=F32)
              + jnp.dot(h_hi, w_lo, preferred_element_type=F32))
    s = jax.nn.sigmoid(logits)
    cur = s + rb_ref[0]
    e_iota = lax.broadcasted_iota(jnp.int32, s.shape, 1).astype(F32)
    lane = lax.broadcasted_iota(jnp.int32, (TM, LANES), 1)
    r_iota = lax.broadcasted_iota(jnp.int32, (TM, TM), 0)
    c_iota = lax.broadcasted_iota(jnp.int32, (TM, TM), 1)
    tri = (c_iota <= r_iota).astype(BF16)
    idx_out = jnp.zeros((TM, LANES), F32)
    gate_out = jnp.zeros((TM, LANES), F32)
    rank_out = jnp.zeros((TM, LANES), F32)
    gsum = jnp.zeros((TM, 1), F32)
    carry = carry_ref[...]
    for k in range(TOP_K):
        m = jnp.max(cur, axis=-1, keepdims=True)
        ek = jnp.min(jnp.where(cur == m, e_iota, float(N_EXPERTS)), axis=-1, keepdims=True)
        hit = e_iota == ek
        gk = jnp.sum(jnp.where(hit, s, 0.0), axis=-1, keepdims=True)
        cur = jnp.where(hit, -jnp.inf, cur)
        onehot = hit.astype(BF16)
        cum = jnp.dot(tri, onehot, preferred_element_type=F32)
        rk = jnp.sum(jnp.where(hit, carry + cum, 0.0), axis=-1, keepdims=True) - 1.0
        carry = carry + cum[TM - 1:TM, :]
        gsum = gsum + gk
        idx_out = jnp.where(lane == k, ek, idx_out)
        gate_out = jnp.where(lane == k, gk, gate_out)
        rank_out = jnp.where(lane == k, rk, rank_out)
    carry_ref[...] = carry
    idx_ref[...] = idx_out.astype(jnp.int32)
    gate_ref[...] = ROUTE_SCALE * gate_out / gsum
    rank_ref[...] = rank_out.astype(jnp.int32)
    cnt_ref[...] = carry.astype(jnp.int32)


def router(x1, norm2, mod, router_w, router_bias, l):
    tok_out = lambda dt: jax.ShapeDtypeStruct((T_ALL, LANES), dt)
    tok_spec = pl.BlockSpec((TM, LANES), lambda i: (i, 0))
    return pl.pallas_call(
        _router_body,
        grid=(T_ALL // TM,),
        in_specs=[pl.BlockSpec((TM, D_MODEL), lambda i: (i, 0)),
                  pl.BlockSpec((None, 1, D_MODEL), lambda i: (l, 0, 0)),
                  _mod_spec(4, TM), _mod_spec(3, TM),
                  pl.BlockSpec((None, D_MODEL, N_EXPERTS), lambda i: (l, 0, 0)),
                  pl.BlockSpec((None, 1, N_EXPERTS), lambda i: (l, 0, 0))],
        out_specs=[pl.BlockSpec((TM, D_MODEL), lambda i: (i, 0)), tok_spec, tok_spec, tok_spec,
                   pl.BlockSpec((1, N_EXPERTS), lambda i: (0, 0))],
        out_shape=[jax.ShapeDtypeStruct((T_ALL, D_MODEL), F32), tok_out(jnp.int32), tok_out(F32), tok_out(jnp.int32),
                   jax.ShapeDtypeStruct((1, N_EXPERTS), jnp.int32)],
        scratch_shapes=[pltpu.VMEM((1, N_EXPERTS), F32)],
        compiler_params=_cparams("arbitrary"),
        name="router",
    )(x1, norm2.reshape(DEPTH, 1, D_MODEL), mod, mod, router_w, router_bias.reshape(DEPTH, 1, N_EXPERTS))


def _experts_body(be_ref, nv_ref, src_ref, nxt_ref, h_hbm, *rest):
    wgu_refs, wdn_refs = rest[:W_STREAMS], rest[W_STREAMS:2 * W_STREAMS]
    y_ref, wgu_b, wdn_b, xg0, xg1, sems = rest[2 * W_STREAMS:]
    i = pl.program_id(0)
    n_valid = nv_ref[0]

    def row_copy(idx_ref, t, buf, sem):
        return pltpu.make_async_copy(h_hbm.at[pl.ds(idx_ref[0, t], 1)], buf.at[pl.ds(t, 1)], sem)

    def all_rows(buf, sem):
        return pltpu.make_async_copy(h_hbm.at[pl.ds(0, TM_E)], buf, sem)

    @pl.when(i == 0)
    def _():
        def issue(t, carry):
            row_copy(src_ref, t, xg0, sems.at[0]).start()
            return carry

        lax.fori_loop(0, TM_E, issue, 0, unroll=8)

    @pl.when(i < n_valid)
    def _():
        prev = be_ref[jnp.maximum(i - 1, 0)]

        @pl.when(jnp.logical_or(i == 0, be_ref[i] != prev))
        def _():
            kg, kd = D_MODEL // W_STREAMS, EXPERT_FF // W_STREAMS
            for s in range(W_STREAMS):
                wgu_b[s * kg:(s + 1) * kg, :] = wgu_refs[s][...].astype(BF16)
                wdn_b[s * kd:(s + 1) * kd, :] = wdn_refs[s][...].astype(BF16)

        for parity, (cur, nxt) in enumerate(((xg0, xg1), (xg1, xg0))):
            @pl.when(i % 2 == parity)
            def _():
                all_rows(cur, sems.at[parity]).wait()
                for t in range(TM_E):
                    row_copy(nxt_ref, t, nxt, sems.at[1 - parity]).start()
                hmid = jnp.dot(cur[...].astype(BF16), wgu_b[...], preferred_element_type=F32)
                a, b = hmid[:, :EXPERT_FF], hmid[:, EXPERT_FF:]
                act = (a * jax.nn.sigmoid(a) * b).astype(BF16)
                y_ref[...] = jnp.dot(act, wdn_b[...], preferred_element_type=F32)

                @pl.when(i == n_valid - 1)
                def _():
                    all_rows(nxt, sems.at[1 - parity]).wait()

    @pl.when(i >= n_valid)
    def _():
        y_ref[...] = jnp.zeros_like(y_ref)


def routed_experts(h2, src, blk_e, n_valid, moe_w_gu, moe_w_dn, l):
    idx_spec = lambda off: pl.BlockSpec((None, 1, TM_E), lambda i, be, nv: (jnp.minimum(i + off, nv[0] - 1), 0, 0),
                                        memory_space=pltpu.SMEM)
    return pl.pallas_call(
        _experts_body,
        grid_spec=pltpu.PrefetchScalarGridSpec(
            num_scalar_prefetch=2,
            grid=(N_BLK_MAX,),
            in_specs=[idx_spec(0), idx_spec(1),
                      pl.BlockSpec(memory_space=pl.ANY)]
                     + [pl.BlockSpec((None, None, D_MODEL // W_STREAMS, 2 * EXPERT_FF),
                                     lambda i, be, nv, s=s: (l, be[i], s, 0)) for s in range(W_STREAMS)]
                     + [pl.BlockSpec((None, None, EXPERT_FF // W_STREAMS, D_MODEL),
                                     lambda i, be, nv, s=s: (l, be[i], s, 0)) for s in range(W_STREAMS)],
            out_specs=pl.BlockSpec((TM_E, D_MODEL), lambda i, be, nv: (i, 0)),
            scratch_shapes=[pltpu.VMEM((D_MODEL, 2 * EXPERT_FF), BF16), pltpu.VMEM((EXPERT_FF, D_MODEL), BF16),
                            pltpu.VMEM((TM_E, D_MODEL), F32), pltpu.VMEM((TM_E, D_MODEL), F32),
                            pltpu.SemaphoreType.DMA((2,))]),
        out_shape=jax.ShapeDtypeStruct((N_BLK_MAX * TM_E, D_MODEL), F32),
        compiler_params=_cparams("arbitrary"),
        name="routed_experts",
    )(blk_e, n_valid, src.reshape(N_BLK_MAX, 1, TM_E), src.reshape(N_BLK_MAX, 1, TM_E), h2,
      *([moe_w_gu] * W_STREAMS), *([moe_w_dn] * W_STREAMS))


def _gathered_rows(ybuf_hbm, rows, sem, n):
    return pltpu.make_async_copy(ybuf_hbm.at[pl.ds(0, n)], rows.at[pl.ds(0, n)], sem)


def _shared_body(dest_ref, gate_ref, h_ref, wgu_ref, wdn_ref, x_ref, g2_ref, ybuf_hbm, o_ref, wgu_b, wdn_b, rows, sem):
    @pl.when(pl.program_id(0) == 0)
    def _():
        wgu_b[...] = wgu_ref[...].astype(BF16)
        wdn_b[...] = wdn_ref[...].astype(BF16)

    for t in range(TM_SH):
        for k in range(TOP_K):
            src = dest_ref[0, t * TOP_K + k]
            pltpu.make_async_copy(ybuf_hbm.at[pl.ds(src, 1)], rows.at[pl.ds(k * TM_SH + t, 1)], sem).start()
    hmid = jnp.dot(h_ref[...].astype(BF16), wgu_b[...], preferred_element_type=F32)
    a, b = hmid[:, :SHARED_FF], hmid[:, SHARED_FF:]
    act = (a * jax.nn.sigmoid(a) * b).astype(BF16)
    y = jnp.dot(act, wdn_b[...], preferred_element_type=F32)
    _gathered_rows(ybuf_hbm, rows, sem, TOP_K * TM_SH).wait()
    gates = gate_ref[...]
    for k in range(TOP_K):
        y = y + gates[:, k:k + 1] * rows[k * TM_SH:(k + 1) * TM_SH, :]
    o_ref[...] = x_ref[...] + g2_ref[0] * y


def shared_combine(h2, sh_w_gu, sh_w_dn, ybuf, dest, gates, x1, mod, l):
    tile = pl.BlockSpec((TM_SH, D_MODEL), lambda i: (i, 0))
    once = pl.Buffered(1)
    n_tiles = T_ALL // TM_SH
    return pl.pallas_call(
        _shared_body,
        grid=(n_tiles,),
        in_specs=[pl.BlockSpec((None, 1, TM_SH * TOP_K), lambda i: (i, 0, 0), memory_space=pltpu.SMEM),
                  pl.BlockSpec((TM_SH, LANES), lambda i: (i, 0)),
                  tile,
                  pl.BlockSpec((None, D_MODEL, 2 * SHARED_FF), lambda i: (l, 0, 0), pipeline_mode=once),
                  pl.BlockSpec((None, SHARED_FF, D_MODEL), lambda i: (l, 0, 0), pipeline_mode=once),
                  tile, _mod_spec(5, TM_SH),
                  pl.BlockSpec(memory_space=pl.ANY)],
        out_specs=tile,
        out_shape=jax.ShapeDtypeStruct((T_ALL, D_MODEL), F32),
        scratch_shapes=[pltpu.VMEM((D_MODEL, 2 * SHARED_FF), BF16), pltpu.VMEM((SHARED_FF, D_MODEL), BF16),
                        pltpu.VMEM((TOP_K * TM_SH, D_MODEL), F32), pltpu.SemaphoreType.DMA(())],
        compiler_params=_cparams("arbitrary"),
        name="shared_combine",
    )(dest.reshape(n_tiles, 1, TM_SH * TOP_K), gates, h2, sh_w_gu, sh_w_dn, x1, mod, ybuf)


def _final_norm_body(x_ref, g_ref, o_ref):
    x = x_ref[...]
    o_ref[...] = x * lax.rsqrt(jnp.mean(x * x, axis=-1, keepdims=True) + EPS) * g_ref[...]


def final_rmsnorm(x, g):
    return pl.pallas_call(
        _final_norm_body,
        grid=(T_ALL // TM,),
        in_specs=[pl.BlockSpec((TM, D_MODEL), lambda i: (i, 0)), pl.BlockSpec((1, D_MODEL), lambda i: (0, 0))],
        out_specs=pl.BlockSpec((TM, D_MODEL), lambda i: (i, 0)),
        out_shape=jax.ShapeDtypeStruct(x.shape, x.dtype),
        compiler_params=_cparams("arbitrary"),
        name="final_norm",
    )(x, g.reshape(1, D_MODEL))


TQ_DIFF = 256
HALF = LANES // 2
NT_DIMS = (((1,), (1,)), ((), ()))


def rope_tables(L):
    pos = jnp.arange(L)
    d = jnp.arange(LANES) % HEAD_DIM
    p = jnp.where(d // (2 * ROPE_PAIRS) == 0, (pos // GRID_W)[:, None], (pos % GRID_W)[:, None]).astype(F32)
    inv = ROPE_BASE ** (-(d % ROPE_PAIRS).astype(F32) / ROPE_PAIRS)
    ang = p * inv
    return jnp.cos(ang), jnp.where((d // ROPE_PAIRS) % 2 == 0, -jnp.sin(ang), jnp.sin(ang))


def _rope(x, cos, sin_signed):
    lane = lax.broadcasted_iota(jnp.int32, x.shape, 1)
    is_x1 = (lane // ROPE_PAIRS) % 2 == 0
    partner = jnp.where(is_x1, pltpu.roll(x, LANES - ROPE_PAIRS, 1), pltpu.roll(x, ROPE_PAIRS, 1))
    return x * cos + partner * sin_signed


def _exp_and_inv_sum(s):
    e = jnp.exp(s - jnp.max(s, axis=-1, keepdims=True))
    return e, 1.0 / jnp.sum(e, axis=-1, keepdims=True)


def _diff_body(*refs, lk, ctx, lam_init):
    *ins, _aliased_dst, o_ref, kb, vb = refs
    if ctx:
        q_ref, k_ref, v_ref, dl_ref, g_ref, ck_ref, cv_ref, cq_ref, sq_ref, ckk_ref, skk_ref = ins
    else:
        q_ref, k_ref, v_ref, dl_ref, g_ref = ins

    @pl.when(pl.program_id(2) == 0)
    def _():
        k = k_ref[...]
        if ctx:
            k = _rope(k, ckk_ref[...], skk_ref[...])
            kb[lk:, :] = ck_ref[...].astype(BF16)
            vb[lk:, :] = cv_ref[...].astype(BF16)
        kb[0:lk, :] = k.astype(BF16)
        vb[0:lk, :] = v_ref[...].astype(BF16)

    q = q_ref[...] * DIFF_SCALE
    if ctx:
        q = _rope(q, cq_ref[...], sq_ref[...])
    lane = lax.broadcasted_iota(jnp.int32, q.shape, 1)
    q1 = jnp.where(lane < HALF, q, 0.0).astype(BF16)
    q2 = jnp.where(lane >= HALF, q, 0.0).astype(BF16)
    keys = kb[...]
    e1, inv1 = _exp_and_inv_sum(lax.dot_general(q1, keys, NT_DIMS, preferred_element_type=F32))
    e2, inv2 = _exp_and_inv_sum(lax.dot_general(q2, keys, NT_DIMS, preferred_element_type=F32))
    dl = dl_ref[...]
    lam = (jnp.exp(jnp.sum(dl[0:1] * dl[1:2], axis=-1, keepdims=True))
           - jnp.exp(jnp.sum(dl[2:3] * dl[3:4], axis=-1, keepdims=True)) + lam_init)
    o = jnp.dot((e1 * inv1 - e2 * (lam * inv2)).astype(BF16), vb[...], preferred_element_type=F32)
    o = o * lax.rsqrt(jnp.mean(o * o, axis=-1, keepdims=True) + EPS) * g_ref[...]
    o_ref[...] = o * (1.0 - lam_init)


def diff_attention(proj, dst, diff_lam, diff_subln, l, lam_init, cache_k=None, cache_v=None, tables=None):
    ctx = cache_k is not None
    nh = DIFF_HEADS
    if ctx:
        nb, lk, tq, row0 = DEC_BATCH, DEC_SEQ, TQ_DIFF, T_PROMPT
    else:
        nb, lk, tq, row0 = BATCH, SEQ, SEQ, 0
    in_specs = [pl.BlockSpec((tq, LANES), lambda b, h, qi: ((row0 + b * lk) // tq + qi, h)),
                pl.BlockSpec((lk, LANES), lambda b, h, qi: (row0 // lk + b, nh + h)),
                pl.BlockSpec((lk, LANES), lambda b, h, qi: (row0 // lk + b, 2 * nh + h)),
                pl.BlockSpec((None, 4, DIFF_QK), lambda b, h, qi: (l, 0, 0)),
                pl.BlockSpec((None, 1, DIFF_VD), lambda b, h, qi: (l, 0, 0))]
    args = [proj, proj, proj, diff_lam, diff_subln.reshape(DEPTH, 1, DIFF_VD)]
    if ctx:
        cos, sin = tables
        ctx_spec = pl.BlockSpec((None, None, PAST_LEN, LANES), lambda b, h, qi: (b, l, 0, h))
        q_tab = pl.BlockSpec((tq, LANES), lambda b, h, qi: (qi, 0))
        k_tab = pl.BlockSpec((lk, LANES), lambda b, h, qi: (0, 0))
        in_specs += [ctx_spec, ctx_spec, q_tab, q_tab, k_tab, k_tab]
        args += [cache_k.reshape(DEC_BATCH, DEPTH, PAST_LEN, GROUP_W), cache_v.reshape(DEC_BATCH, DEPTH, PAST_LEN, GROUP_W),
                 cos, sin, cos, sin]
    n_keys = lk + (PAST_LEN if ctx else 0)
    return pl.pallas_call(
        partial(_diff_body, lk=lk, ctx=ctx, lam_init=lam_init),
        grid=(nb, nh, lk // tq),
        in_specs=in_specs + [pl.BlockSpec(memory_space=pl.ANY)],
        out_specs=pl.BlockSpec((tq, LANES), lambda b, h, qi: ((row0 + b * lk) // tq + qi, h)),
        out_shape=jax.ShapeDtypeStruct((T_ALL, GROUP_W), F32),
        input_output_aliases={len(args): 0},
        scratch_shapes=[pltpu.VMEM((n_keys, LANES), BF16), pltpu.VMEM((n_keys, LANES), BF16)],
        compiler_params=_cparams("arbitrary", "arbitrary", "arbitrary"),
        name="diff_attention_ctx" if ctx else "diff_attention",
    )(*args, dst)


def _swa_body(*refs, lk, tq, banded):
    *ins, _aliased_dst, o_ref, kb, vb = refs
    if banded:
        q_ref, k_ref, v_ref, sk_ref, ck_ref, cv_ref, cq_ref, sq_ref, ckk_ref, skk_ref = ins
    else:
        q_ref, k_ref, v_ref, sk_ref = ins
    qi = pl.program_id(1)

    @pl.when(qi == 0)
    def _():
        k = k_ref[...]
        if banded:
            k = _rope(k, ckk_ref[...], skk_ref[...])
            zeros = jnp.zeros((WINDOW, LANES), BF16)
            kb[0:WINDOW, :] = zeros
            vb[0:WINDOW, :] = zeros
            kb[WINDOW + lk:2 * WINDOW + lk, :] = zeros
            vb[WINDOW + lk:2 * WINDOW + lk, :] = zeros
            kb[2 * WINDOW + lk:, :] = ck_ref[...].astype(BF16)
            vb[2 * WINDOW + lk:, :] = cv_ref[...].astype(BF16)
            kb[WINDOW:WINDOW + lk, :] = k.astype(BF16)
            vb[WINDOW:WINDOW + lk, :] = v_ref[...].astype(BF16)
        else:
            kb[...] = k.astype(BF16)
            vb[...] = v_ref[...].astype(BF16)

    lane = lax.broadcasted_iota(jnp.int32, (tq, LANES), 1)
    sinks = sk_ref[...]
    if banded:
        start = pl.multiple_of(qi * WINDOW, WINDOW)
        k_loc, v_loc = kb[pl.ds(start, 3 * WINDOW), :], vb[pl.ds(start, 3 * WINDOW), :]
        k_ctx, v_ctx = kb[2 * WINDOW + lk:, :], vb[2 * WINDOW + lk:, :]
        row = lax.broadcasted_iota(jnp.int32, (SWA_GROUP * tq, 3 * WINDOW), 0) % tq
        col = lax.broadcasted_iota(jnp.int32, (SWA_GROUP * tq, 3 * WINDOW), 1)
        kpos = (qi - 1) * WINDOW + col
        visible = (col >= row) & (col - row <= 2 * WINDOW) & (kpos >= 0) & (kpos < lk)
    else:
        k_loc, v_loc = kb[...], vb[...]
    outs = []
    for kh in range(SWA_KV_HEADS):
        in_half = (lane >= HALF) if kh else (lane < HALF)
        qs, sink_rows = [], []
        for g in range(SWA_GROUP):
            head = kh * SWA_GROUP + g
            t = q_ref[:, (head // 2) * LANES:(head // 2 + 1) * LANES] * SWA_SCALE
            if banded:
                t = _rope(t, cq_ref[...], sq_ref[...])
            if head % 2 != kh:
                t = pltpu.roll(t, HALF, 1)
            qs.append(jnp.where(in_half, t, 0.0).astype(BF16))
            sink_rows.append(jnp.broadcast_to(sinks[:, head:head + 1], (tq, 1)))
        qs = jnp.concatenate(qs, axis=0)
        sink = jnp.concatenate(sink_rows, axis=0)
        s_loc = lax.dot_general(qs, k_loc, NT_DIMS, preferred_element_type=F32)
        m = sink
        if banded:
            s_loc = jnp.where(visible, s_loc, NEG_INF)
            s_ctx = lax.dot_general(qs, k_ctx, NT_DIMS, preferred_element_type=F32)
            m = jnp.maximum(m, jnp.max(s_ctx, axis=-1, keepdims=True))
        m = jnp.maximum(m, jnp.max(s_loc, axis=-1, keepdims=True))
        e_loc = jnp.exp(s_loc - m)
        den = jnp.sum(e_loc, axis=-1, keepdims=True) + jnp.exp(sink - m)
        if banded:
            e_ctx = jnp.exp(s_ctx - m)
            den = den + jnp.sum(e_ctx, axis=-1, keepdims=True)
        inv = 1.0 / den
        o = jnp.dot((e_loc * inv).astype(BF16), v_loc, preferred_element_type=F32)
        if banded:
            o = o + jnp.dot((e_ctx * inv).astype(BF16), v_ctx, preferred_element_type=F32)
        outs.append(o)
    for t in range(SWA_HEADS // 2):
        kh, g0 = (2 * t) // SWA_GROUP, (2 * t) % SWA_GROUP
        a = outs[kh][g0 * tq:(g0 + 1) * tq]
        b = outs[kh][(g0 + 1) * tq:(g0 + 2) * tq]
        a = pltpu.roll(a, HALF, 1) if kh == 1 else a
        b = pltpu.roll(b, HALF, 1) if kh == 0 else b
        o_ref[:, t * LANES:(t + 1) * LANES] = jnp.where(lane < HALF, a, b)


def swa_attention(proj, dst, swa_sink, l, cache_k=None, cache_v=None, tables=None):
    banded = cache_k is not None
    q_col, k_col = 3, (4 * GROUP_W) // LANES
    if banded:
        nb, lk, tq, row0 = DEC_BATCH, DEC_SEQ, WINDOW, T_PROMPT
    else:
        nb, lk, tq, row0 = BATCH, SEQ, SEQ, 0
    in_specs = [pl.BlockSpec((tq, GROUP_W), lambda b, qi: ((row0 + b * lk) // tq + qi, q_col)),
                pl.BlockSpec((lk, LANES), lambda b, qi: (row0 // lk + b, k_col)),
                pl.BlockSpec((lk, LANES), lambda b, qi: (row0 // lk + b, k_col + 1)),
                pl.BlockSpec((None, 1, SWA_HEADS), lambda b, qi: (l, 0, 0))]
    args = [proj, proj, proj, swa_sink.reshape(DEPTH, 1, SWA_HEADS)]
    if banded:
        cos, sin = tables
        ctx_spec = pl.BlockSpec((None, None, PAST_LEN, LANES), lambda b, qi: (b, l, 0, 0))
        q_tab = pl.BlockSpec((tq, LANES), lambda b, qi: (qi, 0))
        k_tab = pl.BlockSpec((lk, LANES), lambda b, qi: (0, 0))
        in_specs += [ctx_spec, ctx_spec, q_tab, q_tab, k_tab, k_tab]
        args += [cache_k.reshape(DEC_BATCH, DEPTH, PAST_LEN, LANES), cache_v.reshape(DEC_BATCH, DEPTH, PAST_LEN, LANES),
                 cos, sin, cos, sin]
    n_keys = lk + 2 * WINDOW + PAST_LEN if banded else lk
    return pl.pallas_call(
        partial(_swa_body, lk=lk, tq=tq, banded=banded),
        grid=(nb, lk // tq),
        in_specs=in_specs + [pl.BlockSpec(memory_space=pl.ANY)],
        out_specs=pl.BlockSpec((tq, GROUP_W), lambda b, qi: ((row0 + b * lk) // tq + qi, 0)),
        out_shape=jax.ShapeDtypeStruct((T_ALL, GROUP_W), F32),
        input_output_aliases={len(args): 0},
        scratch_shapes=[pltpu.VMEM((n_keys, LANES), BF16), pltpu.VMEM((n_keys, LANES), BF16)],
        compiler_params=_cparams("arbitrary", "arbitrary"),
        name="swa_attention_banded" if banded else "swa_attention",
    )(*args, dst)


LRU_GROUPS = LRU_W // LANES


def lru_gate_params(wa, ba, wx, bx):
    def tile_blocks(w):
        w = w.reshape(LRU_GROUPS, 2, LRU_BD, LRU_BD)
        z = jnp.zeros((LRU_GROUPS, LRU_BD, LRU_BD), F32)
        return jnp.concatenate([jnp.concatenate([w[:, 0], z], axis=-1), jnp.concatenate([z, w[:, 1]], axis=-1)], axis=-2)

    w = jnp.concatenate([tile_blocks(wa[0]), tile_blocks(wx[0]), tile_blocks(wa[1]), tile_blocks(wx[1])], axis=-1)
    b = jnp.stack([ba[0], bx[0], ba[1], bx[1]], axis=0).reshape(4, LRU_GROUPS, LANES)
    return w, jnp.moveaxis(b, 0, 1).reshape(LRU_GROUPS, 1, 4 * LANES)


def _shift_rows(x, s, row, n):
    if s == 0:
        return x
    ok = (row >= -s) if s < 0 else (row < n - s)
    return jnp.where(ok, pltpu.roll(x, (-s) % n, 0), 0.0)


def _linear_scan(a, b, row, n, reverse):
    s = 1
    while s < n:
        ok = (row < n - s) if reverse else (row >= s)
        shift = (n - s) if reverse else s
        b = jnp.where(ok, a * pltpu.roll(b, shift, 0) + b, b)
        a = jnp.where(ok, a * pltpu.roll(a, shift, 0), a)
        s *= 2
    return b


def _lru_body(x_ref, lg_ref, cw_ref, w_ref, bias_ref, lam_ref, h0_ref, _aliased_dst, o_ref, st_ref, *, n):
    x = x_ref[...]
    row = lax.broadcasted_iota(jnp.int32, x.shape, 0)
    cw = cw_ref[...]
    xc = _shift_rows(x, -(LRU_CONV // 2), row, n) * cw[0:1]
    for k in range(1, LRU_CONV):
        xc = xc + _shift_rows(x, k - LRU_CONV // 2, row, n) * cw[k:k + 1]
    gates = jnp.dot(xc.astype(BF16), w_ref[...].astype(BF16), preferred_element_type=F32) + bias_ref[...]
    lam = lam_ref[...]
    log_sig = jnp.minimum(lam, 0.0) - jnp.log1p(jnp.exp(-jnp.abs(lam)))
    h0 = h0_ref[...]
    hs = []
    for d in range(2):
        r = jax.nn.sigmoid(gates[:, (2 * d) * LANES:(2 * d + 1) * LANES])
        i = jax.nn.sigmoid(gates[:, (2 * d + 1) * LANES:(2 * d + 2) * LANES])
        log_a = LRU_C * r * log_sig[d:d + 1]
        a = jnp.exp(log_a)
        th = jnp.tanh(log_a)
        b = jnp.sqrt(-2.0 * th / (1.0 - th)) * (i * xc)
        first = n - 1 if d else 0
        b = jnp.where(row == first, b + a * h0[d:d + 1], b)
        hs.append(_linear_scan(a, b, row, n, reverse=bool(d)))
    lg = lg_ref[...]
    gelu = 0.5 * lg * (1.0 + jnp.tanh(math.sqrt(2.0 / math.pi) * (lg + 0.044715 * (lg * lg * lg))))
    o_ref[...] = (hs[0] + hs[1]) * gelu
    st_ref[0:1, :] = hs[0][n - 1:n, :]
    st_ref[1:2, :] = hs[1][0:1, :]


def rglru(proj, dst, gate_w, gate_b, lru_conv, lru_lam, h0, l, h0_l, prompt):
    nb, n, row0 = (BATCH, SEQ, 0) if prompt else (DEC_BATCH, DEC_SEQ, T_PROMPT)
    x_col = (D_IN - 2 * LRU_W) // LANES
    return pl.pallas_call(
        partial(_lru_body, n=n),
        grid=(nb, LRU_GROUPS),
        in_specs=[pl.BlockSpec((n, LANES), lambda s, c: (row0 // n + s, x_col + c)),
                  pl.BlockSpec((n, LANES), lambda s, c: (row0 // n + s, x_col + LRU_GROUPS + c)),
                  pl.BlockSpec((None, LRU_CONV, LANES), lambda s, c: (l, 0, c)),
                  pl.BlockSpec((None, LANES, 4 * LANES), lambda s, c: (c, 0, 0)),
                  pl.BlockSpec((None, 1, 4 * LANES), lambda s, c: (c, 0, 0)),
                  pl.BlockSpec((None, 2, LANES), lambda s, c: (l, 0, c)),
                  pl.BlockSpec((None, None, 2, LANES), lambda s, c: (s, h0_l, 0, c)),
                  pl.BlockSpec(memory_space=pl.ANY)],
        out_specs=[pl.BlockSpec((n, LANES), lambda s, c: (row0 // n + s, c)),
                   pl.BlockSpec((None, 2, LANES), lambda s, c: (s, 0, c))],
        out_shape=[jax.ShapeDtypeStruct((T_ALL, LRU_W), F32), jax.ShapeDtypeStruct((nb, 2, LRU_W), F32)],
        input_output_aliases={7: 0},
        compiler_params=_cparams("arbitrary", "arbitrary"),
        name="rglru_prompt" if prompt else "rglru_sample",
    )(proj, proj, lru_conv, gate_w, gate_b, lru_lam, h0, dst)


HY_P = 256
HY_F = 2 * HY_P
HY_ACC_VREGS = 8


def _split_bf16(x):
    hi = x.astype(BF16)
    return hi, (x - hi.astype(F32)).astype(BF16)


def _dot3(c_hi, c_lo, d):
    d_hi, d_lo = _split_bf16(d)
    return (jnp.dot(c_hi, d_hi, preferred_element_type=F32) + jnp.dot(c_hi, d_lo, preferred_element_type=F32)
            + jnp.dot(c_lo, d_hi, preferred_element_type=F32))


def dft_constants():
    f = np.arange(HY_F)[:, None].astype(np.float64)
    ang = 2.0 * np.pi * f * np.arange(HY_F)[None, :] / HY_F
    c, s = np.cos(ang), np.sin(ang)
    ch, sh = c[:, :HY_P], s[:, :HY_P]
    fwd_data = np.block([[ch, sh], [-sh, ch]])
    fwd_filter = np.concatenate([c, -s], axis=0)
    inverse = np.block([[ch.T, -sh.T], [sh.T, ch.T]]) / HY_F
    out = []
    for m in (fwd_data, fwd_filter, inverse):
        hi = m.astype(np.float32).astype(BF16)
        lo = (m - hi.astype(np.float64)).astype(np.float32).astype(BF16)
        out += [jnp.asarray(hi), jnp.asarray(lo)]
    return out


def _filter_spectra_body(cur_ref, prev_ref, fh_ref, fl_ref, g_ref):
    row = lax.broadcasted_iota(jnp.int32, prev_ref.shape, 0)
    prev = jnp.where((row == 0) | (pl.program_id(1) == 0), 0.0, prev_ref[...])
    g_ref[...] = _dot3(fh_ref[...], fl_ref[...], jnp.concatenate([cur_ref[...], prev], axis=0))


def filter_spectra(filt, fk_hi, fk_lo):
    nq, n, _ = filt.shape
    nblk = n // HY_P
    const = pl.BlockSpec((2 * HY_F, HY_F), lambda q, m: (0, 0))
    return pl.pallas_call(
        _filter_spectra_body,
        grid=(nq, nblk),
        in_specs=[pl.BlockSpec((None, HY_P, HY_W), lambda q, m: (q, m, 0)),
                  pl.BlockSpec((None, HY_P, HY_W), lambda q, m: (q, jnp.maximum(m - 1, 0), 0)),
                  const, const],
        out_specs=pl.BlockSpec((None, None, 2 * HY_F, HY_W), lambda q, m: (q, m, 0, 0)),
        out_shape=jax.ShapeDtypeStruct((nq, nblk, 2 * HY_F, HY_W), F32),
        compiler_params=_cparams("arbitrary", "arbitrary"),
        name="hyena_filter_spectra",
    )(filt, filt, fk_hi, fk_lo)


def _hyena_body(z0_ref, z1_ref, x0_ref, x1_ref, cwz_ref, cwx_ref, skip_ref, gf_ref, gb_ref, fdh_ref, fdl_ref,
                gih_ref, gil_ref, o_ref, zs, xs, us, ys, *, n, conv_z):
    nblk = n // HY_P
    row = lax.broadcasted_iota(jnp.int32, z0_ref.shape, 0)

    def short_conv(x, cw):
        acc = _shift_rows(x, -(HY_SHORT // 2), row, n) * cw[0:1]
        for k in range(1, HY_SHORT):
            acc = acc + _shift_rows(x, k - HY_SHORT // 2, row, n) * cw[k:k + 1]
        return acc

    for b, (z_ref, x_ref) in enumerate(((z0_ref, x0_ref), (z1_ref, x1_ref))):
        zs[b] = short_conv(z_ref[...], cwz_ref[...]) if conv_z else z_ref[...]
        xs[b] = short_conv(x_ref[...], cwx_ref[...])

    def forward(j, carry):
        r0 = pl.multiple_of(j * HY_P, HY_P)
        d = jnp.concatenate([zs[0, pl.ds(r0, HY_P), :], zs[1, pl.ds(r0, HY_P), :]], axis=0)
        us[j] = _dot3(fdh_ref[...], fdl_ref[...], d)
        return carry

    lax.fori_loop(0, nblk, forward, 0)

    lanes = z0_ref.shape[1]
    fc = HY_ACC_VREGS * 8 * LANES // lanes

    def out_block(i, carry):
        for c0 in range(0, HY_F, fc):
            re, im = pl.ds(c0, fc), pl.ds(HY_F + c0, fc)

            def causal(j, acc):
                g_re, g_im, u_re, u_im = gf_ref[i - j, re, :], gf_ref[i - j, im, :], us[j, re, :], us[j, im, :]
                return acc[0] + (g_re * u_re - g_im * u_im), acc[1] + (g_re * u_im + g_im * u_re)

            def anticausal(j, acc):
                g_re, g_im, u_re, u_im = gb_ref[j - i, re, :], gb_ref[j - i, im, :], us[j, re, :], us[j, im, :]
                return acc[0] + (g_re * u_re + g_im * u_im), acc[1] + (g_re * u_im - g_im * u_re)

            zero = jnp.zeros((fc, lanes), F32)
            acc = lax.fori_loop(0, i + 1, causal, (zero, zero))
            acc = lax.fori_loop(i, nblk, anticausal, acc)
            ys[re, :] = acc[0]
            ys[im, :] = acc[1]
        y = _dot3(gih_ref[...], gil_ref[...], ys[...])
        r0 = pl.multiple_of(i * HY_P, HY_P)
        for b in range(2):
            conv = y[b * HY_P:(b + 1) * HY_P]
            o_ref[pl.ds(b * n + r0, HY_P), :] = xs[b, pl.ds(r0, HY_P), :] * (conv + skip_ref[...] * zs[b, pl.ds(r0, HY_P), :])
        return carry

    lax.fori_loop(0, nblk, out_block, 0)


def hyena_order(zsrc, z_col0, proj, x_col0, hy_conv, hy_skip, spectra, consts, l, order, prompt):
    fd_hi, fd_lo, _, _, gi_hi, gi_lo = consts
    if prompt:
        n, cw, npair, z_row0 = SEQ, 2 * LANES, BATCH // 2, 0
    else:
        n, cw, npair, z_row0 = DEC_SEQ, LANES, DEC_BATCH // 2, (T_PROMPT // DEC_SEQ if order == 0 else 0)
    x_row0 = 0 if prompt else T_PROMPT // DEC_SEQ
    ncg = HY_W // cw
    nblk = n // HY_P
    hy_col = sum(IN_SIZES[:6])
    once = pl.Buffered(1)
    seq = lambda row0, col0, b: pl.BlockSpec((n, cw), lambda p, c: (row0 + 2 * p + b, col0 // cw + c), pipeline_mode=once)
    taps = lambda col0: pl.BlockSpec((None, HY_SHORT, cw), lambda p, c: (l, 0, col0 // cw + c))
    spec_blk = lambda q: pl.BlockSpec((None, nblk, 2 * HY_F, cw), lambda p, c: (q, 0, 0, c), pipeline_mode=once)
    const = lambda shape: pl.BlockSpec(shape, lambda p, c: (0, 0))
    return pl.pallas_call(
        partial(_hyena_body, n=n, conv_z=(order == 0)),
        grid=(npair, ncg),
        in_specs=[seq(z_row0, z_col0, 0), seq(z_row0, z_col0, 1), seq(x_row0, x_col0, 0), seq(x_row0, x_col0, 1),
                  taps(0), taps(x_col0 - hy_col),
                  pl.BlockSpec((None, 1, cw), lambda p, c: (l * HY_ORDER + order, 0, c)),
                  spec_blk(2 * order), spec_blk(2 * order + 1),
                  const((2 * HY_F, HY_F)), const((2 * HY_F, HY_F)), const((HY_F, 2 * HY_F)), const((HY_F, 2 * HY_F))],
        out_specs=pl.BlockSpec((2 * n, cw), lambda p, c: (p, c)),
        out_shape=jax.ShapeDtypeStruct((npair * 2 * n, HY_W), F32),
        scratch_shapes=[pltpu.VMEM((2, n, cw), F32), pltpu.VMEM((2, n, cw), F32), pltpu.VMEM((nblk, 2 * HY_F, cw), F32),
                        pltpu.VMEM((2 * HY_F, cw), F32)],
        compiler_params=_cparams("arbitrary", "arbitrary"),
        name=f"hyena_{'prompt' if prompt else 'sample'}_order{order}",
    )(zsrc, zsrc, proj, proj, hy_conv, hy_conv, hy_skip.reshape(DEPTH * HY_ORDER, 1, HY_W), spectra, spectra,
      fd_hi, fd_lo, gi_hi, gi_lo)


def hyena_group(proj, filt, hy_conv, hy_skip, consts, l, prompt):
    fwd = filt[:, :, 0].at[0].add(filt[0, :, 1])
    bwd = filt[:, :, 1].at[0].set(0.0)
    filt4 = jnp.moveaxis(jnp.stack([fwd, bwd], axis=2), 0, 2).reshape(2 * HY_ORDER, filt.shape[0], HY_W)
    spectra = filter_spectra(filt4, consts[2], consts[3])
    hy_col = sum(IN_SIZES[:6])
    z = hyena_order(proj, hy_col, proj, hy_col + HY_W, hy_conv, hy_skip, spectra, consts, l, 0, prompt)
    return hyena_order(z, 0, proj, hy_col + 2 * HY_W, hy_conv, hy_skip, spectra, consts, l, 1, prompt)


def moe_layer(x1, mod, p, l):
    h2, idx, gates, rank, counts = router(x1, p['norm2'], mod, p['router_w'], p['router_bias'], l)
    idx, rank = idx[:, :TOP_K], rank[:, :TOP_K]
    nblk = (counts[0] + TM_E - 1) // TM_E
    blk_end = jnp.cumsum(nblk)
    row_start = (blk_end - nblk) * TM_E
    dest = row_start[idx] + rank
    n_valid = blk_end[-1:].astype(jnp.int32)
    blk = jnp.minimum(jnp.arange(N_BLK_MAX, dtype=jnp.int32), n_valid[0] - 1)
    blk_e = jnp.minimum(jnp.sum(blk_end[None, :] <= blk[:, None], axis=-1), N_EXPERTS - 1).astype(jnp.int32)
    tok = jnp.broadcast_to(jnp.arange(T_ALL, dtype=jnp.int32)[:, None], dest.shape)
    src = jnp.zeros((N_BLK_MAX * TM_E,), jnp.int32).at[dest.reshape(-1)].set(tok.reshape(-1))
    ybuf = routed_experts(h2, src, blk_e, n_valid, p['moe_w_gu'], p['moe_w_dn'], l)
    return shared_combine(h2, p['sh_w_gu'], p['sh_w_dn'], ybuf, dest, gates, x1, mod, l)


def mixer_heads(proj, p, lam_init, ctx):
    B, L, _ = proj.shape
    dq, dk, dv, sq, sk, sv, hy, lx, lg = jnp.split(proj, np.cumsum(IN_SIZES)[:-1].tolist(), axis=-1)
    dq = dq.reshape(B, L, DIFF_HEADS, 2, DIFF_QK) * DIFF_SCALE
    dk = dk.reshape(B, L, DIFF_HEADS, 2, DIFF_QK)
    dv = dv.reshape(B, L, DIFF_HEADS, DIFF_VD)
    sq = sq.reshape(B, L, SWA_KV_HEADS, SWA_GROUP, HEAD_DIM) * SWA_SCALE
    sk = sk.reshape(B, L, SWA_KV_HEADS, HEAD_DIM)
    sv = sv.reshape(B, L, SWA_KV_HEADS, HEAD_DIM)
    lq1, lk1, lq2, lk2 = p['diff_lam'].astype(F32)
    lam = jnp.exp(jnp.sum(lq1 * lk1)) - jnp.exp(jnp.sum(lq2 * lk2)) + lam_init
    sink = p['swa_sink'].reshape(SWA_KV_HEADS, SWA_GROUP)
    if ctx is None:
        d_out = diff_attend(dq, dk, dv, lam)
        s_out = swa_dense(sq, sk, sv, sink)
        h0_f = jnp.zeros((B, LRU_W), F32)
        h0_b = jnp.zeros((B, LRU_W), F32)
    else:
        ck_d, cv_d, ck_s, cv_s, st = ctx
        cos, sin = axial_rope(L)
        dq_r = apply_rope(dq.reshape(B, L, 2 * DIFF_HEADS, DIFF_QK), cos, sin).reshape(dq.shape)
        dk_r = apply_rope(dk.reshape(B, L, 2 * DIFF_HEADS, DIFF_QK), cos, sin).reshape(dk.shape)
        d_out = diff_attend(dq_r, jnp.concatenate([dk_r, ck_d], axis=1), jnp.concatenate([dv, cv_d], axis=1), lam)
        sq_r = apply_rope(sq.reshape(B, L, SWA_HEADS, HEAD_DIM), cos, sin).reshape(sq.shape)
        s_out = swa_banded(sq_r, apply_rope(sk, cos, sin), sv, ck_s, cv_s, sink)
        h0_f, h0_b = st[:, 0], st[:, 1]
    d_out = (rmsnorm(d_out, p['diff_subln']) * (1.0 - lam_init)).reshape(B, L, DIFF_HEADS * DIFF_VD)
    hy_out = hyena_mix(hy, p)
    lru_in = dwconv(lx, p['lru_conv'], LRU_CONV // 2)
    hl, hf_last, hb_first = rglru_bidir(lru_in, p['lru_wa'], p['lru_ba'], p['lru_wx'], p['lru_bx'], p['lru_lam'], h0_f, h0_b)
    lru_out = hl.astype(proj.dtype) * jax.nn.gelu(lg)
    mix = jnp.concatenate([d_out, s_out, hy_out, lru_out], axis=-1).reshape(B * L, D_MIX)
    if ctx is None:
        return mix, (dk, dv, sk, sv, jnp.stack([hf_last, hb_first], axis=1).astype(proj.dtype))
    return mix, None


def kernel(x_prompt, x_sample, cache_diff_k, cache_diff_v, cache_swa_k, cache_swa_v, state_lru, c, c_ctx,
           w_mod, b_mod, norm1, norm2, w_in, w_out, diff_lam, diff_subln, swa_sink, hy_conv, hy_w1, hy_b1,
           hy_w2, hy_b2, hy_w3, hy_freq, hy_log_rate, hy_skip, lru_conv, lru_wa, lru_ba, lru_wx, lru_bx, lru_lam,
           router_w, router_bias, moe_w_gu, moe_w_dn, sh_w_gu, sh_w_dn, final_norm):
    x = jnp.concatenate([x_prompt.reshape(T_PROMPT, D_MODEL), x_sample.reshape(T_SAMPLE, D_MODEL)], axis=0)
    cond = jnp.concatenate([c_ctx[None, :], c, jnp.zeros((8 - N_COND, D_MODEL), F32)], axis=0)
    dk_l, dv_l, sk_l, sv_l, st_l = [], [], [], [], []
    tables = rope_tables(DEC_SEQ)
    consts = dft_constants()
    h0_prompt = jnp.zeros((BATCH, 1, 2, LRU_W), F32)
    col = np.cumsum((0,) + IN_SIZES).tolist()
    empty = lambda: jnp.zeros((T_ALL, GROUP_W), F32)
    for l in range(DEPTH):
        p = {'hy_conv': hy_conv[l], 'hy_w1': hy_w1[l], 'hy_b1': hy_b1[l], 'hy_w2': hy_w2[l], 'hy_b2': hy_b2[l],
             'hy_w3': hy_w3[l], 'hy_freq': hy_freq[l], 'hy_log_rate': hy_log_rate[l], 'hy_skip': hy_skip[l],
             'norm2': norm2, 'router_w': router_w, 'router_bias': router_bias, 'moe_w_gu': moe_w_gu,
             'moe_w_dn': moe_w_dn, 'sh_w_gu': sh_w_gu, 'sh_w_dn': sh_w_dn}
        lam_init = 0.8 - 0.6 * math.exp(-0.3 * l)
        mod = modulation(cond, w_mod, b_mod, l)[:N_COND].reshape(N_COND * 6, 1, D_MODEL)
        proj = in_proj(x, norm1, mod, w_in, l)
        proj_p = proj[:T_PROMPT].reshape(BATCH, SEQ, D_IN)
        dk_l.append(proj_p[..., col[1]:col[2]].reshape(BATCH, SEQ, DIFF_HEADS, 2, DIFF_QK))
        dv_l.append(proj_p[..., col[2]:col[3]].reshape(BATCH, SEQ, DIFF_HEADS, DIFF_VD))
        sk_l.append(proj_p[..., col[4]:col[5]].reshape(BATCH, SEQ, SWA_KV_HEADS, HEAD_DIM))
        sv_l.append(proj_p[..., col[5]:col[6]].reshape(BATCH, SEQ, SWA_KV_HEADS, HEAD_DIM))
        d_out = diff_attention(proj, empty(), diff_lam, diff_subln, l, lam_init)
        d_out = diff_attention(proj, d_out, diff_lam, diff_subln, l, lam_init, cache_diff_k, cache_diff_v, tables)
        s_out = swa_attention(proj, empty(), swa_sink, l)
        s_out = swa_attention(proj, s_out, swa_sink, l, cache_swa_k, cache_swa_v, tables)
        hy_args = (p['hy_w1'], p['hy_b1'], p['hy_w2'], p['hy_b2'], p['hy_w3'], p['hy_freq'], p['hy_log_rate'])
        hy_out = jnp.concatenate(
            [hyena_group(proj, hyena_filters(SEQ, *hy_args), hy_conv, hy_skip, consts, l, True),
             hyena_group(proj, hyena_filters(DEC_SEQ, *hy_args), hy_conv, hy_skip, consts, l, False)], axis=0)
        gate_w, gate_b = lru_gate_params(lru_wa[l], lru_ba[l], lru_wx[l], lru_bx[l])
        lru_out, st = rglru(proj, empty(), gate_w, gate_b, lru_conv, lru_lam, h0_prompt, l, 0, True)
        lru_out, _ = rglru(proj, lru_out, gate_w, gate_b, lru_conv, lru_lam, state_lru, l, l, False)
        st_l.append(st)
        x1 = out_proj([d_out, s_out, hy_out, lru_out], w_out, x, mod, l)
        x = moe_layer(x1, mod, p, l)
    y = final_rmsnorm(x, final_norm)
    y_prompt = y[:T_PROMPT].reshape(BATCH, SEQ, D_MODEL)
    y_sample = y[T_PROMPT:].reshape(DEC_BATCH, DEC_SEQ, D_MODEL)
    return (y_prompt, y_sample, jnp.stack(dk_l, axis=1), jnp.stack(dv_l, axis=1), jnp.stack(sk_l, axis=1),
            jnp.stack(sv_l, axis=1), jnp.stack(st_l, axis=1))
```

```python
import math
from functools import partial
import jax, jax.numpy as jnp
from jax import lax
import numpy as np
from jax.experimental import pallas as pl
from jax.experimental.pallas import tpu as pltpu

D_MODEL = 2048
BATCH = 16
SEQ = 256
DEPTH = 2
DEC_BATCH = 2
DEC_SEQ = 4096
PAST_LEN = 256

GRID_W = 64
EPS = 1e-6
F32 = jnp.float32
GROUP_W = D_MODEL // 4
D_MIX = 4 * GROUP_W
DIFF_QK = 64
DIFF_VD = 2 * DIFF_QK
DIFF_HEADS = GROUP_W // DIFF_VD
DIFF_SCALE = DIFF_QK ** -0.5
HEAD_DIM = 64
SWA_HEADS = GROUP_W // HEAD_DIM
SWA_KV_HEADS = SWA_HEADS // 4
SWA_GROUP = SWA_HEADS // SWA_KV_HEADS
SWA_SCALE = HEAD_DIM ** -0.5
WINDOW = 128
ROPE_PAIRS = HEAD_DIM // 4
ROPE_BASE = 10000.0
HY_W = GROUP_W
HY_ORDER = 2
HY_SHORT = 3
HY_BANDS = 16
HY_POS_DIM = 1 + 2 * HY_BANDS
HY_HID = 64
LRU_W = GROUP_W
LRU_BLOCKS = 8
LRU_BD = LRU_W // LRU_BLOCKS
LRU_CONV = 4
LRU_C = 8.0
N_EXPERTS = 64
TOP_K = 6
EXPERT_FF = D_MODEL // 4
SHARED_FF = EXPERT_FF
ROUTE_SCALE = 2.5
NEG_INF = -1e30
IN_SIZES = (2 * DIFF_HEADS * DIFF_QK, 2 * DIFF_HEADS * DIFF_QK, DIFF_HEADS * DIFF_VD, SWA_HEADS * HEAD_DIM, SWA_KV_HEADS * HEAD_DIM, SWA_KV_HEADS * HEAD_DIM, (HY_ORDER + 1) * HY_W, LRU_W, LRU_W)
D_IN = sum(IN_SIZES)


BF16 = jnp.bfloat16
T_PROMPT = BATCH * SEQ
T_SAMPLE = DEC_BATCH * DEC_SEQ
T_ALL = T_PROMPT + T_SAMPLE
N_COND = 1 + DEC_BATCH
TM = 1024
TM_IN = 2048
TM_SH = 256
TN_IN = 256
TN_OUT = 512
TN_MOD = 1024
TM_E = 256
N_ASSIGN = T_ALL * TOP_K
N_BLK_MAX = N_ASSIGN // TM_E + N_EXPERTS
LANES = 128
VMEM_LIMIT = 56 * 1024 * 1024
VMEM_LIMIT_IN_PROJ = 60 * 1024 * 1024


def _cparams(*sem):
    return pltpu.CompilerParams(dimension_semantics=sem, vmem_limit_bytes=VMEM_LIMIT)


def _cond_row(i, tm):
    return jnp.where(i < T_PROMPT // tm, 0, 1 + (i - T_PROMPT // tm) // (DEC_SEQ // tm))


def _mod_spec(which, tm, tn=D_MODEL):
    if tn == D_MODEL:
        return pl.BlockSpec((1, 1, D_MODEL), lambda i, *_: (_cond_row(i, tm) * 6 + which, 0, 0))
    return pl.BlockSpec((1, 1, tn), lambda i, j: (_cond_row(i, tm) * 6 + which, 0, j))


def _mod_body(c_ref, *rest):
    w_refs, (b_ref, o_ref) = rest[:W_STREAMS], rest[W_STREAMS:]
    c = c_ref[...]
    a = (c * jax.nn.sigmoid(c)).astype(BF16)
    o_ref[...] = _dot_k_slabs(a, w_refs) + b_ref[0]


def modulation(cond, w_mod, b_mod, l):
    n = 6 * D_MODEL
    return pl.pallas_call(
        _mod_body,
        grid=(n // TN_MOD,),
        in_specs=[pl.BlockSpec((8, D_MODEL), lambda j: (0, 0))]
                 + [pl.BlockSpec((None, D_MODEL // W_STREAMS, TN_MOD), lambda j, s=s: (l, s, j)) for s in range(W_STREAMS)]
                 + [pl.BlockSpec((None, 1, TN_MOD), lambda j: (l, 0, j))],
        out_specs=pl.BlockSpec((8, TN_MOD), lambda j: (0, j)),
        out_shape=jax.ShapeDtypeStruct((8, n), F32),
        compiler_params=_cparams("arbitrary"),
        name="modulation",
    )(cond, *([w_mod] * W_STREAMS), b_mod.reshape(DEPTH, 1, n))


def _norm_mod(x, g, sc, sh):
    y = x * lax.rsqrt(jnp.mean(x * x, axis=-1, keepdims=True) + EPS) * g
    return y * (1.0 + sc) + sh


W_STREAMS = 4


def _k_slab_specs(rows, cols, index):
    return [pl.BlockSpec((None, rows // W_STREAMS, cols), lambda i, j, s=s: (index(i, j)[0], s, index(i, j)[1]))
            for s in range(W_STREAMS)]


def _dot_k_slabs(a, w_refs):
    k = a.shape[1] // len(w_refs)
    acc = jnp.dot(a[:, 0:k], w_refs[0][...].astype(BF16), preferred_element_type=F32)
    for s in range(1, len(w_refs)):
        acc = acc + jnp.dot(a[:, s * k:(s + 1) * k], w_refs[s][...].astype(BF16), preferred_element_type=F32)
    return acc


def _in_proj_body(x_ref, g_ref, sc_ref, sh_ref, *rest):
    w_refs, (o_ref, hb_ref) = rest[:W_STREAMS], rest[W_STREAMS:]

    @pl.when(pl.program_id(1) == 0)
    def _():
        for r in range(0, TM_IN, TM_IN // 4):
            rows = slice(r, r + TM_IN // 4)
            hb_ref[rows, :] = _norm_mod(x_ref[rows, :], g_ref[0], sc_ref[0], sh_ref[0]).astype(BF16)

    o_ref[...] = _dot_k_slabs(hb_ref[...], w_refs)


def in_proj(x, norm1, mod, w_in, l):
    return pl.pallas_call(
        _in_proj_body,
        grid=(T_ALL // TM_IN, D_IN // TN_IN),
        in_specs=[pl.BlockSpec((TM_IN, D_MODEL), lambda i, j: (i, 0)),
                  pl.BlockSpec((None, 1, D_MODEL), lambda i, j: (l, 0, 0)),
                  _mod_spec(1, TM_IN), _mod_spec(0, TM_IN)] + _k_slab_specs(D_MODEL, TN_IN, lambda i, j: (l, j)),
        out_specs=pl.BlockSpec((TM_IN, TN_IN), lambda i, j: (i, j)),
        out_shape=jax.ShapeDtypeStruct((T_ALL, D_IN), F32),
        scratch_shapes=[pltpu.VMEM((TM_IN, D_MODEL), BF16)],
        compiler_params=pltpu.CompilerParams(dimension_semantics=("arbitrary", "arbitrary"),
                                             vmem_limit_bytes=VMEM_LIMIT_IN_PROJ),
        name="in_proj",
    )(x, norm1.reshape(DEPTH, 1, D_MODEL), mod, mod, *([w_in] * W_STREAMS))


N_MIX = D_MIX // GROUP_W


def _out_proj_body(*refs):
    m_refs, (w_ref, x_ref, g1_ref, o_ref, mb_ref) = refs[:N_MIX], refs[N_MIX:]

    @pl.when(pl.program_id(1) == 0)
    def _():
        for g in range(N_MIX):
            mb_ref[g] = m_refs[g][...].astype(BF16)

    acc = jnp.dot(mb_ref[0], w_ref[0:GROUP_W, :].astype(BF16), preferred_element_type=F32)
    for g in range(1, N_MIX):
        acc = acc + jnp.dot(mb_ref[g], w_ref[g * GROUP_W:(g + 1) * GROUP_W, :].astype(BF16), preferred_element_type=F32)
    o_ref[...] = x_ref[...] + g1_ref[0] * acc


def out_proj(mixes, w_out, x, mod, l):
    return pl.pallas_call(
        _out_proj_body,
        grid=(T_ALL // TM, D_MODEL // TN_OUT),
        in_specs=[pl.BlockSpec((TM, GROUP_W), lambda i, j: (i, 0))] * N_MIX + [
                  pl.BlockSpec((None, D_MIX, TN_OUT), lambda i, j: (l, 0, j)),
                  pl.BlockSpec((TM, TN_OUT), lambda i, j: (i, j)),
                  _mod_spec(2, TM, TN_OUT)],
        out_specs=pl.BlockSpec((TM, TN_OUT), lambda i, j: (i, j)),
        out_shape=jax.ShapeDtypeStruct((T_ALL, D_MODEL), F32),
        scratch_shapes=[pltpu.VMEM((N_MIX, TM, GROUP_W), BF16)],
        compiler_params=_cparams("arbitrary", "arbitrary"),
        name="out_proj",
    )(*mixes, w_out, x, mod)


def _router_body(x_ref, g_ref, sc_ref, sh_ref, wr_ref, rb_ref, h_ref, idx_ref, gate_ref, rank_ref, cnt_ref, carry_ref):
    i = pl.program_id(0)

    @pl.when(i == 0)
    def _():
        carry_ref[...] = jnp.zeros_like(carry_ref)

    h = _norm_mod(x_ref[...], g_ref[0], sc_ref[0], sh_ref[0])
    h_ref[...] = h
    h_hi, h_lo = _split_bf16(h)
    w_hi, w_lo = _split_bf16(wr_ref[...])
    logits = (jnp.dot(h_hi, w_hi, preferred_element_type=F32) + jnp.dot(h_lo, w_hi, preferred_element_type=F32)
              + jnp.dot(h_hi, w_lo, preferred_element_type=F32))
    s = jax.nn.sigmoid(logits)
    cur = s + rb_ref[0]
    e_iota = lax.broadcasted_iota(jnp.int32, s.shape, 1).astype(F32)
    lane = lax.broadcasted_iota(jnp.int32, (TM, LANES), 1)
    r_iota = lax.broadcasted_iota(jnp.int32, (TM, TM), 0)
    c_iota = lax.broadcasted_iota(jnp.int32, (TM, TM), 1)
    tri = (c_iota <= r_iota).astype(BF16)
    idx_out = jnp.zeros((TM, LANES), F32)
    gate_out = jnp.zeros((TM, LANES), F32)
    rank_out = jnp.zeros((TM, LANES), F32)
    gsum = jnp.zeros((TM, 1), F32)
    carry = carry_ref[...]
    for k in range(TOP_K):
        m = jnp.max(cur, axis=-1, keepdims=True)
        ek = jnp.min(jnp.where(cur == m, e_iota, float(N_EXPERTS)), axis=-1, keepdims=True)
        hit = e_iota == ek
        gk = jnp.sum(jnp.where(hit, s, 0.0), axis=-1, keepdims=True)
        cur = jnp.where(hit, -jnp.inf, cur)
        onehot = hit.astype(BF16)
        cum = jnp.dot(tri, onehot, preferred_element_type=F32)
        rk = jnp.sum(jnp.where(hit, carry + cum, 0.0), axis=-1, keepdims=True) - 1.0
        carry = carry + cum[TM - 1:TM, :]
        gsum = gsum + gk
        idx_out = jnp.where(lane == k, ek, idx_out)
        gate_out = jnp.where(lane == k, gk, gate_out)
        rank_out = jnp.where(lane == k, rk, rank_out)
    carry_ref[...] = carry
    idx_ref[...] = idx_out.astype(jnp.int32)
    gate_ref[...] = ROUTE_SCALE * gate_out / gsum
    rank_ref[...] = rank_out.astype(jnp.int32)
    cnt_ref[...] = carry.astype(jnp.int32)


def router(x1, norm2, mod, router_w, router_bias, l):
    tok_out = lambda dt: jax.ShapeDtypeStruct((T_ALL, LANES), dt)
    tok_spec = pl.BlockSpec((TM, LANES), lambda i: (i, 0))
    return pl.pallas_call(
        _router_body,
        grid=(T_ALL // TM,),
        in_specs=[pl.BlockSpec((TM, D_MODEL), lambda i: (i, 0)),
                  pl.BlockSpec((None, 1, D_MODEL), lambda i: (l, 0, 0)),
                  _mod_spec(4, TM), _mod_spec(3, TM),
                  pl.BlockSpec((None, D_MODEL, N_EXPERTS), lambda i: (l, 0, 0)),
                  pl.BlockSpec((None, 1, N_EXPERTS), lambda i: (l, 0, 0))],
        out_specs=[pl.BlockSpec((TM, D_MODEL), lambda i: (i, 0)), tok_spec, tok_spec, tok_spec,
                   pl.BlockSpec((1, N_EXPERTS), lambda i: (0, 0))],
        out_shape=[jax.ShapeDtypeStruct((T_ALL, D_MODEL), F32), tok_out(jnp.int32), tok_out(F32), tok_out(jnp.int32),
                   jax.ShapeDtypeStruct((1, N_EXPERTS), jnp.int32)],
        scratch_shapes=[pltpu.VMEM((1, N_EXPERTS), F32)],
        compiler_params=_cparams("arbitrary"),
        name="router",
    )(x1, norm2.reshape(DEPTH, 1, D_MODEL), mod, mod, router_w, router_bias.reshape(DEPTH, 1, N_EXPERTS))


def _experts_body(be_ref, nv_ref, src_ref, nxt_ref, h_hbm, *rest):
    wgu_refs, wdn_refs = rest[:W_STREAMS], rest[W_STREAMS:2 * W_STREAMS]
    y_ref, wgu_b, wdn_b, xg0, xg1, sems = rest[2 * W_STREAMS:]
    i = pl.program_id(0)
    n_valid = nv_ref[0]

    def row_copy(idx_ref, t, buf, sem):
        return pltpu.make_async_copy(h_hbm.at[pl.ds(idx_ref[0, t], 1)], buf.at[pl.ds(t, 1)], sem)

    def all_rows(buf, sem):
        return pltpu.make_async_copy(h_hbm.at[pl.ds(0, TM_E)], buf, sem)

    @pl.when(i == 0)
    def _():
        def issue(t, carry):
            row_copy(src_ref, t, xg0, sems.at[0]).start()
            return carry

        lax.fori_loop(0, TM_E, issue, 0, unroll=8)

    @pl.when(i < n_valid)
    def _():
        prev = be_ref[jnp.maximum(i - 1, 0)]

        @pl.when(jnp.logical_or(i == 0, be_ref[i] != prev))
        def _():
            kg, kd = D_MODEL // W_STREAMS, EXPERT_FF // W_STREAMS
            for s in range(W_STREAMS):
                wgu_b[s * kg:(s + 1) * kg, :] = wgu_refs[s][...].astype(BF16)
                wdn_b[s * kd:(s + 1) * kd, :] = wdn_refs[s][...].astype(BF16)

        for parity, (cur, nxt) in enumerate(((xg0, xg1), (xg1, xg0))):
            @pl.when(i % 2 == parity)
            def _():
                all_rows(cur, sems.at[parity]).wait()
                for t in range(TM_E):
                    row_copy(nxt_ref, t, nxt, sems.at[1 - parity]).start()
                hmid = jnp.dot(cur[...].astype(BF16), wgu_b[...], preferred_element_type=F32)
                a, b = hmid[:, :EXPERT_FF], hmid[:, EXPERT_FF:]
                act = (a * jax.nn.sigmoid(a) * b).astype(BF16)
                y_ref[...] = jnp.dot(act, wdn_b[...], preferred_element_type=F32)

                @pl.when(i == n_valid - 1)
                def _():
                    all_rows(nxt, sems.at[1 - parity]).wait()

    @pl.when(i >= n_valid)
    def _():
        y_ref[...] = jnp.zeros_like(y_ref)


def routed_experts(h2, src, blk_e, n_valid, moe_w_gu, moe_w_dn, l):
    idx_spec = lambda off: pl.BlockSpec((None, 1, TM_E), lambda i, be, nv: (jnp.minimum(i + off, nv[0] - 1), 0, 0),
                                        memory_space=pltpu.SMEM)
    return pl.pallas_call(
        _experts_body,
        grid_spec=pltpu.PrefetchScalarGridSpec(
            num_scalar_prefetch=2,
            grid=(N_BLK_MAX,),
            in_specs=[idx_spec(0), idx_spec(1),
                      pl.BlockSpec(memory_space=pl.ANY)]
                     + [pl.BlockSpec((None, None, D_MODEL // W_STREAMS, 2 * EXPERT_FF),
                                     lambda i, be, nv, s=s: (l, be[i], s, 0)) for s in range(W_STREAMS)]
                     + [pl.BlockSpec((None, None, EXPERT_FF // W_STREAMS, D_MODEL),
                                     lambda i, be, nv, s=s: (l, be[i], s, 0)) for s in range(W_STREAMS)],
            out_specs=pl.BlockSpec((TM_E, D_MODEL), lambda i, be, nv: (i, 0)),
            scratch_shapes=[pltpu.VMEM((D_MODEL, 2 * EXPERT_FF), BF16), pltpu.VMEM((EXPERT_FF, D_MODEL), BF16),
                            pltpu.VMEM((TM_E, D_MODEL), F32), pltpu.VMEM((TM_E, D_MODEL), F32),
                            pltpu.SemaphoreType.DMA((2,))]),
        out_shape=jax.ShapeDtypeStruct((N_BLK_MAX * TM_E, D_MODEL), F32),
        compiler_params=_cparams("arbitrary"),
        name="routed_experts",
    )(blk_e, n_valid, src.reshape(N_BLK_MAX, 1, TM_E), src.reshape(N_BLK_MAX, 1, TM_E), h2,
      *([moe_w_gu] * W_STREAMS), *([moe_w_dn] * W_STREAMS))


def _gathered_rows(ybuf_hbm, rows, sem, n):
    return pltpu.make_async_copy(ybuf_hbm.at[pl.ds(0, n)], rows.at[pl.ds(0, n)], sem)


def _shared_body(dest_ref, gate_ref, h_ref, wgu_ref, wdn_ref, x_ref, g2_ref, ybuf_hbm, o_ref, wgu_b, wdn_b, rows, sem):
    @pl.when(pl.program_id(0) == 0)
    def _():
        wgu_b[...] = wgu_ref[...].astype(BF16)
        wdn_b[...] = wdn_ref[...].astype(BF16)

    for t in range(TM_SH):
        for k in range(TOP_K):
            src = dest_ref[0, t * TOP_K + k]
            pltpu.make_async_copy(ybuf_hbm.at[pl.ds(src, 1)], rows.at[pl.ds(k * TM_SH + t, 1)], sem).start()
    hmid = jnp.dot(h_ref[...].astype(BF16), wgu_b[...], preferred_element_type=F32)
    a, b = hmid[:, :SHARED_FF], hmid[:, SHARED_FF:]
    act = (a * jax.nn.sigmoid(a) * b).astype(BF16)
    y = jnp.dot(act, wdn_b[...], preferred_element_type=F32)
    _gathered_rows(ybuf_hbm, rows, sem, TOP_K * TM_SH).wait()
    gates = gate_ref[...]
    for k in range(TOP_K):
        y = y + gates[:, k:k + 1] * rows[k * TM_SH:(k + 1) * TM_SH, :]
    o_ref[...] = x_ref[...] + g2_ref[0] * y


def shared_combine(h2, sh_w_gu, sh_w_dn, ybuf, dest, gates, x1, mod, l):
    tile = pl.BlockSpec((TM_SH, D_MODEL), lambda i: (i, 0))
    once = pl.Buffered(1)
    n_tiles = T_ALL // TM_SH
    return pl.pallas_call(
        _shared_body,
        grid=(n_tiles,),
        in_specs=[pl.BlockSpec((None, 1, TM_SH * TOP_K), lambda i: (i, 0, 0), memory_space=pltpu.SMEM),
                  pl.BlockSpec((TM_SH, LANES), lambda i: (i, 0)),
                  tile,
                  pl.BlockSpec((None, D_MODEL, 2 * SHARED_FF), lambda i: (l, 0, 0), pipeline_mode=once),
                  pl.BlockSpec((None, SHARED_FF, D_MODEL), lambda i: (l, 0, 0), pipeline_mode=once),
                  tile, _mod_spec(5, TM_SH),
                  pl.BlockSpec(memory_space=pl.ANY)],
        out_specs=tile,
        out_shape=jax.ShapeDtypeStruct((T_ALL, D_MODEL), F32),
        scratch_shapes=[pltpu.VMEM((D_MODEL, 2 * SHARED_FF), BF16), pltpu.VMEM((SHARED_FF, D_MODEL), BF16),
                        pltpu.VMEM((TOP_K * TM_SH, D_MODEL), F32), pltpu.SemaphoreType.DMA(())],
        compiler_params=_cparams("arbitrary"),
        name="shared_combine",
    )(dest.reshape(n_tiles, 1, TM_SH * TOP_K), gates, h2, sh_w_gu, sh_w_dn, x1, mod, ybuf)


def _final_norm_body(x_ref, g_ref, o_ref):
    x = x_ref[...]
    o_ref[...] = x * lax.rsqrt(jnp.mean(x * x, axis=-1, keepdims=True) + EPS) * g_ref[...]


def final_rmsnorm(x, g):
    return pl.pallas_call(
        _final_norm_body,
        grid=(T_ALL // TM,),
        in_specs=[pl.BlockSpec((TM, D_MODEL), lambda i: (i, 0)), pl.BlockSpec((1, D_MODEL), lambda i: (0, 0))],
        out_specs=pl.BlockSpec((TM, D_MODEL), lambda i: (i, 0)),
        out_shape=jax.ShapeDtypeStruct(x.shape, x.dtype),
        compiler_params=_cparams("arbitrary"),
        name="final_norm",
    )(x, g.reshape(1, D_MODEL))


TQ_DIFF = 256
HALF = LANES // 2
NT_DIMS = (((1,), (1,)), ((), ()))


def rope_tables(L):
    pos = jnp.arange(L)
    d = jnp.arange(LANES) % HEAD_DIM
    p = jnp.where(d // (2 * ROPE_PAIRS) == 0, (pos // GRID_W)[:, None], (pos % GRID_W)[:, None]).astype(F32)
    inv = ROPE_BASE ** (-(d % ROPE_PAIRS).astype(F32) / ROPE_PAIRS)
    ang = p * inv
    return jnp.cos(ang), jnp.where((d // ROPE_PAIRS) % 2 == 0, -jnp.sin(ang), jnp.sin(ang))


def _rope(x, cos, sin_signed):
    lane = lax.broadcasted_iota(jnp.int32, x.shape, 1)
    is_x1 = (lane // ROPE_PAIRS) % 2 == 0
    partner = jnp.where(is_x1, pltpu.roll(x, LANES - ROPE_PAIRS, 1), pltpu.roll(x, ROPE_PAIRS, 1))
    return x * cos + partner * sin_signed


def _exp_and_inv_sum(s):
    e = jnp.exp(s - jnp.max(s, axis=-1, keepdims=True))
    return e, 1.0 / jnp.sum(e, axis=-1, keepdims=True)


def _diff_body(*refs, lk, ctx, lam_init):
    *ins, _aliased_dst, o_ref, kb, vb = refs
    if ctx:
        q_ref, k_ref, v_ref, dl_ref, g_ref, ck_ref, cv_ref, cq_ref, sq_ref, ckk_ref, skk_ref = ins
    else:
        q_ref, k_ref, v_ref, dl_ref, g_ref = ins

    @pl.when(pl.program_id(2) == 0)
    def _():
        k = k_ref[...]
        if ctx:
            k = _rope(k, ckk_ref[...], skk_ref[...])
            kb[lk:, :] = ck_ref[...].astype(BF16)
            vb[lk:, :] = cv_ref[...].astype(BF16)
        kb[0:lk, :] = k.astype(BF16)
        vb[0:lk, :] = v_ref[...].astype(BF16)

    q = q_ref[...] * DIFF_SCALE
    if ctx:
        q = _rope(q, cq_ref[...], sq_ref[...])
    lane = lax.broadcasted_iota(jnp.int32, q.shape, 1)
    q1 = jnp.where(lane < HALF, q, 0.0).astype(BF16)
    q2 = jnp.where(lane >= HALF, q, 0.0).astype(BF16)
    keys = kb[...]
    e1, inv1 = _exp_and_inv_sum(lax.dot_general(q1, keys, NT_DIMS, preferred_element_type=F32))
    e2, inv2 = _exp_and_inv_sum(lax.dot_general(q2, keys, NT_DIMS, preferred_element_type=F32))
    dl = dl_ref[...]
    lam = (jnp.exp(jnp.sum(dl[0:1] * dl[1:2], axis=-1, keepdims=True))
           - jnp.exp(jnp.sum(dl[2:3] * dl[3:4], axis=-1, keepdims=True)) + lam_init)
    o = jnp.dot((e1 * inv1 - e2 * (lam * inv2)).astype(BF16), vb[...], preferred_element_type=F32)
    o = o * lax.rsqrt(jnp.mean(o * o, axis=-1, keepdims=True) + EPS) * g_ref[...]
    o_ref[...] = o * (1.0 - lam_init)


def diff_attention(proj, dst, diff_lam, diff_subln, l, lam_init, cache_k=None, cache_v=None, tables=None):
    ctx = cache_k is not None
    nh = DIFF_HEADS
    if ctx:
        nb, lk, tq, row0 = DEC_BATCH, DEC_SEQ, TQ_DIFF, T_PROMPT
    else:
        nb, lk, tq, row0 = BATCH, SEQ, SEQ, 0
    in_specs = [pl.BlockSpec((tq, LANES), lambda b, h, qi: ((row0 + b * lk) // tq + qi, h)),
                pl.BlockSpec((lk, LANES), lambda b, h, qi: (row0 // lk + b, nh + h)),
                pl.BlockSpec((lk, LANES), lambda b, h, qi: (row0 // lk + b, 2 * nh + h)),
                pl.BlockSpec((None, 4, DIFF_QK), lambda b, h, qi: (l, 0, 0)),
                pl.BlockSpec((None, 1, DIFF_VD), lambda b, h, qi: (l, 0, 0))]
    args = [proj, proj, proj, diff_lam, diff_subln.reshape(DEPTH, 1, DIFF_VD)]
    if ctx:
        cos, sin = tables
        ctx_spec = pl.BlockSpec((None, None, PAST_LEN, LANES), lambda b, h, qi: (b, l, 0, h))
        q_tab = pl.BlockSpec((tq, LANES), lambda b, h, qi: (qi, 0))
        k_tab = pl.BlockSpec((lk, LANES), lambda b, h, qi: (0, 0))
        in_specs += [ctx_spec, ctx_spec, q_tab, q_tab, k_tab, k_tab]
        args += [cache_k.reshape(DEC_BATCH, DEPTH, PAST_LEN, GROUP_W), cache_v.reshape(DEC_BATCH, DEPTH, PAST_LEN, GROUP_W),
                 cos, sin, cos, sin]
    n_keys = lk + (PAST_LEN if ctx else 0)
    return pl.pallas_call(
        partial(_diff_body, lk=lk, ctx=ctx, lam_init=lam_init),
        grid=(nb, nh, lk // tq),
        in_specs=in_specs + [pl.BlockSpec(memory_space=pl.ANY)],
        out_specs=pl.BlockSpec((tq, LANES), lambda b, h, qi: ((row0 + b * lk) // tq + qi, h)),
        out_shape=jax.ShapeDtypeStruct((T_ALL, GROUP_W), F32),
        input_output_aliases={len(args): 0},
        scratch_shapes=[pltpu.VMEM((n_keys, LANES), BF16), pltpu.VMEM((n_keys, LANES), BF16)],
        compiler_params=_cparams("arbitrary", "arbitrary", "arbitrary"),
        name="diff_attention_ctx" if ctx else "diff_attention",
    )(*args, dst)


def _swa_body(*refs, lk, tq, banded):
    *ins, _aliased_dst, o_ref, kb, vb = refs
    if banded:
        q_ref, k_ref, v_ref, sk_ref, ck_ref, cv_ref, cq_ref, sq_ref, ckk_ref, skk_ref = ins
    else:
        q_ref, k_ref, v_ref, sk_ref = ins
    qi = pl.program_id(1)

    @pl.when(qi == 0)
    def _():
        k = k_ref[...]
        if banded:
            k = _rope(k, ckk_ref[...], skk_ref[...])
            zeros = jnp.zeros((WINDOW, LANES), BF16)
            kb[0:WINDOW, :] = zeros
            vb[0:WINDOW, :] = zeros
            kb[WINDOW + lk:2 * WINDOW + lk, :] = zeros
            vb[WINDOW + lk:2 * WINDOW + lk, :] = zeros
            kb[2 * WINDOW + lk:, :] = ck_ref[...].astype(BF16)
            vb[2 * WINDOW + lk:, :] = cv_ref[...].astype(BF16)
            kb[WINDOW:WINDOW + lk, :] = k.astype(BF16)
            vb[WINDOW:WINDOW + lk, :] = v_ref[...].astype(BF16)
        else:
            kb[...] = k.astype(BF16)
            vb[...] = v_ref[...].astype(BF16)

    lane = lax.broadcasted_iota(jnp.int32, (tq, LANES), 1)
    sinks = sk_ref[...]
    if banded:
        start = pl.multiple_of(qi * WINDOW, WINDOW)
        k_loc, v_loc = kb[pl.ds(start, 3 * WINDOW), :], vb[pl.ds(start, 3 * WINDOW), :]
        k_ctx, v_ctx = kb[2 * WINDOW + lk:, :], vb[2 * WINDOW + lk:, :]
        row = lax.broadcasted_iota(jnp.int32, (SWA_GROUP * tq, 3 * WINDOW), 0) % tq
        col = lax.broadcasted_iota(jnp.int32, (SWA_GROUP * tq, 3 * WINDOW), 1)
        kpos = (qi - 1) * WINDOW + col
        visible = (col >= row) & (col - row <= 2 * WINDOW) & (kpos >= 0) & (kpos < lk)
    else:
        k_loc, v_loc = kb[...], vb[...]
    outs = []
    for kh in range(SWA_KV_HEADS):
        in_half = (lane >= HALF) if kh else (lane < HALF)
        qs, sink_rows = [], []
        for g in range(SWA_GROUP):
            head = kh * SWA_GROUP + g
            t = q_ref[:, (head // 2) * LANES:(head // 2 + 1) * LANES] * SWA_SCALE
            if banded:
                t = _rope(t, cq_ref[...], sq_ref[...])
            if head % 2 != kh:
                t = pltpu.roll(t, HALF, 1)
            qs.append(jnp.where(in_half, t, 0.0).astype(BF16))
            sink_rows.append(jnp.broadcast_to(sinks[:, head:head + 1], (tq, 1)))
        qs = jnp.concatenate(qs, axis=0)
        sink = jnp.concatenate(sink_rows, axis=0)
        s_loc = lax.dot_general(qs, k_loc, NT_DIMS, preferred_element_type=F32)
        m = sink
        if banded:
            s_loc = jnp.where(visible, s_loc, NEG_INF)
            s_ctx = lax.dot_general(qs, k_ctx, NT_DIMS, preferred_element_type=F32)
            m = jnp.maximum(m, jnp.max(s_ctx, axis=-1, keepdims=True))
        m = jnp.maximum(m, jnp.max(s_loc, axis=-1, keepdims=True))
        e_loc = jnp.exp(s_loc - m)
        den = jnp.sum(e_loc, axis=-1, keepdims=True) + jnp.exp(sink - m)
        if banded:
            e_ctx = jnp.exp(s_ctx - m)
            den = den + jnp.sum(e_ctx, axis=-1, keepdims=True)
        inv = 1.0 / den
        o = jnp.dot((e_loc * inv).astype(BF16), v_loc, preferred_element_type=F32)
        if banded:
            o = o + jnp.dot((e_ctx * inv).astype(BF16), v_ctx, preferred_element_type=F32)
        outs.append(o)
    for t in range(SWA_HEADS // 2):
        kh, g0 = (2 * t) // SWA_GROUP, (2 * t) % SWA_GROUP
        a = outs[kh][g0 * tq:(g0 + 1) * tq]
        b = outs[kh][(g0 + 1) * tq:(g0 + 2) * tq]
        a = pltpu.roll(a, HALF, 1) if kh == 1 else a
        b = pltpu.roll(b, HALF, 1) if kh == 0 else b
        o_ref[:, t * LANES:(t + 1) * LANES] = jnp.where(lane < HALF, a, b)


def swa_attention(proj, dst, swa_sink, l, cache_k=None, cache_v=None, tables=None):
    banded = cache_k is not None
    q_col, k_col = 3, (4 * GROUP_W) // LANES
    if banded:
        nb, lk, tq, row0 = DEC_BATCH, DEC_SEQ, WINDOW, T_PROMPT
    else:
        nb, lk, tq, row0 = BATCH, SEQ, SEQ, 0
    in_specs = [pl.BlockSpec((tq, GROUP_W), lambda b, qi: ((row0 + b * lk) // tq + qi, q_col)),
                pl.BlockSpec((lk, LANES), lambda b, qi: (row0 // lk + b, k_col)),
                pl.BlockSpec((lk, LANES), lambda b, qi: (row0 // lk + b, k_col + 1)),
                pl.BlockSpec((None, 1, SWA_HEADS), lambda b, qi: (l, 0, 0))]
    args = [proj, proj, proj, swa_sink.reshape(DEPTH, 1, SWA_HEADS)]
    if banded:
        cos, sin = tables
        ctx_spec = pl.BlockSpec((None, None, PAST_LEN, LANES), lambda b, qi: (b, l, 0, 0))
        q_tab = pl.BlockSpec((tq, LANES), lambda b, qi: (qi, 0))
        k_tab = pl.BlockSpec((lk, LANES), lambda b, qi: (0, 0))
        in_specs += [ctx_spec, ctx_spec, q_tab, q_tab, k_tab, k_tab]
        args += [cache_k.reshape(DEC_BATCH, DEPTH, PAST_LEN, LANES), cache_v.reshape(DEC_BATCH, DEPTH, PAST_LEN, LANES),
                 cos, sin, cos, sin]
    n_keys = lk + 2 * WINDOW + PAST_LEN if banded else lk
    return pl.pallas_call(
        partial(_swa_body, lk=lk, tq=tq, banded=banded),
        grid=(nb, lk // tq),
        in_specs=in_specs + [pl.BlockSpec(memory_space=pl.ANY)],
        out_specs=pl.BlockSpec((tq, GROUP_W), lambda b, qi: ((row0 + b * lk) // tq + qi, 0)),
        out_shape=jax.ShapeDtypeStruct((T_ALL, GROUP_W), F32),
        input_output_aliases={len(args): 0},
        scratch_shapes=[pltpu.VMEM((n_keys, LANES), BF16), pltpu.VMEM((n_keys, LANES), BF16)],
        compiler_params=_cparams("arbitrary", "arbitrary"),
        name="swa_attention_banded" if banded else "swa_attention",
    )(*args, dst)


LRU_GROUPS = LRU_W // LANES


def lru_gate_params(wa, ba, wx, bx):
    def tile_blocks(w):
        w = w.reshape(LRU_GROUPS, 2, LRU_BD, LRU_BD)
        z = jnp.zeros((LRU_GROUPS, LRU_BD, LRU_BD), F32)
        return jnp.concatenate([jnp.concatenate([w[:, 0], z], axis=-1), jnp.concatenate([z, w[:, 1]], axis=-1)], axis=-2)

    w = jnp.concatenate([tile_blocks(wa[0]), tile_blocks(wx[0]), tile_blocks(wa[1]), tile_blocks(wx[1])], axis=-1)
    b = jnp.stack([ba[0], bx[0], ba[1], bx[1]], axis=0).reshape(4, LRU_GROUPS, LANES)
    return w, jnp.moveaxis(b, 0, 1).reshape(LRU_GROUPS, 1, 4 * LANES)


def _shift_rows(x, s, row, n):
    if s == 0:
        return x
    ok = (row >= -s) if s < 0 else (row < n - s)
    return jnp.where(ok, pltpu.roll(x, (-s) % n, 0), 0.0)


def _linear_scan(a, b, row, n, reverse):
    s = 1
    while s < n:
        ok = (row < n - s) if reverse else (row >= s)
        shift = (n - s) if reverse else s
        b = jnp.where(ok, a * pltpu.roll(b, shift, 0) + b, b)
        a = jnp.where(ok, a * pltpu.roll(a, shift, 0), a)
        s *= 2
    return b


def _lru_body(x_ref, lg_ref, cw_ref, w_ref, bias_ref, lam_ref, h0_ref, _aliased_dst, o_ref, st_ref, *, n):
    x = x_ref[...]
    row = lax.broadcasted_iota(jnp.int32, x.shape, 0)
    cw = cw_ref[...]
    xc = _shift_rows(x, -(LRU_CONV // 2), row, n) * cw[0:1]
    for k in range(1, LRU_CONV):
        xc = xc + _shift_rows(x, k - LRU_CONV // 2, row, n) * cw[k:k + 1]
    gates = jnp.dot(xc.astype(BF16), w_ref[...].astype(BF16), preferred_element_type=F32) + bias_ref[...]
    lam = lam_ref[...]
    log_sig = jnp.minimum(lam, 0.0) - jnp.log1p(jnp.exp(-jnp.abs(lam)))
    h0 = h0_ref[...]
    hs = []
    for d in range(2):
        r = jax.nn.sigmoid(gates[:, (2 * d) * LANES:(2 * d + 1) * LANES])
        i = jax.nn.sigmoid(gates[:, (2 * d + 1) * LANES:(2 * d + 2) * LANES])
        log_a = LRU_C * r * log_sig[d:d + 1]
        a = jnp.exp(log_a)
        th = jnp.tanh(log_a)
        b = jnp.sqrt(-2.0 * th / (1.0 - th)) * (i * xc)
        first = n - 1 if d else 0
        b = jnp.where(row == first, b + a * h0[d:d + 1], b)
        hs.append(_linear_scan(a, b, row, n, reverse=bool(d)))
    lg = lg_ref[...]
    gelu = 0.5 * lg * (1.0 + jnp.tanh(math.sqrt(2.0 / math.pi) * (lg + 0.044715 * (lg * lg * lg))))
    o_ref[...] = (hs[0] + hs[1]) * gelu
    st_ref[0:1, :] = hs[0][n - 1:n, :]
    st_ref[1:2, :] = hs[1][0:1, :]


def rglru(proj, dst, gate_w, gate_b, lru_conv, lru_lam, h0, l, h0_l, prompt):
    nb, n, row0 = (BATCH, SEQ, 0) if prompt else (DEC_BATCH, DEC_SEQ, T_PROMPT)
    x_col = (D_IN - 2 * LRU_W) // LANES
    return pl.pallas_call(
        partial(_lru_body, n=n),
        grid=(nb, LRU_GROUPS),
        in_specs=[pl.BlockSpec((n, LANES), lambda s, c: (row0 // n + s, x_col + c)),
                  pl.BlockSpec((n, LANES), lambda s, c: (row0 // n + s, x_col + LRU_GROUPS + c)),
                  pl.BlockSpec((None, LRU_CONV, LANES), lambda s, c: (l, 0, c)),
                  pl.BlockSpec((None, LANES, 4 * LANES), lambda s, c: (c, 0, 0)),
                  pl.BlockSpec((None, 1, 4 * LANES), lambda s, c: (c, 0, 0)),
                  pl.BlockSpec((None, 2, LANES), lambda s, c: (l, 0, c)),
                  pl.BlockSpec((None, None, 2, LANES), lambda s, c: (s, h0_l, 0, c)),
                  pl.BlockSpec(memory_space=pl.ANY)],
        out_specs=[pl.BlockSpec((n, LANES), lambda s, c: (row0 // n + s, c)),
                   pl.BlockSpec((None, 2, LANES), lambda s, c: (s, 0, c))],
        out_shape=[jax.ShapeDtypeStruct((T_ALL, LRU_W), F32), jax.ShapeDtypeStruct((nb, 2, LRU_W), F32)],
        input_output_aliases={7: 0},
        compiler_params=_cparams("arbitrary", "arbitrary"),
        name="rglru_prompt" if prompt else "rglru_sample",
    )(proj, proj, lru_conv, gate_w, gate_b, lru_lam, h0, dst)


HY_P = 256
HY_F = 2 * HY_P
HY_ACC_VREGS = 8
HY_FILTER_ROWS = 256


def _split_bf16(x):
    hi = x.astype(BF16)
    return hi, (x - hi.astype(F32)).astype(BF16)


def _dot3(c_hi, c_lo, d):
    d_hi, d_lo = _split_bf16(d)
    return (jnp.dot(c_hi, d_hi, preferred_element_type=F32) + jnp.dot(c_hi, d_lo, preferred_element_type=F32)
            + jnp.dot(c_lo, d_hi, preferred_element_type=F32))


def dft_constants():
    f = np.arange(HY_F)[:, None].astype(np.float64)
    ang = 2.0 * np.pi * f * np.arange(HY_F)[None, :] / HY_F
    c, s = np.cos(ang), np.sin(ang)
    ch, sh = c[:, :HY_P], s[:, :HY_P]
    fwd_data = np.block([[ch, sh], [-sh, ch]])
    fwd_filter = np.concatenate([c, -s], axis=0)
    inverse = np.block([[ch.T, -sh.T], [sh.T, ch.T]]) / HY_F
    out = []
    for m in (fwd_data, fwd_filter, inverse):
        hi = m.astype(np.float32).astype(BF16)
        lo = (m - hi.astype(np.float64)).astype(np.float32).astype(BF16)
        out += [jnp.asarray(hi), jnp.asarray(lo)]
    return out


def _filter_spectra_body(cur_ref, prev_ref, fh_ref, fl_ref, g_ref):
    row = lax.broadcasted_iota(jnp.int32, prev_ref.shape, 0)
    prev = jnp.where((row == 0) | (pl.program_id(1) == 0), 0.0, prev_ref[...])
    g_ref[...] = _dot3(fh_ref[...], fl_ref[...], jnp.concatenate([cur_ref[...], prev], axis=0))


def filter_spectra(filt, fk_hi, fk_lo):
    nq, n, _ = filt.shape
    nblk = n // HY_P
    const = pl.BlockSpec((2 * HY_F, HY_F), lambda q, m: (0, 0))
    return pl.pallas_call(
        _filter_spectra_body,
        grid=(nq, nblk),
        in_specs=[pl.BlockSpec((None, HY_P, HY_W), lambda q, m: (q, m, 0)),
                  pl.BlockSpec((None, HY_P, HY_W), lambda q, m: (q, jnp.maximum(m - 1, 0), 0)),
                  const, const],
        out_specs=pl.BlockSpec((None, None, 2 * HY_F, HY_W), lambda q, m: (q, m, 0, 0)),
        out_shape=jax.ShapeDtypeStruct((nq, nblk, 2 * HY_F, HY_W), F32),
        compiler_params=_cparams("arbitrary", "arbitrary"),
        name="hyena_filter_spectra",
    )(filt, filt, fk_hi, fk_lo)


def _hyena_body(z0_ref, z1_ref, x0_ref, x1_ref, cwz_ref, cwx_ref, skip_ref, gf_ref, gb_ref, fdh_ref, fdl_ref,
                gih_ref, gil_ref, o_ref, zs, xs, us, ys, *, n, conv_z):
    nblk = n // HY_P
    row = lax.broadcasted_iota(jnp.int32, z0_ref.shape, 0)

    def short_conv(x, cw):
        acc = _shift_rows(x, -(HY_SHORT // 2), row, n) * cw[0:1]
        for k in range(1, HY_SHORT):
            acc = acc + _shift_rows(x, k - HY_SHORT // 2, row, n) * cw[k:k + 1]
        return acc

    for b, (z_ref, x_ref) in enumerate(((z0_ref, x0_ref), (z1_ref, x1_ref))):
        zs[b] = short_conv(z_ref[...], cwz_ref[...]) if conv_z else z_ref[...]
        xs[b] = short_conv(x_ref[...], cwx_ref[...])

    def forward(j, carry):
        r0 = pl.multiple_of(j * HY_P, HY_P)
        d = jnp.concatenate([zs[0, pl.ds(r0, HY_P), :], zs[1, pl.ds(r0, HY_P), :]], axis=0)
        us[j] = _dot3(fdh_ref[...], fdl_ref[...], d)
        return carry

    lax.fori_loop(0, nblk, forward, 0)

    lanes = z0_ref.shape[1]
    fc = HY_ACC_VREGS * 8 * LANES // lanes

    def out_block(i, carry):
        for c0 in range(0, HY_F, fc):
            re, im = pl.ds(c0, fc), pl.ds(HY_F + c0, fc)

            def causal(j, acc):
                g_re, g_im, u_re, u_im = gf_ref[i - j, re, :], gf_ref[i - j, im, :], us[j, re, :], us[j, im, :]
                return acc[0] + (g_re * u_re - g_im * u_im), acc[1] + (g_re * u_im + g_im * u_re)

            def anticausal(j, acc):
                g_re, g_im, u_re, u_im = gb_ref[j - i, re, :], gb_ref[j - i, im, :], us[j, re, :], us[j, im, :]
                return acc[0] + (g_re * u_re + g_im * u_im), acc[1] + (g_re * u_im - g_im * u_re)

            zero = jnp.zeros((fc, lanes), F32)
            acc = lax.fori_loop(0, i + 1, causal, (zero, zero))
            acc = lax.fori_loop(i, nblk, anticausal, acc)
            ys[re, :] = acc[0]
            ys[im, :] = acc[1]
        y = _dot3(gih_ref[...], gil_ref[...], ys[...])
        r0 = pl.multiple_of(i * HY_P, HY_P)
        for b in range(2):
            conv = y[b * HY_P:(b + 1) * HY_P]
            o_ref[pl.ds(b * n + r0, HY_P), :] = xs[b, pl.ds(r0, HY_P), :] * (conv + skip_ref[...] * zs[b, pl.ds(r0, HY_P), :])
        return carry

    lax.fori_loop(0, nblk, out_block, 0)


def hyena_order(zsrc, z_col0, proj, x_col0, hy_conv, hy_skip, spectra, consts, l, order, prompt):
    fd_hi, fd_lo, _, _, gi_hi, gi_lo = consts
    if prompt:
        n, cw, npair, z_row0 = SEQ, 2 * LANES, BATCH // 2, 0
    else:
        n, cw, npair, z_row0 = DEC_SEQ, LANES, DEC_BATCH // 2, (T_PROMPT // DEC_SEQ if order == 0 else 0)
    x_row0 = 0 if prompt else T_PROMPT // DEC_SEQ
    ncg = HY_W // cw
    nblk = n // HY_P
    hy_col = sum(IN_SIZES[:6])
    once = pl.Buffered(1)
    seq = lambda row0, col0, b: pl.BlockSpec((n, cw), lambda p, c: (row0 + 2 * p + b, col0 // cw + c), pipeline_mode=once)
    taps = lambda col0: pl.BlockSpec((None, HY_SHORT, cw), lambda p, c: (l, 0, col0 // cw + c))
    spec_blk = lambda q: pl.BlockSpec((None, nblk, 2 * HY_F, cw), lambda p, c: (q, 0, 0, c), pipeline_mode=once)
    const = lambda shape: pl.BlockSpec(shape, lambda p, c: (0, 0))
    return pl.pallas_call(
        partial(_hyena_body, n=n, conv_z=(order == 0)),
        grid=(npair, ncg),
        in_specs=[seq(z_row0, z_col0, 0), seq(z_row0, z_col0, 1), seq(x_row0, x_col0, 0), seq(x_row0, x_col0, 1),
                  taps(0), taps(x_col0 - hy_col),
                  pl.BlockSpec((None, 1, cw), lambda p, c: (l * HY_ORDER + order, 0, c)),
                  spec_blk(2 * order), spec_blk(2 * order + 1),
                  const((2 * HY_F, HY_F)), const((2 * HY_F, HY_F)), const((HY_F, 2 * HY_F)), const((HY_F, 2 * HY_F))],
        out_specs=pl.BlockSpec((2 * n, cw), lambda p, c: (p, c)),
        out_shape=jax.ShapeDtypeStruct((npair * 2 * n, HY_W), F32),
        scratch_shapes=[pltpu.VMEM((2, n, cw), F32), pltpu.VMEM((2, n, cw), F32), pltpu.VMEM((nblk, 2 * HY_F, cw), F32),
                        pltpu.VMEM((2 * HY_F, cw), F32)],
        compiler_params=_cparams("arbitrary", "arbitrary"),
        name=f"hyena_{'prompt' if prompt else 'sample'}_order{order}",
    )(zsrc, zsrc, proj, proj, hy_conv, hy_conv, hy_skip.reshape(DEPTH * HY_ORDER, 1, HY_W), spectra, spectra,
      fd_hi, fd_lo, gi_hi, gi_lo)


def _hyena_filter_body(w1_ref, b1_ref, w2_ref, b2_ref, fr_ref, w3f_ref, w3b_ref, rf_ref, rb_ref, o_ref, *, n):
    hp = lax.Precision.HIGHEST
    ch = min(n, HY_FILTER_ROWS)
    c = lax.broadcasted_iota(jnp.int32, (ch, LANES), 1)
    band = jnp.where(c <= HY_BANDS, c, c - HY_BANDS).astype(F32)
    fr = fr_ref[...]

    def raw(i, sumsq):
        r0 = pl.multiple_of(i * ch, ch)
        t = (lax.broadcasted_iota(jnp.int32, (ch, LANES), 0) + r0).astype(F32) / n
        ang = 2.0 * math.pi * t * band
        z = jnp.where(c == 0, t, jnp.where(c <= HY_BANDS, jnp.sin(ang), jnp.where(c < HY_POS_DIM, jnp.cos(ang), 0.0)))
        h = jnp.sin(fr[0:1] * (jnp.dot(z, w1_ref[...], preferred_element_type=F32, precision=hp) + b1_ref[...]))
        h = jnp.sin(fr[1:2] * (jnp.dot(h, w2_ref[...], preferred_element_type=F32, precision=hp) + b2_ref[...]))
        tcol = t[:, 0:1]
        f = jnp.dot(h, w3f_ref[...], preferred_element_type=F32, precision=hp) * jnp.exp(-tcol * jnp.exp(rf_ref[...]))
        b = jnp.dot(h, w3b_ref[...], preferred_element_type=F32, precision=hp) * jnp.exp(-tcol * jnp.exp(rb_ref[...]))
        o_ref[0, pl.ds(r0, ch), :] = f
        o_ref[1, pl.ds(r0, ch), :] = b
        return sumsq + jnp.sum(f * f, axis=0, keepdims=True) + jnp.sum(b * b, axis=0, keepdims=True)

    scale = lax.rsqrt(lax.fori_loop(0, n // ch, raw, jnp.zeros((1, LANES), F32)) + EPS)

    def normalise(i, carry):
        r0 = pl.multiple_of(i * ch, ch)
        f, b = o_ref[0, pl.ds(r0, ch), :] * scale, o_ref[1, pl.ds(r0, ch), :] * scale
        first = lax.broadcasted_iota(jnp.int32, (ch, LANES), 0) + r0 == 0
        o_ref[0, pl.ds(r0, ch), :] = jnp.where(first, f + b, f)
        o_ref[1, pl.ds(r0, ch), :] = jnp.where(first, 0.0, b)
        return carry

    lax.fori_loop(0, n // ch, normalise, 0)


def hyena_filters_onesided(n, hy_w1, hy_b1, hy_w2, hy_b2, hy_w3, hy_freq, hy_log_rate, l):
    nt = HY_W // LANES
    w1 = jnp.pad(hy_w1, ((0, 0), (0, LANES - HY_POS_DIM), (0, 0)))
    vec = lambda a: a.reshape(DEPTH, 1, HY_HID)
    col = lambda d: (lambda o, w: (l, 0, (2 * o + d) * nt + w))
    whole = lambda shape: pl.BlockSpec((None,) + shape, lambda o, w: (l,) + (0,) * len(shape))
    return pl.pallas_call(
        partial(_hyena_filter_body, n=n),
        grid=(HY_ORDER, nt),
        in_specs=[whole((LANES, HY_HID)), whole((1, HY_HID)), whole((HY_HID, HY_HID)), whole((1, HY_HID)),
                  whole((2, HY_HID)),
                  pl.BlockSpec((None, HY_HID, LANES), col(0)), pl.BlockSpec((None, HY_HID, LANES), col(1)),
                  pl.BlockSpec((None, 1, LANES), col(0)), pl.BlockSpec((None, 1, LANES), col(1))],
        out_specs=pl.BlockSpec((None, 2, n, LANES), lambda o, w: (o, 0, 0, w)),
        out_shape=jax.ShapeDtypeStruct((HY_ORDER, 2, n, HY_W), F32),
        compiler_params=_cparams("arbitrary", "arbitrary"),
        name="hyena_filters",
    )(w1, vec(hy_b1), hy_w2, vec(hy_b2), hy_freq, hy_w3, hy_w3,
      hy_log_rate.reshape(DEPTH, 1, 2 * HY_ORDER * HY_W), hy_log_rate.reshape(DEPTH, 1, 2 * HY_ORDER * HY_W)
      ).reshape(2 * HY_ORDER, n, HY_W)


def hyena_group(proj, filt4, hy_conv, hy_skip, consts, l, prompt):
    spectra = filter_spectra(filt4, consts[2], consts[3])
    hy_col = sum(IN_SIZES[:6])
    z = hyena_order(proj, hy_col, proj, hy_col + HY_W, hy_conv, hy_skip, spectra, consts, l, 0, prompt)
    return hyena_order(z, 0, proj, hy_col + 2 * HY_W, hy_conv, hy_skip, spectra, consts, l, 1, prompt)


def moe_layer(x1, mod, p, l):
    h2, idx, gates, rank, counts = router(x1, p['norm2'], mod, p['router_w'], p['router_bias'], l)
    idx, rank = idx[:, :TOP_K], rank[:, :TOP_K]
    nblk = (counts[0] + TM_E - 1) // TM_E
    blk_end = jnp.cumsum(nblk)
    row_start = (blk_end - nblk) * TM_E
    dest = row_start[idx] + rank
    n_valid = blk_end[-1:].astype(jnp.int32)
    blk = jnp.minimum(jnp.arange(N_BLK_MAX, dtype=jnp.int32), n_valid[0] - 1)
    blk_e = jnp.minimum(jnp.sum(blk_end[None, :] <= blk[:, None], axis=-1), N_EXPERTS - 1).astype(jnp.int32)
    tok = jnp.broadcast_to(jnp.arange(T_ALL, dtype=jnp.int32)[:, None], dest.shape)
    src = jnp.zeros((N_BLK_MAX * TM_E,), jnp.int32).at[dest.reshape(-1)].set(tok.reshape(-1))
    ybuf = routed_experts(h2, src, blk_e, n_valid, p['moe_w_gu'], p['moe_w_dn'], l)
    return shared_combine(h2, p['sh_w_gu'], p['sh_w_dn'], ybuf, dest, gates, x1, mod, l)


def kernel(x_prompt, x_sample, cache_diff_k, cache_diff_v, cache_swa_k, cache_swa_v, state_lru, c, c_ctx,
           w_mod, b_mod, norm1, norm2, w_in, w_out, diff_lam, diff_subln, swa_sink, hy_conv, hy_w1, hy_b1,
           hy_w2, hy_b2, hy_w3, hy_freq, hy_log_rate, hy_skip, lru_conv, lru_wa, lru_ba, lru_wx, lru_bx, lru_lam,
           router_w, router_bias, moe_w_gu, moe_w_dn, sh_w_gu, sh_w_dn, final_norm):
    x = jnp.concatenate([x_prompt.reshape(T_PROMPT, D_MODEL), x_sample.reshape(T_SAMPLE, D_MODEL)], axis=0)
    cond = jnp.concatenate([c_ctx[None, :], c, jnp.zeros((8 - N_COND, D_MODEL), F32)], axis=0)
    dk_l, dv_l, sk_l, sv_l, st_l = [], [], [], [], []
    tables = rope_tables(DEC_SEQ)
    consts = dft_constants()
    h0_prompt = jnp.zeros((BATCH, 1, 2, LRU_W), F32)
    col = np.cumsum((0,) + IN_SIZES).tolist()
    empty = lambda: jnp.zeros((T_ALL, GROUP_W), F32)
    for l in range(DEPTH):
        p = {'norm2': norm2, 'router_w': router_w, 'router_bias': router_bias, 'moe_w_gu': moe_w_gu,
             'moe_w_dn': moe_w_dn, 'sh_w_gu': sh_w_gu, 'sh_w_dn': sh_w_dn}
        lam_init = 0.8 - 0.6 * math.exp(-0.3 * l)
        mod = modulation(cond, w_mod, b_mod, l)[:N_COND].reshape(N_COND * 6, 1, D_MODEL)
        proj = in_proj(x, norm1, mod, w_in, l)
        proj_p = proj[:T_PROMPT].reshape(BATCH, SEQ, D_IN)
        dk_l.append(proj_p[..., col[1]:col[2]].reshape(BATCH, SEQ, DIFF_HEADS, 2, DIFF_QK))
        dv_l.append(proj_p[..., col[2]:col[3]].reshape(BATCH, SEQ, DIFF_HEADS, DIFF_VD))
        sk_l.append(proj_p[..., col[4]:col[5]].reshape(BATCH, SEQ, SWA_KV_HEADS, HEAD_DIM))
        sv_l.append(proj_p[..., col[5]:col[6]].reshape(BATCH, SEQ, SWA_KV_HEADS, HEAD_DIM))
        d_out = diff_attention(proj, empty(), diff_lam, diff_subln, l, lam_init)
        d_out = diff_attention(proj, d_out, diff_lam, diff_subln, l, lam_init, cache_diff_k, cache_diff_v, tables)
        s_out = swa_attention(proj, empty(), swa_sink, l)
        s_out = swa_attention(proj, s_out, swa_sink, l, cache_swa_k, cache_swa_v, tables)
        hy_args = (hy_w1, hy_b1, hy_w2, hy_b2, hy_w3, hy_freq, hy_log_rate, l)
        hy_out = jnp.concatenate(
            [hyena_group(proj, hyena_filters_onesided(SEQ, *hy_args), hy_conv, hy_skip, consts, l, True),
             hyena_group(proj, hyena_filters_onesided(DEC_SEQ, *hy_args), hy_conv, hy_skip, consts, l, False)], axis=0)
        gate_w, gate_b = lru_gate_params(lru_wa[l], lru_ba[l], lru_wx[l], lru_bx[l])
        lru_out, st = rglru(proj, empty(), gate_w, gate_b, lru_conv, lru_lam, h0_prompt, l, 0, True)
        lru_out, _ = rglru(proj, lru_out, gate_w, gate_b, lru_conv, lru_lam, state_lru, l, l, False)
        st_l.append(st)
        x1 = out_proj([d_out, s_out, hy_out, lru_out], w_out, x, mod, l)
        x = moe_layer(x1, mod, p, l)
    y = final_rmsnorm(x, final_norm)
    y_prompt = y[:T_PROMPT].reshape(BATCH, SEQ, D_MODEL)
    y_sample = y[T_PROMPT:].reshape(DEC_BATCH, DEC_SEQ, D_MODEL)
    return (y_prompt, y_sample, jnp.stack(dk_l, axis=1), jnp.stack(dv_l, axis=1), jnp.stack(sk_l, axis=1),
            jnp.stack(sv_l, axis=1), jnp.stack(st_l, axis=1))
```

```python
import math
from functools import partial
import jax, jax.numpy as jnp
from jax import lax
import numpy as np
from jax.experimental import pallas as pl
from jax.experimental.pallas import tpu as pltpu

D_MODEL = 2048
BATCH = 16
SEQ = 256
DEPTH = 2
DEC_BATCH = 2
DEC_SEQ = 4096
PAST_LEN = 256

GRID_W = 64
EPS = 1e-6
F32 = jnp.float32
GROUP_W = D_MODEL // 4
D_MIX = 4 * GROUP_W
DIFF_QK = 64
DIFF_VD = 2 * DIFF_QK
DIFF_HEADS = GROUP_W // DIFF_VD
DIFF_SCALE = DIFF_QK ** -0.5
HEAD_DIM = 64
SWA_HEADS = GROUP_W // HEAD_DIM
SWA_KV_HEADS = SWA_HEADS // 4
SWA_GROUP = SWA_HEADS // SWA_KV_HEADS
SWA_SCALE = HEAD_DIM ** -0.5
WINDOW = 128
ROPE_PAIRS = HEAD_DIM // 4
ROPE_BASE = 10000.0
HY_W = GROUP_W
HY_ORDER = 2
HY_SHORT = 3
HY_BANDS = 16
HY_POS_DIM = 1 + 2 * HY_BANDS
HY_HID = 64
LRU_W = GROUP_W
LRU_BLOCKS = 8
LRU_BD = LRU_W // LRU_BLOCKS
LRU_CONV = 4
LRU_C = 8.0
N_EXPERTS = 64
TOP_K = 6
EXPERT_FF = D_MODEL // 4
SHARED_FF = EXPERT_FF
ROUTE_SCALE = 2.5
NEG_INF = -1e30
IN_SIZES = (2 * DIFF_HEADS * DIFF_QK, 2 * DIFF_HEADS * DIFF_QK, DIFF_HEADS * DIFF_VD, SWA_HEADS * HEAD_DIM, SWA_KV_HEADS * HEAD_DIM, SWA_KV_HEADS * HEAD_DIM, (HY_ORDER + 1) * HY_W, LRU_W, LRU_W)
D_IN = sum(IN_SIZES)


BF16 = jnp.bfloat16
T_PROMPT = BATCH * SEQ
T_SAMPLE = DEC_BATCH * DEC_SEQ
T_ALL = T_PROMPT + T_SAMPLE
N_COND = 1 + DEC_BATCH
TM = 1024
TM_IN = 2048
TM_SH = 256
TN_IN = 256
TN_OUT = 512
TN_MOD = 1024
TM_E = 256
N_ASSIGN = T_ALL * TOP_K
N_BLK_MAX = N_ASSIGN // TM_E + N_EXPERTS
LANES = 128
VMEM_LIMIT = 56 * 1024 * 1024
VMEM_LIMIT_IN_PROJ = 60 * 1024 * 1024


def _cparams(*sem):
    return pltpu.CompilerParams(dimension_semantics=sem, vmem_limit_bytes=VMEM_LIMIT)


def _cond_row(i, tm):
    return jnp.where(i < T_PROMPT // tm, 0, 1 + (i - T_PROMPT // tm) // (DEC_SEQ // tm))


def _mod_spec(which, tm, tn=D_MODEL):
    if tn == D_MODEL:
        return pl.BlockSpec((1, 1, D_MODEL), lambda i, *_: (_cond_row(i, tm) * 6 + which, 0, 0))
    return pl.BlockSpec((1, 1, tn), lambda i, j: (_cond_row(i, tm) * 6 + which, 0, j))


def _mod_body(c_ref, *rest):
    w_refs, (b_ref, o_ref) = rest[:W_STREAMS], rest[W_STREAMS:]
    c = c_ref[...]
    a = (c * jax.nn.sigmoid(c)).astype(BF16)
    o_ref[...] = _dot_k_slabs(a, w_refs) + b_ref[0]


def modulation(cond, w_mod, b_mod, l):
    n = 6 * D_MODEL
    return pl.pallas_call(
        _mod_body,
        grid=(n // TN_MOD,),
        in_specs=[pl.BlockSpec((8, D_MODEL), lambda j: (0, 0))]
                 + [pl.BlockSpec((None, D_MODEL // W_STREAMS, TN_MOD), lambda j, s=s: (l, s, j)) for s in range(W_STREAMS)]
                 + [pl.BlockSpec((None, 1, TN_MOD), lambda j: (l, 0, j))],
        out_specs=pl.BlockSpec((8, TN_MOD), lambda j: (0, j)),
        out_shape=jax.ShapeDtypeStruct((8, n), F32),
        compiler_params=_cparams("arbitrary"),
        name="modulation",
    )(cond, *([w_mod] * W_STREAMS), b_mod.reshape(DEPTH, 1, n))


def _norm_mod(x, g, sc, sh):
    y = x * lax.rsqrt(jnp.mean(x * x, axis=-1, keepdims=True) + EPS) * g
    return y * (1.0 + sc) + sh


W_STREAMS = 4


def _k_slab_specs(rows, cols, index):
    return [pl.BlockSpec((None, rows // W_STREAMS, cols), lambda i, j, s=s: (index(i, j)[0], s, index(i, j)[1]))
            for s in range(W_STREAMS)]


def _dot_k_slabs(a, w_refs):
    k = a.shape[1] // len(w_refs)
    acc = jnp.dot(a[:, 0:k], w_refs[0][...].astype(BF16), preferred_element_type=F32)
    for s in range(1, len(w_refs)):
        acc = acc + jnp.dot(a[:, s * k:(s + 1) * k], w_refs[s][...].astype(BF16), preferred_element_type=F32)
    return acc


def _in_proj_body(x_ref, g_ref, sc_ref, sh_ref, *rest):
    w_refs, (o_ref, hb_ref) = rest[:W_STREAMS], rest[W_STREAMS:]

    @pl.when(pl.program_id(1) == 0)
    def _():
        for r in range(0, TM_IN, TM_IN // 4):
            rows = slice(r, r + TM_IN // 4)
            hb_ref[rows, :] = _norm_mod(x_ref[rows, :], g_ref[0], sc_ref[0], sh_ref[0]).astype(BF16)

    o_ref[...] = _dot_k_slabs(hb_ref[...], w_refs)


def in_proj(x, norm1, mod, w_in, l):
    return pl.pallas_call(
        _in_proj_body,
        grid=(T_ALL // TM_IN, D_IN // TN_IN),
        in_specs=[pl.BlockSpec((TM_IN, D_MODEL), lambda i, j: (i, 0)),
                  pl.BlockSpec((None, 1, D_MODEL), lambda i, j: (l, 0, 0)),
                  _mod_spec(1, TM_IN), _mod_spec(0, TM_IN)] + _k_slab_specs(D_MODEL, TN_IN, lambda i, j: (l, j)),
        out_specs=pl.BlockSpec((TM_IN, TN_IN), lambda i, j: (i, j)),
        out_shape=jax.ShapeDtypeStruct((T_ALL, D_IN), F32),
        scratch_shapes=[pltpu.VMEM((TM_IN, D_MODEL), BF16)],
        compiler_params=pltpu.CompilerParams(dimension_semantics=("arbitrary", "arbitrary"),
                                             vmem_limit_bytes=VMEM_LIMIT_IN_PROJ),
        name="in_proj",
    )(x, norm1.reshape(DEPTH, 1, D_MODEL), mod, mod, *([w_in] * W_STREAMS))


N_MIX = D_MIX // GROUP_W


def _out_proj_body(*refs):
    m_refs, (w_ref, x_ref, g1_ref, o_ref, mb_ref) = refs[:N_MIX], refs[N_MIX:]

    @pl.when(pl.program_id(1) == 0)
    def _():
        for g in range(N_MIX):
            mb_ref[g] = m_refs[g][...].astype(BF16)

    acc = jnp.dot(mb_ref[0], w_ref[0:GROUP_W, :].astype(BF16), preferred_element_type=F32)
    for g in range(1, N_MIX):
        acc = acc + jnp.dot(mb_ref[g], w_ref[g * GROUP_W:(g + 1) * GROUP_W, :].astype(BF16), preferred_element_type=F32)
    o_ref[...] = x_ref[...] + g1_ref[0] * acc


def out_proj(mixes, w_out, x, mod, l):
    return pl.pallas_call(
        _out_proj_body,
        grid=(T_ALL // TM, D_MODEL // TN_OUT),
        in_specs=[pl.BlockSpec((TM, GROUP_W), lambda i, j: (i, 0))] * N_MIX + [
                  pl.BlockSpec((None, D_MIX, TN_OUT), lambda i, j: (l, 0, j)),
                  pl.BlockSpec((TM, TN_OUT), lambda i, j: (i, j)),
                  _mod_spec(2, TM, TN_OUT)],
        out_specs=pl.BlockSpec((TM, TN_OUT), lambda i, j: (i, j)),
        out_shape=jax.ShapeDtypeStruct((T_ALL, D_MODEL), F32),
        scratch_shapes=[pltpu.VMEM((N_MIX, TM, GROUP_W), BF16)],
        compiler_params=_cparams("arbitrary", "arbitrary"),
        name="out_proj",
    )(*mixes, w_out, x, mod)


def _router_body(x_ref, g_ref, sc_ref, sh_ref, wr_ref, rb_ref, h_ref, idx_ref, gate_ref, rank_ref, cnt_ref, carry_ref):
    i = pl.program_id(0)

    @pl.when(i == 0)
    def _():
        carry_ref[...] = jnp.zeros_like(carry_ref)

    h = _norm_mod(x_ref[...], g_ref[0], sc_ref[0], sh_ref[0])
    h_ref[...] = h
    h_hi, h_lo = _split_bf16(h)
    w_hi, w_lo = _split_bf16(wr_ref[...])
    logits = (jnp.dot(h_hi, w_hi, preferred_element_type=F32) + jnp.dot(h_lo, w_hi, preferred_element_type=F32)
              + jnp.dot(h_hi, w_lo, preferred_element_type=F32))
    s = jax.nn.sigmoid(logits)
    cur = s + rb_ref[0]
    e_iota = lax.broadcasted_iota(jnp.int32, s.shape, 1).astype(F32)
    lane = lax.broadcasted_iota(jnp.int32, (TM, LANES), 1)
    r_iota = lax.broadcasted_iota(jnp.int32, (TM, TM), 0)
    c_iota = lax.broadcasted_iota(jnp.int32, (TM, TM), 1)
    tri = (c_iota <= r_iota).astype(BF16)
    idx_out = jnp.zeros((TM, LANES), F32)
    gate_out = jnp.zeros((TM, LANES), F32)
    rank_out = jnp.zeros((TM, LANES), F32)
    gsum = jnp.zeros((TM, 1), F32)
    carry = carry_ref[...]
    for k in range(TOP_K):
        m = jnp.max(cur, axis=-1, keepdims=True)
        ek = jnp.min(jnp.where(cur == m, e_iota, float(N_EXPERTS)), axis=-1, keepdims=True)
        hit = e_iota == ek
        gk = jnp.sum(jnp.where(hit, s, 0.0), axis=-1, keepdims=True)
        cur = jnp.where(hit, -jnp.inf, cur)
        onehot = hit.astype(BF16)
        cum = jnp.dot(tri, onehot, preferred_element_type=F32)
        rk = jnp.sum(jnp.where(hit, carry + cum, 0.0), axis=-1, keepdims=True) - 1.0
        carry = carry + cum[TM - 1:TM, :]
        gsum = gsum + gk
        idx_out = jnp.where(lane == k, ek, idx_out)
        gate_out = jnp.where(lane == k, gk, gate_out)
        rank_out = jnp.where(lane == k, rk, rank_out)
    carry_ref[...] = carry
    idx_ref[...] = idx_out.astype(jnp.int32)
    gate_ref[...] = ROUTE_SCALE * gate_out / gsum
    rank_ref[...] = rank_out.astype(jnp.int32)
    cnt_ref[...] = carry.astype(jnp.int32)


def router(x1, norm2, mod, router_w, router_bias, l):
    tok_out = lambda dt: jax.ShapeDtypeStruct((T_ALL, LANES), dt)
    tok_spec = pl.BlockSpec((TM, LANES), lambda i: (i, 0))
    return pl.pallas_call(
        _router_body,
        grid=(T_ALL // TM,),
        in_specs=[pl.BlockSpec((TM, D_MODEL), lambda i: (i, 0)),
                  pl.BlockSpec((None, 1, D_MODEL), lambda i: (l, 0, 0)),
                  _mod_spec(4, TM), _mod_spec(3, TM),
                  pl.BlockSpec((None, D_MODEL, N_EXPERTS), lambda i: (l, 0, 0)),
                  pl.BlockSpec((None, 1, N_EXPERTS), lambda i: (l, 0, 0))],
        out_specs=[pl.BlockSpec((TM, D_MODEL), lambda i: (i, 0)), tok_spec, tok_spec, tok_spec,
                   pl.BlockSpec((1, N_EXPERTS), lambda i: (0, 0))],
        out_shape=[jax.ShapeDtypeStruct((T_ALL, D_MODEL), F32), tok_out(jnp.int32), tok_out(F32), tok_out(jnp.int32),
                   jax.ShapeDtypeStruct((1, N_EXPERTS), jnp.int32)],
        scratch_shapes=[pltpu.VMEM((1, N_EXPERTS), F32)],
        compiler_params=_cparams("arbitrary"),
        name="router",
    )(x1, norm2.reshape(DEPTH, 1, D_MODEL), mod, mod, router_w, router_bias.reshape(DEPTH, 1, N_EXPERTS))


def _experts_body(be_ref, nv_ref, src_ref, nxt_ref, h_hbm, *rest):
    wgu_refs, wdn_refs = rest[:W_STREAMS], rest[W_STREAMS:2 * W_STREAMS]
    y_ref, wgu_b, wdn_b, xg0, xg1, sems = rest[2 * W_STREAMS:]
    i = pl.program_id(0)
    n_valid = nv_ref[0]

    def row_copy(idx_ref, t, buf, sem):
        return pltpu.make_async_copy(h_hbm.at[pl.ds(idx_ref[0, t], 1)], buf.at[pl.ds(t, 1)], sem)

    def all_rows(buf, sem):
        return pltpu.make_async_copy(h_hbm.at[pl.ds(0, TM_E)], buf, sem)

    @pl.when(i == 0)
    def _():
        def issue(t, carry):
            row_copy(src_ref, t, xg0, sems.at[0]).start()
            return carry

        lax.fori_loop(0, TM_E, issue, 0, unroll=8)

    @pl.when(i < n_valid)
    def _():
        prev = be_ref[jnp.maximum(i - 1, 0)]

        @pl.when(jnp.logical_or(i == 0, be_ref[i] != prev))
        def _():
            kg, kd = D_MODEL // W_STREAMS, EXPERT_FF // W_STREAMS
            for s in range(W_STREAMS):
                wgu_b[s * kg:(s + 1) * kg, :] = wgu_refs[s][...].astype(BF16)
                wdn_b[s * kd:(s + 1) * kd, :] = wdn_refs[s][...].astype(BF16)

        for parity, (cur, nxt) in enumerate(((xg0, xg1), (xg1, xg0))):
            @pl.when(i % 2 == parity)
            def _():
                all_rows(cur, sems.at[parity]).wait()
                for t in range(TM_E):
                    row_copy(nxt_ref, t, nxt, sems.at[1 - parity]).start()
                hmid = jnp.dot(cur[...].astype(BF16), wgu_b[...], preferred_element_type=F32)
                a, b = hmid[:, :EXPERT_FF], hmid[:, EXPERT_FF:]
                act = (a * jax.nn.sigmoid(a) * b).astype(BF16)
                y_ref[...] = jnp.dot(act, wdn_b[...], preferred_element_type=F32)

                @pl.when(i == n_valid - 1)
                def _():
                    all_rows(nxt, sems.at[1 - parity]).wait()

    @pl.when(i >= n_valid)
    def _():
        y_ref[...] = jnp.zeros_like(y_ref)


def routed_experts(h2, src, blk_e, n_valid, moe_w_gu, moe_w_dn, l):
    idx_spec = lambda off: pl.BlockSpec((None, 1, TM_E), lambda i, be, nv: (jnp.minimum(i + off, nv[0] - 1), 0, 0),
                                        memory_space=pltpu.SMEM)
    return pl.pallas_call(
        _experts_body,
        grid_spec=pltpu.PrefetchScalarGridSpec(
            num_scalar_prefetch=2,
            grid=(N_BLK_MAX,),
            in_specs=[idx_spec(0), idx_spec(1),
                      pl.BlockSpec(memory_space=pl.ANY)]
                     + [pl.BlockSpec((None, None, D_MODEL // W_STREAMS, 2 * EXPERT_FF),
                                     lambda i, be, nv, s=s: (l, be[i], s, 0)) for s in range(W_STREAMS)]
                     + [pl.BlockSpec((None, None, EXPERT_FF // W_STREAMS, D_MODEL),
                                     lambda i, be, nv, s=s: (l, be[i], s, 0)) for s in range(W_STREAMS)],
            out_specs=pl.BlockSpec((TM_E, D_MODEL), lambda i, be, nv: (i, 0)),
            scratch_shapes=[pltpu.VMEM((D_MODEL, 2 * EXPERT_FF), BF16), pltpu.VMEM((EXPERT_FF, D_MODEL), BF16),
                            pltpu.VMEM((TM_E, D_MODEL), F32), pltpu.VMEM((TM_E, D_MODEL), F32),
                            pltpu.SemaphoreType.DMA((2,))]),
        out_shape=jax.ShapeDtypeStruct((N_BLK_MAX * TM_E, D_MODEL), F32),
        compiler_params=_cparams("arbitrary"),
        name="routed_experts",
    )(blk_e, n_valid, src.reshape(N_BLK_MAX, 1, TM_E), src.reshape(N_BLK_MAX, 1, TM_E), h2,
      *([moe_w_gu] * W_STREAMS), *([moe_w_dn] * W_STREAMS))


def _gathered_rows(ybuf_hbm, rows, sem, n):
    return pltpu.make_async_copy(ybuf_hbm.at[pl.ds(0, n)], rows.at[pl.ds(0, n)], sem)


def _shared_body(dest_ref, gate_ref, h_ref, wgu_ref, wdn_ref, x_ref, g2_ref, ybuf_hbm, o_ref, wgu_b, wdn_b, rows, sem):
    @pl.when(pl.program_id(0) == 0)
    def _():
        wgu_b[...] = wgu_ref[...].astype(BF16)
        wdn_b[...] = wdn_ref[...].astype(BF16)

    for t in range(TM_SH):
        for k in range(TOP_K):
            src = dest_ref[0, t * TOP_K + k]
            pltpu.make_async_copy(ybuf_hbm.at[pl.ds(src, 1)], rows.at[pl.ds(k * TM_SH + t, 1)], sem).start()
    hmid = jnp.dot(h_ref[...].astype(BF16), wgu_b[...], preferred_element_type=F32)
    a, b = hmid[:, :SHARED_FF], hmid[:, SHARED_FF:]
    act = (a * jax.nn.sigmoid(a) * b).astype(BF16)
    y = jnp.dot(act, wdn_b[...], preferred_element_type=F32)
    _gathered_rows(ybuf_hbm, rows, sem, TOP_K * TM_SH).wait()
    gates = gate_ref[...]
    for k in range(TOP_K):
        y = y + gates[:, k:k + 1] * rows[k * TM_SH:(k + 1) * TM_SH, :]
    o_ref[...] = x_ref[...] + g2_ref[0] * y


def shared_combine(h2, sh_w_gu, sh_w_dn, ybuf, dest, gates, x1, mod, l):
    tile = pl.BlockSpec((TM_SH, D_MODEL), lambda i: (i, 0))
    once = pl.Buffered(1)
    n_tiles = T_ALL // TM_SH
    return pl.pallas_call(
        _shared_body,
        grid=(n_tiles,),
        in_specs=[pl.BlockSpec((None, 1, TM_SH * TOP_K), lambda i: (i, 0, 0), memory_space=pltpu.SMEM),
                  pl.BlockSpec((TM_SH, LANES), lambda i: (i, 0)),
                  tile,
                  pl.BlockSpec((None, D_MODEL, 2 * SHARED_FF), lambda i: (l, 0, 0), pipeline_mode=once),
                  pl.BlockSpec((None, SHARED_FF, D_MODEL), lambda i: (l, 0, 0), pipeline_mode=once),
                  tile, _mod_spec(5, TM_SH),
                  pl.BlockSpec(memory_space=pl.ANY)],
        out_specs=tile,
        out_shape=jax.ShapeDtypeStruct((T_ALL, D_MODEL), F32),
        scratch_shapes=[pltpu.VMEM((D_MODEL, 2 * SHARED_FF), BF16), pltpu.VMEM((SHARED_FF, D_MODEL), BF16),
                        pltpu.VMEM((TOP_K * TM_SH, D_MODEL), F32), pltpu.SemaphoreType.DMA(())],
        compiler_params=_cparams("arbitrary"),
        name="shared_combine",
    )(dest.reshape(n_tiles, 1, TM_SH * TOP_K), gates, h2, sh_w_gu, sh_w_dn, x1, mod, ybuf)


def _final_norm_body(x_ref, g_ref, o_ref):
    x = x_ref[...]
    o_ref[...] = x * lax.rsqrt(jnp.mean(x * x, axis=-1, keepdims=True) + EPS) * g_ref[...]


def final_rmsnorm(x, g):
    return pl.pallas_call(
        _final_norm_body,
        grid=(T_ALL // TM,),
        in_specs=[pl.BlockSpec((TM, D_MODEL), lambda i: (i, 0)), pl.BlockSpec((1, D_MODEL), lambda i: (0, 0))],
        out_specs=pl.BlockSpec((TM, D_MODEL), lambda i: (i, 0)),
        out_shape=jax.ShapeDtypeStruct(x.shape, x.dtype),
        compiler_params=_cparams("arbitrary"),
        name="final_norm",
    )(x, g.reshape(1, D_MODEL))


TQ_DIFF = 256
HALF = LANES // 2
NT_DIMS = (((1,), (1,)), ((), ()))


def rope_tables(L):
    pos = jnp.arange(L)
    d = jnp.arange(LANES) % HEAD_DIM
    p = jnp.where(d // (2 * ROPE_PAIRS) == 0, (pos // GRID_W)[:, None], (pos % GRID_W)[:, None]).astype(F32)
    inv = ROPE_BASE ** (-(d % ROPE_PAIRS).astype(F32) / ROPE_PAIRS)
    ang = p * inv
    return jnp.cos(ang), jnp.where((d // ROPE_PAIRS) % 2 == 0, -jnp.sin(ang), jnp.sin(ang))


def _rope(x, cos, sin_signed):
    lane = lax.broadcasted_iota(jnp.int32, x.shape, 1)
    is_x1 = (lane // ROPE_PAIRS) % 2 == 0
    partner = jnp.where(is_x1, pltpu.roll(x, LANES - ROPE_PAIRS, 1), pltpu.roll(x, ROPE_PAIRS, 1))
    return x * cos + partner * sin_signed


def _exp_and_inv_sum(s):
    e = jnp.exp(s - jnp.max(s, axis=-1, keepdims=True))
    return e, 1.0 / jnp.sum(e, axis=-1, keepdims=True)


def _diff_body(*refs, lk, ctx, lam_init):
    *ins, _aliased_dst, o_ref, kb, vb = refs
    if ctx:
        q_ref, k_ref, v_ref, dl_ref, g_ref, ck_ref, cv_ref, cq_ref, sq_ref, ckk_ref, skk_ref = ins
    else:
        q_ref, k_ref, v_ref, dl_ref, g_ref = ins

    @pl.when(pl.program_id(2) == 0)
    def _():
        k = k_ref[...]
        if ctx:
            k = _rope(k, ckk_ref[...], skk_ref[...])
            kb[lk:, :] = ck_ref[...].astype(BF16)
            vb[lk:, :] = cv_ref[...].astype(BF16)
        kb[0:lk, :] = k.astype(BF16)
        vb[0:lk, :] = v_ref[...].astype(BF16)

    q = q_ref[...] * DIFF_SCALE
    if ctx:
        q = _rope(q, cq_ref[...], sq_ref[...])
    lane = lax.broadcasted_iota(jnp.int32, q.shape, 1)
    q1 = jnp.where(lane < HALF, q, 0.0).astype(BF16)
    q2 = jnp.where(lane >= HALF, q, 0.0).astype(BF16)
    keys = kb[...]
    e1, inv1 = _exp_and_inv_sum(lax.dot_general(q1, keys, NT_DIMS, preferred_element_type=F32))
    e2, inv2 = _exp_and_inv_sum(lax.dot_general(q2, keys, NT_DIMS, preferred_element_type=F32))
    dl = dl_ref[...]
    lam = (jnp.exp(jnp.sum(dl[0:1] * dl[1:2], axis=-1, keepdims=True))
           - jnp.exp(jnp.sum(dl[2:3] * dl[3:4], axis=-1, keepdims=True)) + lam_init)
    o = jnp.dot((e1 * inv1 - e2 * (lam * inv2)).astype(BF16), vb[...], preferred_element_type=F32)
    o = o * lax.rsqrt(jnp.mean(o * o, axis=-1, keepdims=True) + EPS) * g_ref[...]
    o_ref[...] = o * (1.0 - lam_init)


def diff_attention(proj, dst, diff_lam, diff_subln, l, lam_init, cache_k=None, cache_v=None, tables=None):
    ctx = cache_k is not None
    nh = DIFF_HEADS
    if ctx:
        nb, lk, tq, row0 = DEC_BATCH, DEC_SEQ, TQ_DIFF, T_PROMPT
    else:
        nb, lk, tq, row0 = BATCH, SEQ, SEQ, 0
    in_specs = [pl.BlockSpec((tq, LANES), lambda b, h, qi: ((row0 + b * lk) // tq + qi, h)),
                pl.BlockSpec((lk, LANES), lambda b, h, qi: (row0 // lk + b, nh + h)),
                pl.BlockSpec((lk, LANES), lambda b, h, qi: (row0 // lk + b, 2 * nh + h)),
                pl.BlockSpec((None, 4, DIFF_QK), lambda b, h, qi: (l, 0, 0)),
                pl.BlockSpec((None, 1, DIFF_VD), lambda b, h, qi: (l, 0, 0))]
    args = [proj, proj, proj, diff_lam, diff_subln.reshape(DEPTH, 1, DIFF_VD)]
    if ctx:
        cos, sin = tables
        ctx_spec = pl.BlockSpec((None, None, PAST_LEN, LANES), lambda b, h, qi: (b, l, 0, h))
        q_tab = pl.BlockSpec((tq, LANES), lambda b, h, qi: (qi, 0))
        k_tab = pl.BlockSpec((lk, LANES), lambda b, h, qi: (0, 0))
        in_specs += [ctx_spec, ctx_spec, q_tab, q_tab, k_tab, k_tab]
        args += [cache_k.reshape(DEC_BATCH, DEPTH, PAST_LEN, GROUP_W), cache_v.reshape(DEC_BATCH, DEPTH, PAST_LEN, GROUP_W),
                 cos, sin, cos, sin]
    n_keys = lk + (PAST_LEN if ctx else 0)
    return pl.pallas_call(
        partial(_diff_body, lk=lk, ctx=ctx, lam_init=lam_init),
        grid=(nb, nh, lk // tq),
        in_specs=in_specs + [pl.BlockSpec(memory_space=pl.ANY)],
        out_specs=pl.BlockSpec((tq, LANES), lambda b, h, qi: ((row0 + b * lk) // tq + qi, h)),
        out_shape=jax.ShapeDtypeStruct((T_ALL, GROUP_W), F32),
        input_output_aliases={len(args): 0},
        scratch_shapes=[pltpu.VMEM((n_keys, LANES), BF16), pltpu.VMEM((n_keys, LANES), BF16)],
        compiler_params=_cparams("arbitrary", "arbitrary", "arbitrary"),
        name="diff_attention_ctx" if ctx else "diff_attention",
    )(*args, dst)


def _swa_body(*refs, lk, tq, banded):
    *ins, _aliased_dst, o_ref, kb, vb = refs
    if banded:
        q_ref, k_ref, v_ref, sk_ref, ck_ref, cv_ref, cq_ref, sq_ref, ckk_ref, skk_ref = ins
    else:
        q_ref, k_ref, v_ref, sk_ref = ins
    qi = pl.program_id(1)

    @pl.when(qi == 0)
    def _():
        k = k_ref[...]
        if banded:
            k = _rope(k, ckk_ref[...], skk_ref[...])
            zeros = jnp.zeros((WINDOW, LANES), BF16)
            kb[0:WINDOW, :] = zeros
            vb[0:WINDOW, :] = zeros
            kb[WINDOW + lk:2 * WINDOW + lk, :] = zeros
            vb[WINDOW + lk:2 * WINDOW + lk, :] = zeros
            kb[2 * WINDOW + lk:, :] = ck_ref[...].astype(BF16)
            vb[2 * WINDOW + lk:, :] = cv_ref[...].astype(BF16)
            kb[WINDOW:WINDOW + lk, :] = k.astype(BF16)
            vb[WINDOW:WINDOW + lk, :] = v_ref[...].astype(BF16)
        else:
            kb[...] = k.astype(BF16)
            vb[...] = v_ref[...].astype(BF16)

    lane = lax.broadcasted_iota(jnp.int32, (tq, LANES), 1)
    sinks = sk_ref[...]
    if banded:
        start = pl.multiple_of(qi * WINDOW, WINDOW)
        k_loc, v_loc = kb[pl.ds(start, 3 * WINDOW), :], vb[pl.ds(start, 3 * WINDOW), :]
        k_ctx, v_ctx = kb[2 * WINDOW + lk:, :], vb[2 * WINDOW + lk:, :]
        row = lax.broadcasted_iota(jnp.int32, (SWA_GROUP * tq, 3 * WINDOW), 0) % tq
        col = lax.broadcasted_iota(jnp.int32, (SWA_GROUP * tq, 3 * WINDOW), 1)
        kpos = (qi - 1) * WINDOW + col
        visible = (col >= row) & (col - row <= 2 * WINDOW) & (kpos >= 0) & (kpos < lk)
    else:
        k_loc, v_loc = kb[...], vb[...]
    outs = []
    for kh in range(SWA_KV_HEADS):
        in_half = (lane >= HALF) if kh else (lane < HALF)
        qs, sink_rows = [], []
        for g in range(SWA_GROUP):
            head = kh * SWA_GROUP + g
            t = q_ref[:, (head // 2) * LANES:(head // 2 + 1) * LANES] * SWA_SCALE
            if banded:
                t = _rope(t, cq_ref[...], sq_ref[...])
            if head % 2 != kh:
                t = pltpu.roll(t, HALF, 1)
            qs.append(jnp.where(in_half, t, 0.0).astype(BF16))
            sink_rows.append(jnp.broadcast_to(sinks[:, head:head + 1], (tq, 1)))
        qs = jnp.concatenate(qs, axis=0)
        sink = jnp.concatenate(sink_rows, axis=0)
        s_loc = lax.dot_general(qs, k_loc, NT_DIMS, preferred_element_type=F32)
        m = sink
        if banded:
            s_loc = jnp.where(visible, s_loc, NEG_INF)
            s_ctx = lax.dot_general(qs, k_ctx, NT_DIMS, preferred_element_type=F32)
            m = jnp.maximum(m, jnp.max(s_ctx, axis=-1, keepdims=True))
        m = jnp.maximum(m, jnp.max(s_loc, axis=-1, keepdims=True))
        e_loc = jnp.exp(s_loc - m)
        den = jnp.sum(e_loc, axis=-1, keepdims=True) + jnp.exp(sink - m)
        if banded:
            e_ctx = jnp.exp(s_ctx - m)
            den = den + jnp.sum(e_ctx, axis=-1, keepdims=True)
        inv = 1.0 / den
        o = jnp.dot((e_loc * inv).astype(BF16), v_loc, preferred_element_type=F32)
        if banded:
            o = o + jnp.dot((e_ctx * inv).astype(BF16), v_ctx, preferred_element_type=F32)
        outs.append(o)
    for t in range(SWA_HEADS // 2):
        kh, g0 = (2 * t) // SWA_GROUP, (2 * t) % SWA_GROUP
        a = outs[kh][g0 * tq:(g0 + 1) * tq]
        b = outs[kh][(g0 + 1) * tq:(g0 + 2) * tq]
        a = pltpu.roll(a, HALF, 1) if kh == 1 else a
        b = pltpu.roll(b, HALF, 1) if kh == 0 else b
        o_ref[:, t * LANES:(t + 1) * LANES] = jnp.where(lane < HALF, a, b)


def swa_attention(proj, dst, swa_sink, l, cache_k=None, cache_v=None, tables=None):
    banded = cache_k is not None
    q_col, k_col = 3, (4 * GROUP_W) // LANES
    if banded:
        nb, lk, tq, row0 = DEC_BATCH, DEC_SEQ, WINDOW, T_PROMPT
    else:
        nb, lk, tq, row0 = BATCH, SEQ, SEQ, 0
    in_specs = [pl.BlockSpec((tq, GROUP_W), lambda b, qi: ((row0 + b * lk) // tq + qi, q_col)),
                pl.BlockSpec((lk, LANES), lambda b, qi: (row0 // lk + b, k_col)),
                pl.BlockSpec((lk, LANES), lambda b, qi: (row0 // lk + b, k_col + 1)),
                pl.BlockSpec((None, 1, SWA_HEADS), lambda b, qi: (l, 0, 0))]
    args = [proj, proj, proj, swa_sink.reshape(DEPTH, 1, SWA_HEADS)]
    if banded:
        cos, sin = tables
        ctx_spec = pl.BlockSpec((None, None, PAST_LEN, LANES), lambda b, qi: (b, l, 0, 0))
        q_tab = pl.BlockSpec((tq, LANES), lambda b, qi: (qi, 0))
        k_tab = pl.BlockSpec((lk, LANES), lambda b, qi: (0, 0))
        in_specs += [ctx_spec, ctx_spec, q_tab, q_tab, k_tab, k_tab]
        args += [cache_k.reshape(DEC_BATCH, DEPTH, PAST_LEN, LANES), cache_v.reshape(DEC_BATCH, DEPTH, PAST_LEN, LANES),
                 cos, sin, cos, sin]
    n_keys = lk + 2 * WINDOW + PAST_LEN if banded else lk
    return pl.pallas_call(
        partial(_swa_body, lk=lk, tq=tq, banded=banded),
        grid=(nb, lk // tq),
        in_specs=in_specs + [pl.BlockSpec(memory_space=pl.ANY)],
        out_specs=pl.BlockSpec((tq, GROUP_W), lambda b, qi: ((row0 + b * lk) // tq + qi, 0)),
        out_shape=jax.ShapeDtypeStruct((T_ALL, GROUP_W), F32),
        input_output_aliases={len(args): 0},
        scratch_shapes=[pltpu.VMEM((n_keys, LANES), BF16), pltpu.VMEM((n_keys, LANES), BF16)],
        compiler_params=_cparams("arbitrary", "arbitrary"),
        name="swa_attention_banded" if banded else "swa_attention",
    )(*args, dst)


LRU_GROUPS = LRU_W // LANES


def lru_gate_params(wa, ba, wx, bx):
    def tile_blocks(w):
        w = w.reshape(LRU_GROUPS, 2, LRU_BD, LRU_BD)
        z = jnp.zeros((LRU_GROUPS, LRU_BD, LRU_BD), F32)
        return jnp.concatenate([jnp.concatenate([w[:, 0], z], axis=-1), jnp.concatenate([z, w[:, 1]], axis=-1)], axis=-2)

    w = jnp.concatenate([tile_blocks(wa[0]), tile_blocks(wx[0]), tile_blocks(wa[1]), tile_blocks(wx[1])], axis=-1)
    b = jnp.stack([ba[0], bx[0], ba[1], bx[1]], axis=0).reshape(4, LRU_GROUPS, LANES)
    return w, jnp.moveaxis(b, 0, 1).reshape(LRU_GROUPS, 1, 4 * LANES)


def _shift_rows(x, s, row, n):
    if s == 0:
        return x
    ok = (row >= -s) if s < 0 else (row < n - s)
    return jnp.where(ok, pltpu.roll(x, (-s) % n, 0), 0.0)


def _linear_scan(a, b, row, n, reverse):
    s = 1
    while s < n:
        ok = (row < n - s) if reverse else (row >= s)
        shift = (n - s) if reverse else s
        b = jnp.where(ok, a * pltpu.roll(b, shift, 0) + b, b)
        a = jnp.where(ok, a * pltpu.roll(a, shift, 0), a)
        s *= 2
    return b


def _lru_body(x_ref, lg_ref, cw_ref, w_ref, bias_ref, lam_ref, h0_ref, _aliased_dst, o_ref, st_ref, *, n):
    x = x_ref[...]
    row = lax.broadcasted_iota(jnp.int32, x.shape, 0)
    cw = cw_ref[...]
    xc = _shift_rows(x, -(LRU_CONV // 2), row, n) * cw[0:1]
    for k in range(1, LRU_CONV):
        xc = xc + _shift_rows(x, k - LRU_CONV // 2, row, n) * cw[k:k + 1]
    gates = jnp.dot(xc.astype(BF16), w_ref[...].astype(BF16), preferred_element_type=F32) + bias_ref[...]
    lam = lam_ref[...]
    log_sig = jnp.minimum(lam, 0.0) - jnp.log1p(jnp.exp(-jnp.abs(lam)))
    h0 = h0_ref[...]
    hs = []
    for d in range(2):
        r = jax.nn.sigmoid(gates[:, (2 * d) * LANES:(2 * d + 1) * LANES])
        i = jax.nn.sigmoid(gates[:, (2 * d + 1) * LANES:(2 * d + 2) * LANES])
        log_a = LRU_C * r * log_sig[d:d + 1]
        a = jnp.exp(log_a)
        th = jnp.tanh(log_a)
        b = jnp.sqrt(-2.0 * th / (1.0 - th)) * (i * xc)
        first = n - 1 if d else 0
        b = jnp.where(row == first, b + a * h0[d:d + 1], b)
        hs.append(_linear_scan(a, b, row, n, reverse=bool(d)))
    lg = lg_ref[...]
    gelu = 0.5 * lg * (1.0 + jnp.tanh(math.sqrt(2.0 / math.pi) * (lg + 0.044715 * (lg * lg * lg))))
    o_ref[...] = (hs[0] + hs[1]) * gelu
    st_ref[0:1, :] = hs[0][n - 1:n, :]
    st_ref[1:2, :] = hs[1][0:1, :]


def rglru(proj, dst, gate_w, gate_b, lru_conv, lru_lam, h0, l, h0_l, prompt):
    nb, n, row0 = (BATCH, SEQ, 0) if prompt else (DEC_BATCH, DEC_SEQ, T_PROMPT)
    x_col = (D_IN - 2 * LRU_W) // LANES
    return pl.pallas_call(
        partial(_lru_body, n=n),
        grid=(nb, LRU_GROUPS),
        in_specs=[pl.BlockSpec((n, LANES), lambda s, c: (row0 // n + s, x_col + c)),
                  pl.BlockSpec((n, LANES), lambda s, c: (row0 // n + s, x_col + LRU_GROUPS + c)),
                  pl.BlockSpec((None, LRU_CONV, LANES), lambda s, c: (l, 0, c)),
                  pl.BlockSpec((None, LANES, 4 * LANES), lambda s, c: (c, 0, 0)),
                  pl.BlockSpec((None, 1, 4 * LANES), lambda s, c: (c, 0, 0)),
                  pl.BlockSpec((None, 2, LANES), lambda s, c: (l, 0, c)),
                  pl.BlockSpec((None, None, 2, LANES), lambda s, c: (s, h0_l, 0, c)),
                  pl.BlockSpec(memory_space=pl.ANY)],
        out_specs=[pl.BlockSpec((n, LANES), lambda s, c: (row0 // n + s, c)),
                   pl.BlockSpec((None, 2, LANES), lambda s, c: (s, 0, c))],
        out_shape=[jax.ShapeDtypeStruct((T_ALL, LRU_W), F32), jax.ShapeDtypeStruct((nb, 2, LRU_W), F32)],
        input_output_aliases={7: 0},
        compiler_params=_cparams("arbitrary", "arbitrary"),
        name="rglru_prompt" if prompt else "rglru_sample",
    )(proj, proj, lru_conv, gate_w, gate_b, lru_lam, h0, dst)


HY_P = 256
HY_F = 2 * HY_P
HY_ACC_VREGS = 8
HY_FILTER_ROWS = 256


def _split_bf16(x):
    hi = x.astype(BF16)
    return hi, (x - hi.astype(F32)).astype(BF16)


def _dot3(c_hi, c_lo, d):
    d_hi, d_lo = _split_bf16(d)
    return (jnp.dot(c_hi, d_hi, preferred_element_type=F32) + jnp.dot(c_hi, d_lo, preferred_element_type=F32)
            + jnp.dot(c_lo, d_hi, preferred_element_type=F32))


def dft_constants():
    f = np.arange(HY_F)[:, None].astype(np.float64)
    ang = 2.0 * np.pi * f * np.arange(HY_F)[None, :] / HY_F
    c, s = np.cos(ang), np.sin(ang)
    ch, sh = c[:, :HY_P], s[:, :HY_P]
    fwd_data = np.block([[ch, sh], [-sh, ch]])
    fwd_filter = np.concatenate([c, -s], axis=0)
    inverse = np.block([[ch.T, -sh.T], [sh.T, ch.T]]) / HY_F
    out = []
    for m in (fwd_data, fwd_filter, inverse):
        hi = m.astype(np.float32).astype(BF16)
        lo = (m - hi.astype(np.float64)).astype(np.float32).astype(BF16)
        out += [jnp.asarray(hi), jnp.asarray(lo)]
    return out


def _filter_spectra_body(cur_ref, prev_ref, fh_ref, fl_ref, g_ref):
    row = lax.broadcasted_iota(jnp.int32, prev_ref.shape, 0)
    prev = jnp.where((row == 0) | (pl.program_id(1) == 0), 0.0, prev_ref[...])
    g_ref[...] = _dot3(fh_ref[...], fl_ref[...], jnp.concatenate([cur_ref[...], prev], axis=0))


def filter_spectra(filt, fk_hi, fk_lo):
    nq, n, _ = filt.shape
    nblk = n // HY_P
    const = pl.BlockSpec((2 * HY_F, HY_F), lambda q, m: (0, 0))
    return pl.pallas_call(
        _filter_spectra_body,
        grid=(nq, nblk),
        in_specs=[pl.BlockSpec((None, HY_P, HY_W), lambda q, m: (q, m, 0)),
                  pl.BlockSpec((None, HY_P, HY_W), lambda q, m: (q, jnp.maximum(m - 1, 0), 0)),
                  const, const],
        out_specs=pl.BlockSpec((None, None, 2 * HY_F, HY_W), lambda q, m: (q, m, 0, 0)),
        out_shape=jax.ShapeDtypeStruct((nq, nblk, 2 * HY_F, HY_W), F32),
        compiler_params=_cparams("arbitrary", "arbitrary"),
        name="hyena_filter_spectra",
    )(filt, filt, fk_hi, fk_lo)


def _hyena_body(z0_ref, z1_ref, x0_ref, x1_ref, cwz_ref, cwx_ref, skip_ref, gf_ref, gb_ref, fdh_ref, fdl_ref,
                gih_ref, gil_ref, o_ref, zs, xs, us, ys, *, n, conv_z):
    nblk = n // HY_P
    row = lax.broadcasted_iota(jnp.int32, z0_ref.shape, 0)

    def short_conv(x, cw):
        acc = _shift_rows(x, -(HY_SHORT // 2), row, n) * cw[0:1]
        for k in range(1, HY_SHORT):
            acc = acc + _shift_rows(x, k - HY_SHORT // 2, row, n) * cw[k:k + 1]
        return acc

    for b, (z_ref, x_ref) in enumerate(((z0_ref, x0_ref), (z1_ref, x1_ref))):
        zs[b] = short_conv(z_ref[...], cwz_ref[...]) if conv_z else z_ref[...]
        xs[b] = short_conv(x_ref[...], cwx_ref[...])

    def forward(j, carry):
        r0 = pl.multiple_of(j * HY_P, HY_P)
        d = jnp.concatenate([zs[0, pl.ds(r0, HY_P), :], zs[1, pl.ds(r0, HY_P), :]], axis=0)
        us[j] = _dot3(fdh_ref[...], fdl_ref[...], d)
        return carry

    lax.fori_loop(0, nblk, forward, 0)

    lanes = z0_ref.shape[1]
    fc = HY_ACC_VREGS * 8 * LANES // lanes

    def out_block(i, carry):
        for c0 in range(0, HY_F, fc):
            re, im = pl.ds(c0, fc), pl.ds(HY_F + c0, fc)

            def causal(j, acc):
                g_re, g_im, u_re, u_im = gf_ref[i - j, re, :], gf_ref[i - j, im, :], us[j, re, :], us[j, im, :]
                return acc[0] + (g_re * u_re - g_im * u_im), acc[1] + (g_re * u_im + g_im * u_re)

            def anticausal(j, acc):
                g_re, g_im, u_re, u_im = gb_ref[j - i, re, :], gb_ref[j - i, im, :], us[j, re, :], us[j, im, :]
                return acc[0] + (g_re * u_re + g_im * u_im), acc[1] + (g_re * u_im - g_im * u_re)

            zero = jnp.zeros((fc, lanes), F32)
            acc = lax.fori_loop(0, i + 1, causal, (zero, zero))
            acc = lax.fori_loop(i, nblk, anticausal, acc)
            ys[re, :] = acc[0]
            ys[im, :] = acc[1]
        y = _dot3(gih_ref[...], gil_ref[...], ys[...])
        r0 = pl.multiple_of(i * HY_P, HY_P)
        for b in range(2):
            conv = y[b * HY_P:(b + 1) * HY_P]
            o_ref[pl.ds(b * n + r0, HY_P), :] = xs[b, pl.ds(r0, HY_P), :] * (conv + skip_ref[...] * zs[b, pl.ds(r0, HY_P), :])
        return carry

    lax.fori_loop(0, nblk, out_block, 0)


def hyena_order(zsrc, z_col0, proj, x_col0, hy_conv, hy_skip, spectra, consts, l, order, prompt):
    fd_hi, fd_lo, _, _, gi_hi, gi_lo = consts
    if prompt:
        n, cw, npair, z_row0 = SEQ, 2 * LANES, BATCH // 2, 0
    else:
        n, cw, npair, z_row0 = DEC_SEQ, LANES, DEC_BATCH // 2, (T_PROMPT // DEC_SEQ if order == 0 else 0)
    x_row0 = 0 if prompt else T_PROMPT // DEC_SEQ
    ncg = HY_W // cw
    nblk = n // HY_P
    hy_col = sum(IN_SIZES[:6])
    once = pl.Buffered(1)
    seq = lambda row0, col0, b: pl.BlockSpec((n, cw), lambda p, c: (row0 + 2 * p + b, col0 // cw + c), pipeline_mode=once)
    taps = lambda col0: pl.BlockSpec((None, HY_SHORT, cw), lambda p, c: (l, 0, col0 // cw + c))
    spec_blk = lambda q: pl.BlockSpec((None, nblk, 2 * HY_F, cw), lambda p, c: (q, 0, 0, c), pipeline_mode=once)
    const = lambda shape: pl.BlockSpec(shape, lambda p, c: (0, 0))
    return pl.pallas_call(
        partial(_hyena_body, n=n, conv_z=(order == 0)),
        grid=(npair, ncg),
        in_specs=[seq(z_row0, z_col0, 0), seq(z_row0, z_col0, 1), seq(x_row0, x_col0, 0), seq(x_row0, x_col0, 1),
                  taps(0), taps(x_col0 - hy_col),
                  pl.BlockSpec((None, 1, cw), lambda p, c: (l * HY_ORDER + order, 0, c)),
                  spec_blk(2 * order), spec_blk(2 * order + 1),
                  const((2 * HY_F, HY_F)), const((2 * HY_F, HY_F)), const((HY_F, 2 * HY_F)), const((HY_F, 2 * HY_F))],
        out_specs=pl.BlockSpec((2 * n, cw), lambda p, c: (p, c)),
        out_shape=jax.ShapeDtypeStruct((npair * 2 * n, HY_W), F32),
        scratch_shapes=[pltpu.VMEM((2, n, cw), F32), pltpu.VMEM((2, n, cw), F32), pltpu.VMEM((nblk, 2 * HY_F, cw), F32),
                        pltpu.VMEM((2 * HY_F, cw), F32)],
        compiler_params=_cparams("arbitrary", "arbitrary"),
        name=f"hyena_{'prompt' if prompt else 'sample'}_order{order}",
    )(zsrc, zsrc, proj, proj, hy_conv, hy_conv, hy_skip.reshape(DEPTH * HY_ORDER, 1, HY_W), spectra, spectra,
      fd_hi, fd_lo, gi_hi, gi_lo)


def _hyena_filter_body(w1_ref, b1_ref, w2_ref, b2_ref, fr_ref, w3f_ref, w3b_ref, rf_ref, rb_ref, o_ref, h_scr, *, n):
    hp = lax.Precision.HIGHEST
    ch = min(n, HY_FILTER_ROWS)
    c = lax.broadcasted_iota(jnp.int32, (ch, LANES), 1)
    band = jnp.where(c <= HY_BANDS, c, c - HY_BANDS).astype(F32)
    fr = fr_ref[...]

    def positions(i):
        r0 = pl.multiple_of(i * ch, ch)
        return r0, (lax.broadcasted_iota(jnp.int32, (ch, LANES), 0) + r0).astype(F32) / n

    @pl.when((pl.program_id(0) == 0) & (pl.program_id(1) == 0))
    def _():
        def hidden(i, carry):
            r0, t = positions(i)
            ang = 2.0 * math.pi * t * band
            z = jnp.where(c == 0, t, jnp.where(c <= HY_BANDS, jnp.sin(ang), jnp.where(c < HY_POS_DIM, jnp.cos(ang), 0.0)))
            h = jnp.sin(fr[0:1] * (jnp.dot(z, w1_ref[...], preferred_element_type=F32, precision=hp) + b1_ref[...]))
            h = jnp.sin(fr[1:2] * (jnp.dot(h, w2_ref[...], preferred_element_type=F32, precision=hp) + b2_ref[...]))
            h_scr[pl.ds(r0, ch), :] = h
            return carry

        lax.fori_loop(0, n // ch, hidden, 0)

    def raw(i, sumsq):
        r0, t = positions(i)
        h = h_scr[pl.ds(r0, ch), :]
        tcol = t[:, 0:1]
        f = jnp.dot(h, w3f_ref[...], preferred_element_type=F32, precision=hp) * jnp.exp(-tcol * jnp.exp(rf_ref[...]))
        b = jnp.dot(h, w3b_ref[...], preferred_element_type=F32, precision=hp) * jnp.exp(-tcol * jnp.exp(rb_ref[...]))
        o_ref[0, pl.ds(r0, ch), :] = f
        o_ref[1, pl.ds(r0, ch), :] = b
        return sumsq + jnp.sum(f * f, axis=0, keepdims=True) + jnp.sum(b * b, axis=0, keepdims=True)

    scale = lax.rsqrt(lax.fori_loop(0, n // ch, raw, jnp.zeros((1, LANES), F32)) + EPS)

    def normalise(i, carry):
        r0 = pl.multiple_of(i * ch, ch)
        f, b = o_ref[0, pl.ds(r0, ch), :] * scale, o_ref[1, pl.ds(r0, ch), :] * scale
        first = lax.broadcasted_iota(jnp.int32, (ch, LANES), 0) + r0 == 0
        o_ref[0, pl.ds(r0, ch), :] = jnp.where(first, f + b, f)
        o_ref[1, pl.ds(r0, ch), :] = jnp.where(first, 0.0, b)
        return carry

    lax.fori_loop(0, n // ch, normalise, 0)


def hyena_filters_onesided(n, hy_w1, hy_b1, hy_w2, hy_b2, hy_w3, hy_freq, hy_log_rate, l):
    nt = HY_W // LANES
    w1 = jnp.pad(hy_w1, ((0, 0), (0, LANES - HY_POS_DIM), (0, 0)))
    vec = lambda a: a.reshape(DEPTH, 1, HY_HID)
    col = lambda d: (lambda o, w: (l, 0, (2 * o + d) * nt + w))
    whole = lambda shape: pl.BlockSpec((None,) + shape, lambda o, w: (l,) + (0,) * len(shape))
    return pl.pallas_call(
        partial(_hyena_filter_body, n=n),
        grid=(HY_ORDER, nt),
        in_specs=[whole((LANES, HY_HID)), whole((1, HY_HID)), whole((HY_HID, HY_HID)), whole((1, HY_HID)),
                  whole((2, HY_HID)),
                  pl.BlockSpec((None, HY_HID, LANES), col(0)), pl.BlockSpec((None, HY_HID, LANES), col(1)),
                  pl.BlockSpec((None, 1, LANES), col(0)), pl.BlockSpec((None, 1, LANES), col(1))],
        out_specs=pl.BlockSpec((None, 2, n, LANES), lambda o, w: (o, 0, 0, w)),
        out_shape=jax.ShapeDtypeStruct((HY_ORDER, 2, n, HY_W), F32),
        scratch_shapes=[pltpu.VMEM((n, HY_HID), F32)],
        compiler_params=_cparams("arbitrary", "arbitrary"),
        name="hyena_filters",
    )(w1, vec(hy_b1), hy_w2, vec(hy_b2), hy_freq, hy_w3, hy_w3,
      hy_log_rate.reshape(DEPTH, 1, 2 * HY_ORDER * HY_W), hy_log_rate.reshape(DEPTH, 1, 2 * HY_ORDER * HY_W)
      ).reshape(2 * HY_ORDER, n, HY_W)


def hyena_group(proj, filt4, hy_conv, hy_skip, consts, l, prompt):
    spectra = filter_spectra(filt4, consts[2], consts[3])
    hy_col = sum(IN_SIZES[:6])
    z = hyena_order(proj, hy_col, proj, hy_col + HY_W, hy_conv, hy_skip, spectra, consts, l, 0, prompt)
    return hyena_order(z, 0, proj, hy_col + 2 * HY_W, hy_conv, hy_skip, spectra, consts, l, 1, prompt)


def moe_layer(x1, mod, p, l):
    h2, idx, gates, rank, counts = router(x1, p['norm2'], mod, p['router_w'], p['router_bias'], l)
    idx, rank = idx[:, :TOP_K], rank[:, :TOP_K]
    nblk = (counts[0] + TM_E - 1) // TM_E
    blk_end = jnp.cumsum(nblk)
    row_start = (blk_end - nblk) * TM_E
    dest = row_start[idx] + rank
    n_valid = blk_end[-1:].astype(jnp.int32)
    blk = jnp.minimum(jnp.arange(N_BLK_MAX, dtype=jnp.int32), n_valid[0] - 1)
    blk_e = jnp.minimum(jnp.sum(blk_end[None, :] <= blk[:, None], axis=-1), N_EXPERTS - 1).astype(jnp.int32)
    tok = jnp.broadcast_to(jnp.arange(T_ALL, dtype=jnp.int32)[:, None], dest.shape)
    src = jnp.zeros((N_BLK_MAX * TM_E,), jnp.int32).at[dest.reshape(-1)].set(tok.reshape(-1))
    ybuf = routed_experts(h2, src, blk_e, n_valid, p['moe_w_gu'], p['moe_w_dn'], l)
    return shared_combine(h2, p['sh_w_gu'], p['sh_w_dn'], ybuf, dest, gates, x1, mod, l)


def kernel(x_prompt, x_sample, cache_diff_k, cache_diff_v, cache_swa_k, cache_swa_v, state_lru, c, c_ctx,
           w_mod, b_mod, norm1, norm2, w_in, w_out, diff_lam, diff_subln, swa_sink, hy_conv, hy_w1, hy_b1,
           hy_w2, hy_b2, hy_w3, hy_freq, hy_log_rate, hy_skip, lru_conv, lru_wa, lru_ba, lru_wx, lru_bx, lru_lam,
           router_w, router_bias, moe_w_gu, moe_w_dn, sh_w_gu, sh_w_dn, final_norm):
    x = jnp.concatenate([x_prompt.reshape(T_PROMPT, D_MODEL), x_sample.reshape(T_SAMPLE, D_MODEL)], axis=0)
    cond = jnp.concatenate([c_ctx[None, :], c, jnp.zeros((8 - N_COND, D_MODEL), F32)], axis=0)
    dk_l, dv_l, sk_l, sv_l, st_l = [], [], [], [], []
    tables = rope_tables(DEC_SEQ)
    consts = dft_constants()
    h0_prompt = jnp.zeros((BATCH, 1, 2, LRU_W), F32)
    col = np.cumsum((0,) + IN_SIZES).tolist()
    empty = lambda: jnp.zeros((T_ALL, GROUP_W), F32)
    for l in range(DEPTH):
        p = {'norm2': norm2, 'router_w': router_w, 'router_bias': router_bias, 'moe_w_gu': moe_w_gu,
             'moe_w_dn': moe_w_dn, 'sh_w_gu': sh_w_gu, 'sh_w_dn': sh_w_dn}
        lam_init = 0.8 - 0.6 * math.exp(-0.3 * l)
        mod = modulation(cond, w_mod, b_mod, l)[:N_COND].reshape(N_COND * 6, 1, D_MODEL)
        proj = in_proj(x, norm1, mod, w_in, l)
        proj_p = proj[:T_PROMPT].reshape(BATCH, SEQ, D_IN)
        dk_l.append(proj_p[..., col[1]:col[2]].reshape(BATCH, SEQ, DIFF_HEADS, 2, DIFF_QK))
        dv_l.append(proj_p[..., col[2]:col[3]].reshape(BATCH, SEQ, DIFF_HEADS, DIFF_VD))
        sk_l.append(proj_p[..., col[4]:col[5]].reshape(BATCH, SEQ, SWA_KV_HEADS, HEAD_DIM))
        sv_l.append(proj_p[..., col[5]:col[6]].reshape(BATCH, SEQ, SWA_KV_HEADS, HEAD_DIM))
        d_out = diff_attention(proj, empty(), diff_lam, diff_subln, l, lam_init)
        d_out = diff_attention(proj, d_out, diff_lam, diff_subln, l, lam_init, cache_diff_k, cache_diff_v, tables)
        s_out = swa_attention(proj, empty(), swa_sink, l)
        s_out = swa_attention(proj, s_out, swa_sink, l, cache_swa_k, cache_swa_v, tables)
        hy_args = (hy_w1, hy_b1, hy_w2, hy_b2, hy_w3, hy_freq, hy_log_rate, l)
        hy_out = jnp.concatenate(
            [hyena_group(proj, hyena_filters_onesided(SEQ, *hy_args), hy_conv, hy_skip, consts, l, True),
             hyena_group(proj, hyena_filters_onesided(DEC_SEQ, *hy_args), hy_conv, hy_skip, consts, l, False)], axis=0)
        gate_w, gate_b = lru_gate_params(lru_wa[l], lru_ba[l], lru_wx[l], lru_bx[l])
        lru_out, st = rglru(proj, empty(), gate_w, gate_b, lru_conv, lru_lam, h0_prompt, l, 0, True)
        lru_out, _ = rglru(proj, lru_out, gate_w, gate_b, lru_conv, lru_lam, state_lru, l, l, False)
        st_l.append(st)
        x1 = out_proj([d_out, s_out, hy_out, lru_out], w_out, x, mod, l)
        x = moe_layer(x1, mod, p, l)
    y = final_rmsnorm(x, final_norm)
    y_prompt = y[:T_PROMPT].reshape(BATCH, SEQ, D_MODEL)
    y_sample = y[T_PROMPT:].reshape(DEC_BATCH, DEC_SEQ, D_MODEL)
    return (y_prompt, y_sample, jnp.stack(dk_l, axis=1), jnp.stack(dv_l, axis=1), jnp.stack(sk_l, axis=1),
            jnp.stack(sv_l, axis=1), jnp.stack(st_l, axis=1))
```

```python
import math
from functools import partial
import jax, jax.numpy as jnp
from jax import lax
import numpy as np
from jax.experimental import pallas as pl
from jax.experimental.pallas import tpu as pltpu

D_MODEL = 2048
BATCH = 16
SEQ = 256
DEPTH = 2
DEC_BATCH = 2
DEC_SEQ = 4096
PAST_LEN = 256

GRID_W = 64
EPS = 1e-6
F32 = jnp.float32
GROUP_W = D_MODEL // 4
D_MIX = 4 * GROUP_W
DIFF_QK = 64
DIFF_VD = 2 * DIFF_QK
DIFF_HEADS = GROUP_W // DIFF_VD
DIFF_SCALE = DIFF_QK ** -0.5
HEAD_DIM = 64
SWA_HEADS = GROUP_W // HEAD_DIM
SWA_KV_HEADS = SWA_HEADS // 4
SWA_GROUP = SWA_HEADS // SWA_KV_HEADS
SWA_SCALE = HEAD_DIM ** -0.5
WINDOW = 128
ROPE_PAIRS = HEAD_DIM // 4
ROPE_BASE = 10000.0
HY_W = GROUP_W
HY_ORDER = 2
HY_SHORT = 3
HY_BANDS = 16
HY_POS_DIM = 1 + 2 * HY_BANDS
HY_HID = 64
LRU_W = GROUP_W
LRU_BLOCKS = 8
LRU_BD = LRU_W // LRU_BLOCKS
LRU_CONV = 4
LRU_C = 8.0
N_EXPERTS = 64
TOP_K = 6
EXPERT_FF = D_MODEL // 4
SHARED_FF = EXPERT_FF
ROUTE_SCALE = 2.5
NEG_INF = -1e30
IN_SIZES = (2 * DIFF_HEADS * DIFF_QK, 2 * DIFF_HEADS * DIFF_QK, DIFF_HEADS * DIFF_VD, SWA_HEADS * HEAD_DIM, SWA_KV_HEADS * HEAD_DIM, SWA_KV_HEADS * HEAD_DIM, (HY_ORDER + 1) * HY_W, LRU_W, LRU_W)
D_IN = sum(IN_SIZES)


BF16 = jnp.bfloat16
T_PROMPT = BATCH * SEQ
T_SAMPLE = DEC_BATCH * DEC_SEQ
T_ALL = T_PROMPT + T_SAMPLE
N_COND = 1 + DEC_BATCH
TM = 1024
TM_IN = 2048
TM_SH = 256
TN_IN = 256
TN_OUT = 512
TN_MOD = 1024
TM_E = 256
N_ASSIGN = T_ALL * TOP_K
N_BLK_MAX = N_ASSIGN // TM_E + N_EXPERTS
LANES = 128
VMEM_LIMIT = 56 * 1024 * 1024
VMEM_LIMIT_IN_PROJ = 60 * 1024 * 1024


def _cparams(*sem):
    return pltpu.CompilerParams(dimension_semantics=sem, vmem_limit_bytes=VMEM_LIMIT)


def _cond_row(i, tm):
    return jnp.where(i < T_PROMPT // tm, 0, 1 + (i - T_PROMPT // tm) // (DEC_SEQ // tm))


def _mod_spec(which, tm, tn=D_MODEL):
    if tn == D_MODEL:
        return pl.BlockSpec((1, 1, D_MODEL), lambda i, *_: (_cond_row(i, tm) * 6 + which, 0, 0))
    return pl.BlockSpec((1, 1, tn), lambda i, j: (_cond_row(i, tm) * 6 + which, 0, j))


def _mod_body(c_ref, *rest):
    w_refs, (b_ref, o_ref) = rest[:W_STREAMS], rest[W_STREAMS:]
    c = c_ref[...]
    a = (c * jax.nn.sigmoid(c)).astype(BF16)
    o_ref[...] = _dot_k_slabs(a, w_refs) + b_ref[0]


def modulation(cond, w_mod, b_mod, l):
    n = 6 * D_MODEL
    return pl.pallas_call(
        _mod_body,
        grid=(n // TN_MOD,),
        in_specs=[pl.BlockSpec((8, D_MODEL), lambda j: (0, 0))]
                 + [pl.BlockSpec((None, D_MODEL // W_STREAMS, TN_MOD), lambda j, s=s: (l, s, j)) for s in range(W_STREAMS)]
                 + [pl.BlockSpec((None, 1, TN_MOD), lambda j: (l, 0, j))],
        out_specs=pl.BlockSpec((8, TN_MOD), lambda j: (0, j)),
        out_shape=jax.ShapeDtypeStruct((8, n), F32),
        compiler_params=_cparams("arbitrary"),
        name="modulation",
    )(cond, *([w_mod] * W_STREAMS), b_mod.reshape(DEPTH, 1, n))


def _norm_mod(x, g, sc, sh):
    y = x * lax.rsqrt(jnp.mean(x * x, axis=-1, keepdims=True) + EPS) * g
    return y * (1.0 + sc) + sh


W_STREAMS = 4


def _k_slab_specs(rows, cols, index):
    return [pl.BlockSpec((None, rows // W_STREAMS, cols), lambda i, j, s=s: (index(i, j)[0], s, index(i, j)[1]))
            for s in range(W_STREAMS)]


def _dot_k_slabs(a, w_refs):
    k = a.shape[1] // len(w_refs)
    acc = jnp.dot(a[:, 0:k], w_refs[0][...].astype(BF16), preferred_element_type=F32)
    for s in range(1, len(w_refs)):
        acc = acc + jnp.dot(a[:, s * k:(s + 1) * k], w_refs[s][...].astype(BF16), preferred_element_type=F32)
    return acc


def _in_proj_body(x_ref, g_ref, sc_ref, sh_ref, *rest):
    w_refs, (o_ref, hb_ref) = rest[:W_STREAMS], rest[W_STREAMS:]

    @pl.when(pl.program_id(1) == 0)
    def _():
        for r in range(0, TM_IN, TM_IN // 4):
            rows = slice(r, r + TM_IN // 4)
            hb_ref[rows, :] = _norm_mod(x_ref[rows, :], g_ref[0], sc_ref[0], sh_ref[0]).astype(BF16)

    o_ref[...] = _dot_k_slabs(hb_ref[...], w_refs)


def in_proj(x, norm1, mod, w_in, l):
    return pl.pallas_call(
        _in_proj_body,
        grid=(T_ALL // TM_IN, D_IN // TN_IN),
        in_specs=[pl.BlockSpec((TM_IN, D_MODEL), lambda i, j: (i, 0)),
                  pl.BlockSpec((None, 1, D_MODEL), lambda i, j: (l, 0, 0)),
                  _mod_spec(1, TM_IN), _mod_spec(0, TM_IN)] + _k_slab_specs(D_MODEL, TN_IN, lambda i, j: (l, j)),
        out_specs=pl.BlockSpec((TM_IN, TN_IN), lambda i, j: (i, j)),
        out_shape=jax.ShapeDtypeStruct((T_ALL, D_IN), F32),
        scratch_shapes=[pltpu.VMEM((TM_IN, D_MODEL), BF16)],
        compiler_params=pltpu.CompilerParams(dimension_semantics=("arbitrary", "arbitrary"),
                                             vmem_limit_bytes=VMEM_LIMIT_IN_PROJ),
        name="in_proj",
    )(x, norm1.reshape(DEPTH, 1, D_MODEL), mod, mod, *([w_in] * W_STREAMS))


N_MIX = D_MIX // GROUP_W


def _out_proj_body(*refs):
    m_refs, (w_ref, x_ref, g1_ref, o_ref, mb_ref) = refs[:N_MIX], refs[N_MIX:]

    @pl.when(pl.program_id(1) == 0)
    def _():
        for g in range(N_MIX):
            mb_ref[g] = m_refs[g][...].astype(BF16)

    acc = jnp.dot(mb_ref[0], w_ref[0:GROUP_W, :].astype(BF16), preferred_element_type=F32)
    for g in range(1, N_MIX):
        acc = acc + jnp.dot(mb_ref[g], w_ref[g * GROUP_W:(g + 1) * GROUP_W, :].astype(BF16), preferred_element_type=F32)
    o_ref[...] = x_ref[...] + g1_ref[0] * acc


def out_proj(mixes, w_out, x, mod, l):
    return pl.pallas_call(
        _out_proj_body,
        grid=(T_ALL // TM, D_MODEL // TN_OUT),
        in_specs=[pl.BlockSpec((TM, GROUP_W), lambda i, j: (i, 0))] * N_MIX + [
                  pl.BlockSpec((None, D_MIX, TN_OUT), lambda i, j: (l, 0, j)),
                  pl.BlockSpec((TM, TN_OUT), lambda i, j: (i, j)),
                  _mod_spec(2, TM, TN_OUT)],
        out_specs=pl.BlockSpec((TM, TN_OUT), lambda i, j: (i, j)),
        out_shape=jax.ShapeDtypeStruct((T_ALL, D_MODEL), F32),
        scratch_shapes=[pltpu.VMEM((N_MIX, TM, GROUP_W), BF16)],
        compiler_params=_cparams("arbitrary", "arbitrary"),
        name="out_proj",
    )(*mixes, w_out, x, mod)


def _router_body(x_ref, g_ref, sc_ref, sh_ref, wr_ref, rb_ref, h_ref, idx_ref, gate_ref, rank_ref, cnt_ref, carry_ref):
    i = pl.program_id(0)

    @pl.when(i == 0)
    def _():
        carry_ref[...] = jnp.zeros_like(carry_ref)

    h = _norm_mod(x_ref[...], g_ref[0], sc_ref[0], sh_ref[0])
    h_ref[...] = h
    h_hi, h_lo = _split_bf16(h)
    w_hi, w_lo = _split_bf16(wr_ref[...])
    logits = (jnp.dot(h_hi, w_hi, preferred_element_type=F32) + jnp.dot(h_lo, w_hi, preferred_element_type=F32)
              + jnp.dot(h_hi, w_lo, preferred_element_type=F32))
    s = jax.nn.sigmoid(logits)
    cur = s + rb_ref[0]
    e_iota = lax.broadcasted_iota(jnp.int32, s.shape, 1).astype(F32)
    lane = lax.broadcasted_iota(jnp.int32, (TM, LANES), 1)
    r_iota = lax.broadcasted_iota(jnp.int32, (TM, TM), 0)
    c_iota = lax.broadcasted_iota(jnp.int32, (TM, TM), 1)
    tri = (c_iota <= r_iota).astype(BF16)
    idx_out = jnp.zeros((TM, LANES), F32)
    gate_out = jnp.zeros((TM, LANES), F32)
    rank_out = jnp.zeros((TM, LANES), F32)
    gsum = jnp.zeros((TM, 1), F32)
    picks = []
    for k in range(TOP_K):
        m = jnp.max(cur, axis=-1, keepdims=True)
        ek = jnp.min(jnp.where(cur == m, e_iota, float(N_EXPERTS)), axis=-1, keepdims=True)
        hit = e_iota == ek
        gk = jnp.sum(jnp.where(hit, s, 0.0), axis=-1, keepdims=True)
        cur = jnp.where(hit, -jnp.inf, cur)
        picks.append(ek)
        gsum = gsum + gk
        idx_out = jnp.where(lane == k, ek, idx_out)
        gate_out = jnp.where(lane == k, gk, gate_out)
    lane_f = lane.astype(F32)
    hits = [(lane_f == picks[2 * p], lane_f == picks[2 * p + 1] + float(N_EXPERTS)) for p in range(TOP_K // 2)]
    onehots = jnp.concatenate([(a | b).astype(BF16) for a, b in hits], axis=1)
    cum_all = jnp.dot(tri, onehots, preferred_element_type=F32)
    carry = carry_ref[...]
    for p, (hit_a, hit_b) in enumerate(hits):
        cum = cum_all[:, p * LANES:(p + 1) * LANES]
        last = cum[TM - 1:TM, :]
        swapped = pltpu.roll(last, N_EXPERTS, 1)
        ra = jnp.sum(jnp.where(hit_a, carry + cum, 0.0), axis=-1, keepdims=True) - 1.0
        rb = jnp.sum(jnp.where(hit_b, carry + swapped + cum, 0.0), axis=-1, keepdims=True) - 1.0
        carry = carry + last + swapped
        rank_out = jnp.where(lane == 2 * p, ra, jnp.where(lane == 2 * p + 1, rb, rank_out))
    carry_ref[...] = carry
    idx_ref[...] = idx_out.astype(jnp.int32)
    gate_ref[...] = ROUTE_SCALE * gate_out / gsum
    rank_ref[...] = rank_out.astype(jnp.int32)
    cnt_ref[...] = carry[:, :N_EXPERTS].astype(jnp.int32)


def router(x1, norm2, mod, router_w, router_bias, l):
    tok_out = lambda dt: jax.ShapeDtypeStruct((T_ALL, LANES), dt)
    tok_spec = pl.BlockSpec((TM, LANES), lambda i: (i, 0))
    return pl.pallas_call(
        _router_body,
        grid=(T_ALL // TM,),
        in_specs=[pl.BlockSpec((TM, D_MODEL), lambda i: (i, 0)),
                  pl.BlockSpec((None, 1, D_MODEL), lambda i: (l, 0, 0)),
                  _mod_spec(4, TM), _mod_spec(3, TM),
                  pl.BlockSpec((None, D_MODEL, N_EXPERTS), lambda i: (l, 0, 0)),
                  pl.BlockSpec((None, 1, N_EXPERTS), lambda i: (l, 0, 0))],
        out_specs=[pl.BlockSpec((TM, D_MODEL), lambda i: (i, 0)), tok_spec, tok_spec, tok_spec,
                   pl.BlockSpec((1, N_EXPERTS), lambda i: (0, 0))],
        out_shape=[jax.ShapeDtypeStruct((T_ALL, D_MODEL), F32), tok_out(jnp.int32), tok_out(F32), tok_out(jnp.int32),
                   jax.ShapeDtypeStruct((1, N_EXPERTS), jnp.int32)],
        scratch_shapes=[pltpu.VMEM((1, 2 * N_EXPERTS), F32)],
        compiler_params=_cparams("arbitrary"),
        name="router",
    )(x1, norm2.reshape(DEPTH, 1, D_MODEL), mod, mod, router_w, router_bias.reshape(DEPTH, 1, N_EXPERTS))


def _experts_body(be_ref, nv_ref, src_ref, nxt_ref, h_hbm, *rest):
    wgu_refs, wdn_refs = rest[:W_STREAMS], rest[W_STREAMS:2 * W_STREAMS]
    y_ref, wgu_b, wdn_b, xg0, xg1, sems = rest[2 * W_STREAMS:]
    i = pl.program_id(0)
    n_valid = nv_ref[0]

    def row_copy(idx_ref, t, buf, sem):
        return pltpu.make_async_copy(h_hbm.at[pl.ds(idx_ref[0, t], 1)], buf.at[pl.ds(t, 1)], sem)

    def all_rows(buf, sem):
        return pltpu.make_async_copy(h_hbm.at[pl.ds(0, TM_E)], buf, sem)

    @pl.when(i == 0)
    def _():
        def issue(t, carry):
            row_copy(src_ref, t, xg0, sems.at[0]).start()
            return carry

        lax.fori_loop(0, TM_E, issue, 0, unroll=8)

    @pl.when(i < n_valid)
    def _():
        prev = be_ref[jnp.maximum(i - 1, 0)]

        @pl.when(jnp.logical_or(i == 0, be_ref[i] != prev))
        def _():
            kg, kd = D_MODEL // W_STREAMS, EXPERT_FF // W_STREAMS
            for s in range(W_STREAMS):
                wgu_b[s * kg:(s + 1) * kg, :] = wgu_refs[s][...].astype(BF16)
                wdn_b[s * kd:(s + 1) * kd, :] = wdn_refs[s][...].astype(BF16)

        for parity, (cur, nxt) in enumerate(((xg0, xg1), (xg1, xg0))):
            @pl.when(i % 2 == parity)
            def _():
                all_rows(cur, sems.at[parity]).wait()
                for t in range(TM_E):
                    row_copy(nxt_ref, t, nxt, sems.at[1 - parity]).start()
                hmid = jnp.dot(cur[...].astype(BF16), wgu_b[...], preferred_element_type=F32)
                a, b = hmid[:, :EXPERT_FF], hmid[:, EXPERT_FF:]
                act = (a * jax.nn.sigmoid(a) * b).astype(BF16)
                y_ref[...] = jnp.dot(act, wdn_b[...], preferred_element_type=F32)

                @pl.when(i == n_valid - 1)
                def _():
                    all_rows(nxt, sems.at[1 - parity]).wait()

    @pl.when(i >= n_valid)
    def _():
        y_ref[...] = jnp.zeros_like(y_ref)


def routed_experts(h2, src, blk_e, n_valid, moe_w_gu, moe_w_dn, l):
    idx_spec = lambda off: pl.BlockSpec((None, 1, TM_E), lambda i, be, nv: (jnp.minimum(i + off, nv[0] - 1), 0, 0),
                                        memory_space=pltpu.SMEM)
    return pl.pallas_call(
        _experts_body,
        grid_spec=pltpu.PrefetchScalarGridSpec(
            num_scalar_prefetch=2,
            grid=(N_BLK_MAX,),
            in_specs=[idx_spec(0), idx_spec(1),
                      pl.BlockSpec(memory_space=pl.ANY)]
                     + [pl.BlockSpec((None, None, D_MODEL // W_STREAMS, 2 * EXPERT_FF),
                                     lambda i, be, nv, s=s: (l, be[i], s, 0)) for s in range(W_STREAMS)]
                     + [pl.BlockSpec((None, None, EXPERT_FF // W_STREAMS, D_MODEL),
                                     lambda i, be, nv, s=s: (l, be[i], s, 0)) for s in range(W_STREAMS)],
            out_specs=pl.BlockSpec((TM_E, D_MODEL), lambda i, be, nv: (i, 0)),
            scratch_shapes=[pltpu.VMEM((D_MODEL, 2 * EXPERT_FF), BF16), pltpu.VMEM((EXPERT_FF, D_MODEL), BF16),
                            pltpu.VMEM((TM_E, D_MODEL), F32), pltpu.VMEM((TM_E, D_MODEL), F32),
                            pltpu.SemaphoreType.DMA((2,))]),
        out_shape=jax.ShapeDtypeStruct((N_BLK_MAX * TM_E, D_MODEL), F32),
        compiler_params=_cparams("arbitrary"),
        name="routed_experts",
    )(blk_e, n_valid, src.reshape(N_BLK_MAX, 1, TM_E), src.reshape(N_BLK_MAX, 1, TM_E), h2,
      *([moe_w_gu] * W_STREAMS), *([moe_w_dn] * W_STREAMS))


def _gathered_rows(ybuf_hbm, rows, sem, n):
    return pltpu.make_async_copy(ybuf_hbm.at[pl.ds(0, n)], rows.at[pl.ds(0, n)], sem)


def _shared_body(dest_ref, gate_ref, h_ref, wgu_ref, wdn_ref, x_ref, g2_ref, ybuf_hbm, o_ref, wgu_b, wdn_b, rows, sem):
    @pl.when(pl.program_id(0) == 0)
    def _():
        wgu_b[...] = wgu_ref[...].astype(BF16)
        wdn_b[...] = wdn_ref[...].astype(BF16)

    for t in range(TM_SH):
        for k in range(TOP_K):
            src = dest_ref[0, t * TOP_K + k]
            pltpu.make_async_copy(ybuf_hbm.at[pl.ds(src, 1)], rows.at[pl.ds(k * TM_SH + t, 1)], sem).start()
    hmid = jnp.dot(h_ref[...].astype(BF16), wgu_b[...], preferred_element_type=F32)
    a, b = hmid[:, :SHARED_FF], hmid[:, SHARED_FF:]
    act = (a * jax.nn.sigmoid(a) * b).astype(BF16)
    y = jnp.dot(act, wdn_b[...], preferred_element_type=F32)
    _gathered_rows(ybuf_hbm, rows, sem, TOP_K * TM_SH).wait()
    gates = gate_ref[...]
    for k in range(TOP_K):
        y = y + gates[:, k:k + 1] * rows[k * TM_SH:(k + 1) * TM_SH, :]
    o_ref[...] = x_ref[...] + g2_ref[0] * y


def shared_combine(h2, sh_w_gu, sh_w_dn, ybuf, dest, gates, x1, mod, l):
    tile = pl.BlockSpec((TM_SH, D_MODEL), lambda i: (i, 0))
    once = pl.Buffered(1)
    n_tiles = T_ALL // TM_SH
    return pl.pallas_call(
        _shared_body,
        grid=(n_tiles,),
        in_specs=[pl.BlockSpec((None, 1, TM_SH * TOP_K), lambda i: (i, 0, 0), memory_space=pltpu.SMEM),
                  pl.BlockSpec((TM_SH, LANES), lambda i: (i, 0)),
                  tile,
                  pl.BlockSpec((None, D_MODEL, 2 * SHARED_FF), lambda i: (l, 0, 0), pipeline_mode=once),
                  pl.BlockSpec((None, SHARED_FF, D_MODEL), lambda i: (l, 0, 0), pipeline_mode=once),
                  tile, _mod_spec(5, TM_SH),
                  pl.BlockSpec(memory_space=pl.ANY)],
        out_specs=tile,
        out_shape=jax.ShapeDtypeStruct((T_ALL, D_MODEL), F32),
        scratch_shapes=[pltpu.VMEM((D_MODEL, 2 * SHARED_FF), BF16), pltpu.VMEM((SHARED_FF, D_MODEL), BF16),
                        pltpu.VMEM((TOP_K * TM_SH, D_MODEL), F32), pltpu.SemaphoreType.DMA(())],
        compiler_params=_cparams("arbitrary"),
        name="shared_combine",
    )(dest.reshape(n_tiles, 1, TM_SH * TOP_K), gates, h2, sh_w_gu, sh_w_dn, x1, mod, ybuf)


def _final_norm_body(x_ref, g_ref, o_ref):
    x = x_ref[...]
    o_ref[...] = x * lax.rsqrt(jnp.mean(x * x, axis=-1, keepdims=True) + EPS) * g_ref[...]


def final_rmsnorm(x, g):
    return pl.pallas_call(
        _final_norm_body,
        grid=(T_ALL // TM,),
        in_specs=[pl.BlockSpec((TM, D_MODEL), lambda i: (i, 0)), pl.BlockSpec((1, D_MODEL), lambda i: (0, 0))],
        out_specs=pl.BlockSpec((TM, D_MODEL), lambda i: (i, 0)),
        out_shape=jax.ShapeDtypeStruct(x.shape, x.dtype),
        compiler_params=_cparams("arbitrary"),
        name="final_norm",
    )(x, g.reshape(1, D_MODEL))


TQ_DIFF = 256
HALF = LANES // 2
NT_DIMS = (((1,), (1,)), ((), ()))


def rope_tables(L):
    pos = jnp.arange(L)
    d = jnp.arange(LANES) % HEAD_DIM
    p = jnp.where(d // (2 * ROPE_PAIRS) == 0, (pos // GRID_W)[:, None], (pos % GRID_W)[:, None]).astype(F32)
    inv = ROPE_BASE ** (-(d % ROPE_PAIRS).astype(F32) / ROPE_PAIRS)
    ang = p * inv
    return jnp.cos(ang), jnp.where((d // ROPE_PAIRS) % 2 == 0, -jnp.sin(ang), jnp.sin(ang))


def _rope(x, cos, sin_signed):
    lane = lax.broadcasted_iota(jnp.int32, x.shape, 1)
    is_x1 = (lane // ROPE_PAIRS) % 2 == 0
    partner = jnp.where(is_x1, pltpu.roll(x, LANES - ROPE_PAIRS, 1), pltpu.roll(x, ROPE_PAIRS, 1))
    return x * cos + partner * sin_signed


def _exp_and_inv_sum(s):
    e = jnp.exp(s - jnp.max(s, axis=-1, keepdims=True))
    return e, 1.0 / jnp.sum(e, axis=-1, keepdims=True)


def _diff_body(*refs, lk, ctx, lam_init):
    *ins, _aliased_dst, o_ref, kb, vb = refs
    if ctx:
        q_ref, k_ref, v_ref, dl_ref, g_ref, ck_ref, cv_ref, cq_ref, sq_ref, ckk_ref, skk_ref = ins
    else:
        q_ref, k_ref, v_ref, dl_ref, g_ref = ins

    @pl.when(pl.program_id(2) == 0)
    def _():
        k = k_ref[...]
        if ctx:
            k = _rope(k, ckk_ref[...], skk_ref[...])
            kb[lk:, :] = ck_ref[...].astype(BF16)
            vb[lk:, :] = cv_ref[...].astype(BF16)
        kb[0:lk, :] = k.astype(BF16)
        vb[0:lk, :] = v_ref[...].astype(BF16)

    q = q_ref[...] * DIFF_SCALE
    if ctx:
        q = _rope(q, cq_ref[...], sq_ref[...])
    lane = lax.broadcasted_iota(jnp.int32, q.shape, 1)
    q1 = jnp.where(lane < HALF, q, 0.0).astype(BF16)
    q2 = jnp.where(lane >= HALF, q, 0.0).astype(BF16)
    keys = kb[...]
    e1, inv1 = _exp_and_inv_sum(lax.dot_general(q1, keys, NT_DIMS, preferred_element_type=F32))
    e2, inv2 = _exp_and_inv_sum(lax.dot_general(q2, keys, NT_DIMS, preferred_element_type=F32))
    dl = dl_ref[...]
    lam = (jnp.exp(jnp.sum(dl[0:1] * dl[1:2], axis=-1, keepdims=True))
           - jnp.exp(jnp.sum(dl[2:3] * dl[3:4], axis=-1, keepdims=True)) + lam_init)
    o = jnp.dot((e1 * inv1 - e2 * (lam * inv2)).astype(BF16), vb[...], preferred_element_type=F32)
    o = o * lax.rsqrt(jnp.mean(o * o, axis=-1, keepdims=True) + EPS) * g_ref[...]
    o_ref[...] = o * (1.0 - lam_init)


def diff_attention(proj, dst, diff_lam, diff_subln, l, lam_init, cache_k=None, cache_v=None, tables=None):
    ctx = cache_k is not None
    nh = DIFF_HEADS
    if ctx:
        nb, lk, tq, row0 = DEC_BATCH, DEC_SEQ, TQ_DIFF, T_PROMPT
    else:
        nb, lk, tq, row0 = BATCH, SEQ, SEQ, 0
    in_specs = [pl.BlockSpec((tq, LANES), lambda b, h, qi: ((row0 + b * lk) // tq + qi, h)),
                pl.BlockSpec((lk, LANES), lambda b, h, qi: (row0 // lk + b, nh + h)),
                pl.BlockSpec((lk, LANES), lambda b, h, qi: (row0 // lk + b, 2 * nh + h)),
                pl.BlockSpec((None, 4, DIFF_QK), lambda b, h, qi: (l, 0, 0)),
                pl.BlockSpec((None, 1, DIFF_VD), lambda b, h, qi: (l, 0, 0))]
    args = [proj, proj, proj, diff_lam, diff_subln.reshape(DEPTH, 1, DIFF_VD)]
    if ctx:
        cos, sin = tables
        ctx_spec = pl.BlockSpec((None, None, PAST_LEN, LANES), lambda b, h, qi: (b, l, 0, h))
        q_tab = pl.BlockSpec((tq, LANES), lambda b, h, qi: (qi, 0))
        k_tab = pl.BlockSpec((lk, LANES), lambda b, h, qi: (0, 0))
        in_specs += [ctx_spec, ctx_spec, q_tab, q_tab, k_tab, k_tab]
        args += [cache_k.reshape(DEC_BATCH, DEPTH, PAST_LEN, GROUP_W), cache_v.reshape(DEC_BATCH, DEPTH, PAST_LEN, GROUP_W),
                 cos, sin, cos, sin]
    n_keys = lk + (PAST_LEN if ctx else 0)
    return pl.pallas_call(
        partial(_diff_body, lk=lk, ctx=ctx, lam_init=lam_init),
        grid=(nb, nh, lk // tq),
        in_specs=in_specs + [pl.BlockSpec(memory_space=pl.ANY)],
        out_specs=pl.BlockSpec((tq, LANES), lambda b, h, qi: ((row0 + b * lk) // tq + qi, h)),
        out_shape=jax.ShapeDtypeStruct((T_ALL, GROUP_W), F32),
        input_output_aliases={len(args): 0},
        scratch_shapes=[pltpu.VMEM((n_keys, LANES), BF16), pltpu.VMEM((n_keys, LANES), BF16)],
        compiler_params=_cparams("arbitrary", "arbitrary", "arbitrary"),
        name="diff_attention_ctx" if ctx else "diff_attention",
    )(*args, dst)


def _swa_body(*refs, lk, tq, banded):
    *ins, _aliased_dst, o_ref, kb, vb = refs
    if banded:
        q_ref, k_ref, v_ref, sk_ref, ck_ref, cv_ref, cq_ref, sq_ref, ckk_ref, skk_ref = ins
    else:
        q_ref, k_ref, v_ref, sk_ref = ins
    qi = pl.program_id(1)

    @pl.when(qi == 0)
    def _():
        k = k_ref[...]
        if banded:
            k = _rope(k, ckk_ref[...], skk_ref[...])
            zeros = jnp.zeros((WINDOW, LANES), BF16)
            kb[0:WINDOW, :] = zeros
            vb[0:WINDOW, :] = zeros
            kb[WINDOW + lk:2 * WINDOW + lk, :] = zeros
            vb[WINDOW + lk:2 * WINDOW + lk, :] = zeros
            kb[2 * WINDOW + lk:, :] = ck_ref[...].astype(BF16)
            vb[2 * WINDOW + lk:, :] = cv_ref[...].astype(BF16)
            kb[WINDOW:WINDOW + lk, :] = k.astype(BF16)
            vb[WINDOW:WINDOW + lk, :] = v_ref[...].astype(BF16)
        else:
            kb[...] = k.astype(BF16)
            vb[...] = v_ref[...].astype(BF16)

    lane = lax.broadcasted_iota(jnp.int32, (tq, LANES), 1)
    sinks = sk_ref[...]
    if banded:
        start = pl.multiple_of(qi * WINDOW, WINDOW)
        k_loc, v_loc = kb[pl.ds(start, 3 * WINDOW), :], vb[pl.ds(start, 3 * WINDOW), :]
        k_ctx, v_ctx = kb[2 * WINDOW + lk:, :], vb[2 * WINDOW + lk:, :]
        row = lax.broadcasted_iota(jnp.int32, (SWA_GROUP * tq, 3 * WINDOW), 0) % tq
        col = lax.broadcasted_iota(jnp.int32, (SWA_GROUP * tq, 3 * WINDOW), 1)
        kpos = (qi - 1) * WINDOW + col
        visible = (col >= row) & (col - row <= 2 * WINDOW) & (kpos >= 0) & (kpos < lk)
    else:
        k_loc, v_loc = kb[...], vb[...]
    outs = []
    for kh in range(SWA_KV_HEADS):
        in_half = (lane >= HALF) if kh else (lane < HALF)
        qs, sink_rows = [], []
        for g in range(SWA_GROUP):
            head = kh * SWA_GROUP + g
            t = q_ref[:, (head // 2) * LANES:(head // 2 + 1) * LANES] * SWA_SCALE
            if banded:
                t = _rope(t, cq_ref[...], sq_ref[...])
            if head % 2 != kh:
                t = pltpu.roll(t, HALF, 1)
            qs.append(jnp.where(in_half, t, 0.0).astype(BF16))
            sink_rows.append(jnp.broadcast_to(sinks[:, head:head + 1], (tq, 1)))
        qs = jnp.concatenate(qs, axis=0)
        sink = jnp.concatenate(sink_rows, axis=0)
        s_loc = lax.dot_general(qs, k_loc, NT_DIMS, preferred_element_type=F32)
        m = sink
        if banded:
            s_loc = jnp.where(visible, s_loc, NEG_INF)
            s_ctx = lax.dot_general(qs, k_ctx, NT_DIMS, preferred_element_type=F32)
            m = jnp.maximum(m, jnp.max(s_ctx, axis=-1, keepdims=True))
        m = jnp.maximum(m, jnp.max(s_loc, axis=-1, keepdims=True))
        e_loc = jnp.exp(s_loc - m)
        den = jnp.sum(e_loc, axis=-1, keepdims=True) + jnp.exp(sink - m)
        if banded:
            e_ctx = jnp.exp(s_ctx - m)
            den = den + jnp.sum(e_ctx, axis=-1, keepdims=True)
        inv = 1.0 / den
        o = jnp.dot((e_loc * inv).astype(BF16), v_loc, preferred_element_type=F32)
        if banded:
            o = o + jnp.dot((e_ctx * inv).astype(BF16), v_ctx, preferred_element_type=F32)
        outs.append(o)
    for t in range(SWA_HEADS // 2):
        kh, g0 = (2 * t) // SWA_GROUP, (2 * t) % SWA_GROUP
        a = outs[kh][g0 * tq:(g0 + 1) * tq]
        b = outs[kh][(g0 + 1) * tq:(g0 + 2) * tq]
        a = pltpu.roll(a, HALF, 1) if kh == 1 else a
        b = pltpu.roll(b, HALF, 1) if kh == 0 else b
        o_ref[:, t * LANES:(t + 1) * LANES] = jnp.where(lane < HALF, a, b)


def swa_attention(proj, dst, swa_sink, l, cache_k=None, cache_v=None, tables=None):
    banded = cache_k is not None
    q_col, k_col = 3, (4 * GROUP_W) // LANES
    if banded:
        nb, lk, tq, row0 = DEC_BATCH, DEC_SEQ, WINDOW, T_PROMPT
    else:
        nb, lk, tq, row0 = BATCH, SEQ, SEQ, 0
    in_specs = [pl.BlockSpec((tq, GROUP_W), lambda b, qi: ((row0 + b * lk) // tq + qi, q_col)),
                pl.BlockSpec((lk, LANES), lambda b, qi: (row0 // lk + b, k_col)),
                pl.BlockSpec((lk, LANES), lambda b, qi: (row0 // lk + b, k_col + 1)),
                pl.BlockSpec((None, 1, SWA_HEADS), lambda b, qi: (l, 0, 0))]
    args = [proj, proj, proj, swa_sink.reshape(DEPTH, 1, SWA_HEADS)]
    if banded:
        cos, sin = tables
        ctx_spec = pl.BlockSpec((None, None, PAST_LEN, LANES), lambda b, qi: (b, l, 0, 0))
        q_tab = pl.BlockSpec((tq, LANES), lambda b, qi: (qi, 0))
        k_tab = pl.BlockSpec((lk, LANES), lambda b, qi: (0, 0))
        in_specs += [ctx_spec, ctx_spec, q_tab, q_tab, k_tab, k_tab]
        args += [cache_k.reshape(DEC_BATCH, DEPTH, PAST_LEN, LANES), cache_v.reshape(DEC_BATCH, DEPTH, PAST_LEN, LANES),
                 cos, sin, cos, sin]
    n_keys = lk + 2 * WINDOW + PAST_LEN if banded else lk
    return pl.pallas_call(
        partial(_swa_body, lk=lk, tq=tq, banded=banded),
        grid=(nb, lk // tq),
        in_specs=in_specs + [pl.BlockSpec(memory_space=pl.ANY)],
        out_specs=pl.BlockSpec((tq, GROUP_W), lambda b, qi: ((row0 + b * lk) // tq + qi, 0)),
        out_shape=jax.ShapeDtypeStruct((T_ALL, GROUP_W), F32),
        input_output_aliases={len(args): 0},
        scratch_shapes=[pltpu.VMEM((n_keys, LANES), BF16), pltpu.VMEM((n_keys, LANES), BF16)],
        compiler_params=_cparams("arbitrary", "arbitrary"),
        name="swa_attention_banded" if banded else "swa_attention",
    )(*args, dst)


LRU_GROUPS = LRU_W // LANES


def lru_gate_params(wa, ba, wx, bx):
    def tile_blocks(w):
        w = w.reshape(LRU_GROUPS, 2, LRU_BD, LRU_BD)
        z = jnp.zeros((LRU_GROUPS, LRU_BD, LRU_BD), F32)
        return jnp.concatenate([jnp.concatenate([w[:, 0], z], axis=-1), jnp.concatenate([z, w[:, 1]], axis=-1)], axis=-2)

    w = jnp.concatenate([tile_blocks(wa[0]), tile_blocks(wx[0]), tile_blocks(wa[1]), tile_blocks(wx[1])], axis=-1)
    b = jnp.stack([ba[0], bx[0], ba[1], bx[1]], axis=0).reshape(4, LRU_GROUPS, LANES)
    return w, jnp.moveaxis(b, 0, 1).reshape(LRU_GROUPS, 1, 4 * LANES)


def _shift_rows(x, s, row, n):
    if s == 0:
        return x
    ok = (row >= -s) if s < 0 else (row < n - s)
    return jnp.where(ok, pltpu.roll(x, (-s) % n, 0), 0.0)


def _linear_scan(a, b, row, n, reverse):
    s = 1
    while s < n:
        ok = (row < n - s) if reverse else (row >= s)
        shift = (n - s) if reverse else s
        b = jnp.where(ok, a * pltpu.roll(b, shift, 0) + b, b)
        a = jnp.where(ok, a * pltpu.roll(a, shift, 0), a)
        s *= 2
    return b


def _lru_body(x_ref, lg_ref, cw_ref, w_ref, bias_ref, lam_ref, h0_ref, _aliased_dst, o_ref, st_ref, *, n):
    x = x_ref[...]
    row = lax.broadcasted_iota(jnp.int32, x.shape, 0)
    cw = cw_ref[...]
    xc = _shift_rows(x, -(LRU_CONV // 2), row, n) * cw[0:1]
    for k in range(1, LRU_CONV):
        xc = xc + _shift_rows(x, k - LRU_CONV // 2, row, n) * cw[k:k + 1]
    gates = jnp.dot(xc.astype(BF16), w_ref[...].astype(BF16), preferred_element_type=F32) + bias_ref[...]
    lam = lam_ref[...]
    log_sig = jnp.minimum(lam, 0.0) - jnp.log1p(jnp.exp(-jnp.abs(lam)))
    h0 = h0_ref[...]
    hs = []
    for d in range(2):
        r = jax.nn.sigmoid(gates[:, (2 * d) * LANES:(2 * d + 1) * LANES])
        i = jax.nn.sigmoid(gates[:, (2 * d + 1) * LANES:(2 * d + 2) * LANES])
        log_a = LRU_C * r * log_sig[d:d + 1]
        a = jnp.exp(log_a)
        th = jnp.tanh(log_a)
        b = jnp.sqrt(-2.0 * th / (1.0 - th)) * (i * xc)
        first = n - 1 if d else 0
        b = jnp.where(row == first, b + a * h0[d:d + 1], b)
        hs.append(_linear_scan(a, b, row, n, reverse=bool(d)))
    lg = lg_ref[...]
    gelu = 0.5 * lg * (1.0 + jnp.tanh(math.sqrt(2.0 / math.pi) * (lg + 0.044715 * (lg * lg * lg))))
    o_ref[...] = (hs[0] + hs[1]) * gelu
    st_ref[0:1, :] = hs[0][n - 1:n, :]
    st_ref[1:2, :] = hs[1][0:1, :]


def rglru(proj, dst, gate_w, gate_b, lru_conv, lru_lam, h0, l, h0_l, prompt):
    nb, n, row0 = (BATCH, SEQ, 0) if prompt else (DEC_BATCH, DEC_SEQ, T_PROMPT)
    x_col = (D_IN - 2 * LRU_W) // LANES
    return pl.pallas_call(
        partial(_lru_body, n=n),
        grid=(nb, LRU_GROUPS),
        in_specs=[pl.BlockSpec((n, LANES), lambda s, c: (row0 // n + s, x_col + c)),
                  pl.BlockSpec((n, LANES), lambda s, c: (row0 // n + s, x_col + LRU_GROUPS + c)),
                  pl.BlockSpec((None, LRU_CONV, LANES), lambda s, c: (l, 0, c)),
                  pl.BlockSpec((None, LANES, 4 * LANES), lambda s, c: (c, 0, 0)),
                  pl.BlockSpec((None, 1, 4 * LANES), lambda s, c: (c, 0, 0)),
                  pl.BlockSpec((None, 2, LANES), lambda s, c: (l, 0, c)),
                  pl.BlockSpec((None, None, 2, LANES), lambda s, c: (s, h0_l, 0, c)),
                  pl.BlockSpec(memory_space=pl.ANY)],
        out_specs=[pl.BlockSpec((n, LANES), lambda s, c: (row0 // n + s, c)),
                   pl.BlockSpec((None, 2, LANES), lambda s, c: (s, 0, c))],
        out_shape=[jax.ShapeDtypeStruct((T_ALL, LRU_W), F32), jax.ShapeDtypeStruct((nb, 2, LRU_W), F32)],
        input_output_aliases={7: 0},
        compiler_params=_cparams("arbitrary", "arbitrary"),
        name="rglru_prompt" if prompt else "rglru_sample",
    )(proj, proj, lru_conv, gate_w, gate_b, lru_lam, h0, dst)


HY_P = 256
HY_F = 2 * HY_P
HY_ACC_VREGS = 8
HY_FILTER_ROWS = 256


def _split_bf16(x):
    hi = x.astype(BF16)
    return hi, (x - hi.astype(F32)).astype(BF16)


def _dot3(c_hi, c_lo, d):
    d_hi, d_lo = _split_bf16(d)
    return (jnp.dot(c_hi, d_hi, preferred_element_type=F32) + jnp.dot(c_hi, d_lo, preferred_element_type=F32)
            + jnp.dot(c_lo, d_hi, preferred_element_type=F32))


def dft_constants():
    f = np.arange(HY_F)[:, None].astype(np.float64)
    ang = 2.0 * np.pi * f * np.arange(HY_F)[None, :] / HY_F
    c, s = np.cos(ang), np.sin(ang)
    ch, sh = c[:, :HY_P], s[:, :HY_P]
    fwd_data = np.block([[ch, sh], [-sh, ch]])
    fwd_filter = np.concatenate([c, -s], axis=0)
    inverse = np.block([[ch.T, -sh.T], [sh.T, ch.T]]) / HY_F
    out = []
    for m in (fwd_data, fwd_filter, inverse):
        hi = m.astype(np.float32).astype(BF16)
        lo = (m - hi.astype(np.float64)).astype(np.float32).astype(BF16)
        out += [jnp.asarray(hi), jnp.asarray(lo)]
    return out


def _filter_spectra_body(cur_ref, prev_ref, fh_ref, fl_ref, g_ref):
    row = lax.broadcasted_iota(jnp.int32, prev_ref.shape, 0)
    prev = jnp.where((row == 0) | (pl.program_id(1) == 0), 0.0, prev_ref[...])
    g_ref[...] = _dot3(fh_ref[...], fl_ref[...], jnp.concatenate([cur_ref[...], prev], axis=0))


def filter_spectra(filt, fk_hi, fk_lo):
    nq, n, _ = filt.shape
    nblk = n // HY_P
    const = pl.BlockSpec((2 * HY_F, HY_F), lambda q, m: (0, 0))
    return pl.pallas_call(
        _filter_spectra_body,
        grid=(nq, nblk),
        in_specs=[pl.BlockSpec((None, HY_P, HY_W), lambda q, m: (q, m, 0)),
                  pl.BlockSpec((None, HY_P, HY_W), lambda q, m: (q, jnp.maximum(m - 1, 0), 0)),
                  const, const],
        out_specs=pl.BlockSpec((None, None, 2 * HY_F, HY_W), lambda q, m: (q, m, 0, 0)),
        out_shape=jax.ShapeDtypeStruct((nq, nblk, 2 * HY_F, HY_W), F32),
        compiler_params=_cparams("arbitrary", "arbitrary"),
        name="hyena_filter_spectra",
    )(filt, filt, fk_hi, fk_lo)


def _hyena_body(z0_ref, z1_ref, x0_ref, x1_ref, cwz_ref, cwx_ref, skip_ref, gf_ref, gb_ref, fdh_ref, fdl_ref,
                gih_ref, gil_ref, o_ref, zs, xs, us, ys, *, n, conv_z):
    nblk = n // HY_P
    row = lax.broadcasted_iota(jnp.int32, z0_ref.shape, 0)

    def short_conv(x, cw):
        acc = _shift_rows(x, -(HY_SHORT // 2), row, n) * cw[0:1]
        for k in range(1, HY_SHORT):
            acc = acc + _shift_rows(x, k - HY_SHORT // 2, row, n) * cw[k:k + 1]
        return acc

    for b, (z_ref, x_ref) in enumerate(((z0_ref, x0_ref), (z1_ref, x1_ref))):
        zs[b] = short_conv(z_ref[...], cwz_ref[...]) if conv_z else z_ref[...]
        xs[b] = short_conv(x_ref[...], cwx_ref[...])

    def forward(j, carry):
        r0 = pl.multiple_of(j * HY_P, HY_P)
        d = jnp.concatenate([zs[0, pl.ds(r0, HY_P), :], zs[1, pl.ds(r0, HY_P), :]], axis=0)
        us[j] = _dot3(fdh_ref[...], fdl_ref[...], d)
        return carry

    lax.fori_loop(0, nblk, forward, 0)

    lanes = z0_ref.shape[1]
    fc = HY_ACC_VREGS * 8 * LANES // lanes

    def out_block(i, carry):
        for c0 in range(0, HY_F, fc):
            re, im = pl.ds(c0, fc), pl.ds(HY_F + c0, fc)

            def causal(j, acc):
                g_re, g_im, u_re, u_im = gf_ref[i - j, re, :], gf_ref[i - j, im, :], us[j, re, :], us[j, im, :]
                return acc[0] + (g_re * u_re - g_im * u_im), acc[1] + (g_re * u_im + g_im * u_re)

            def anticausal(j, acc):
                g_re, g_im, u_re, u_im = gb_ref[j - i, re, :], gb_ref[j - i, im, :], us[j, re, :], us[j, im, :]
                return acc[0] + (g_re * u_re + g_im * u_im), acc[1] + (g_re * u_im - g_im * u_re)

            zero = jnp.zeros((fc, lanes), F32)
            acc = lax.fori_loop(0, i + 1, causal, (zero, zero))
            acc = lax.fori_loop(i, nblk, anticausal, acc)
            ys[re, :] = acc[0]
            ys[im, :] = acc[1]
        y = _dot3(gih_ref[...], gil_ref[...], ys[...])
        r0 = pl.multiple_of(i * HY_P, HY_P)
        for b in range(2):
            conv = y[b * HY_P:(b + 1) * HY_P]
            o_ref[pl.ds(b * n + r0, HY_P), :] = xs[b, pl.ds(r0, HY_P), :] * (conv + skip_ref[...] * zs[b, pl.ds(r0, HY_P), :])
        return carry

    lax.fori_loop(0, nblk, out_block, 0)


def hyena_order(zsrc, z_col0, proj, x_col0, hy_conv, hy_skip, spectra, consts, l, order, prompt):
    fd_hi, fd_lo, _, _, gi_hi, gi_lo = consts
    if prompt:
        n, cw, npair, z_row0 = SEQ, 2 * LANES, BATCH // 2, 0
    else:
        n, cw, npair, z_row0 = DEC_SEQ, LANES, DEC_BATCH // 2, (T_PROMPT // DEC_SEQ if order == 0 else 0)
    x_row0 = 0 if prompt else T_PROMPT // DEC_SEQ
    ncg = HY_W // cw
    nblk = n // HY_P
    hy_col = sum(IN_SIZES[:6])
    once = pl.Buffered(1)
    seq = lambda row0, col0, b: pl.BlockSpec((n, cw), lambda p, c: (row0 + 2 * p + b, col0 // cw + c), pipeline_mode=once)
    taps = lambda col0: pl.BlockSpec((None, HY_SHORT, cw), lambda p, c: (l, 0, col0 // cw + c))
    spec_blk = lambda q: pl.BlockSpec((None, nblk, 2 * HY_F, cw), lambda p, c: (q, 0, 0, c), pipeline_mode=once)
    const = lambda shape: pl.BlockSpec(shape, lambda p, c: (0, 0))
    return pl.pallas_call(
        partial(_hyena_body, n=n, conv_z=(order == 0)),
        grid=(npair, ncg),
        in_specs=[seq(z_row0, z_col0, 0), seq(z_row0, z_col0, 1), seq(x_row0, x_col0, 0), seq(x_row0, x_col0, 1),
                  taps(0), taps(x_col0 - hy_col),
                  pl.BlockSpec((None, 1, cw), lambda p, c: (l * HY_ORDER + order, 0, c)),
                  spec_blk(2 * order), spec_blk(2 * order + 1),
                  const((2 * HY_F, HY_F)), const((2 * HY_F, HY_F)), const((HY_F, 2 * HY_F)), const((HY_F, 2 * HY_F))],
        out_specs=pl.BlockSpec((2 * n, cw), lambda p, c: (p, c)),
        out_shape=jax.ShapeDtypeStruct((npair * 2 * n, HY_W), F32),
        scratch_shapes=[pltpu.VMEM((2, n, cw), F32), pltpu.VMEM((2, n, cw), F32), pltpu.VMEM((nblk, 2 * HY_F, cw), F32),
                        pltpu.VMEM((2 * HY_F, cw), F32)],
        compiler_params=_cparams("arbitrary", "arbitrary"),
        name=f"hyena_{'prompt' if prompt else 'sample'}_order{order}",
    )(zsrc, zsrc, proj, proj, hy_conv, hy_conv, hy_skip.reshape(DEPTH * HY_ORDER, 1, HY_W), spectra, spectra,
      fd_hi, fd_lo, gi_hi, gi_lo)


def _hyena_filter_body(w1_ref, b1_ref, w2_ref, b2_ref, fr_ref, w3f_ref, w3b_ref, rf_ref, rb_ref, o_ref, h_scr, *, n):
    hp = lax.Precision.HIGHEST
    ch = min(n, HY_FILTER_ROWS)
    c = lax.broadcasted_iota(jnp.int32, (ch, LANES), 1)
    band = jnp.where(c <= HY_BANDS, c, c - HY_BANDS).astype(F32)
    fr = fr_ref[...]

    def positions(i):
        r0 = pl.multiple_of(i * ch, ch)
        return r0, (lax.broadcasted_iota(jnp.int32, (ch, LANES), 0) + r0).astype(F32) / n

    @pl.when((pl.program_id(0) == 0) & (pl.program_id(1) == 0))
    def _():
        def hidden(i, carry):
            r0, t = positions(i)
            ang = 2.0 * math.pi * t * band
            z = jnp.where(c == 0, t, jnp.where(c <= HY_BANDS, jnp.sin(ang), jnp.where(c < HY_POS_DIM, jnp.cos(ang), 0.0)))
            h = jnp.sin(fr[0:1] * (jnp.dot(z, w1_ref[...], preferred_element_type=F32, precision=hp) + b1_ref[...]))
            h = jnp.sin(fr[1:2] * (jnp.dot(h, w2_ref[...], preferred_element_type=F32, precision=hp) + b2_ref[...]))
            h_scr[pl.ds(r0, ch), :] = h
            return carry

        lax.fori_loop(0, n // ch, hidden, 0)

    def raw(i, sumsq):
        r0, t = positions(i)
        h = h_scr[pl.ds(r0, ch), :]
        tcol = t[:, 0:1]
        f = jnp.dot(h, w3f_ref[...], preferred_element_type=F32, precision=hp) * jnp.exp(-tcol * jnp.exp(rf_ref[...]))
        b = jnp.dot(h, w3b_ref[...], preferred_element_type=F32, precision=hp) * jnp.exp(-tcol * jnp.exp(rb_ref[...]))
        o_ref[0, pl.ds(r0, ch), :] = f
        o_ref[1, pl.ds(r0, ch), :] = b
        return sumsq + jnp.sum(f * f, axis=0, keepdims=True) + jnp.sum(b * b, axis=0, keepdims=True)

    scale = lax.rsqrt(lax.fori_loop(0, n // ch, raw, jnp.zeros((1, LANES), F32)) + EPS)

    def normalise(i, carry):
        r0 = pl.multiple_of(i * ch, ch)
        f, b = o_ref[0, pl.ds(r0, ch), :] * scale, o_ref[1, pl.ds(r0, ch), :] * scale
        first = lax.broadcasted_iota(jnp.int32, (ch, LANES), 0) + r0 == 0
        o_ref[0, pl.ds(r0, ch), :] = jnp.where(first, f + b, f)
        o_ref[1, pl.ds(r0, ch), :] = jnp.where(first, 0.0, b)
        return carry

    lax.fori_loop(0, n // ch, normalise, 0)


def hyena_filters_onesided(n, hy_w1, hy_b1, hy_w2, hy_b2, hy_w3, hy_freq, hy_log_rate, l):
    nt = HY_W // LANES
    w1 = jnp.pad(hy_w1, ((0, 0), (0, LANES - HY_POS_DIM), (0, 0)))
    vec = lambda a: a.reshape(DEPTH, 1, HY_HID)
    col = lambda d: (lambda o, w: (l, 0, (2 * o + d) * nt + w))
    whole = lambda shape: pl.BlockSpec((None,) + shape, lambda o, w: (l,) + (0,) * len(shape))
    return pl.pallas_call(
        partial(_hyena_filter_body, n=n),
        grid=(HY_ORDER, nt),
        in_specs=[whole((LANES, HY_HID)), whole((1, HY_HID)), whole((HY_HID, HY_HID)), whole((1, HY_HID)),
                  whole((2, HY_HID)),
                  pl.BlockSpec((None, HY_HID, LANES), col(0)), pl.BlockSpec((None, HY_HID, LANES), col(1)),
                  pl.BlockSpec((None, 1, LANES), col(0)), pl.BlockSpec((None, 1, LANES), col(1))],
        out_specs=pl.BlockSpec((None, 2, n, LANES), lambda o, w: (o, 0, 0, w)),
        out_shape=jax.ShapeDtypeStruct((HY_ORDER, 2, n, HY_W), F32),
        scratch_shapes=[pltpu.VMEM((n, HY_HID), F32)],
        compiler_params=_cparams("arbitrary", "arbitrary"),
        name="hyena_filters",
    )(w1, vec(hy_b1), hy_w2, vec(hy_b2), hy_freq, hy_w3, hy_w3,
      hy_log_rate.reshape(DEPTH, 1, 2 * HY_ORDER * HY_W), hy_log_rate.reshape(DEPTH, 1, 2 * HY_ORDER * HY_W)
      ).reshape(2 * HY_ORDER, n, HY_W)


def hyena_group(proj, filt4, hy_conv, hy_skip, consts, l, prompt):
    spectra = filter_spectra(filt4, consts[2], consts[3])
    hy_col = sum(IN_SIZES[:6])
    z = hyena_order(proj, hy_col, proj, hy_col + HY_W, hy_conv, hy_skip, spectra, consts, l, 0, prompt)
    return hyena_order(z, 0, proj, hy_col + 2 * HY_W, hy_conv, hy_skip, spectra, consts, l, 1, prompt)


def moe_layer(x1, mod, p, l):
    h2, idx, gates, rank, counts = router(x1, p['norm2'], mod, p['router_w'], p['router_bias'], l)
    idx, rank = idx[:, :TOP_K], rank[:, :TOP_K]
    nblk = (counts[0] + TM_E - 1) // TM_E
    blk_end = jnp.cumsum(nblk)
    row_start = (blk_end - nblk) * TM_E
    dest = row_start[idx] + rank
    n_valid = blk_end[-1:].astype(jnp.int32)
    blk = jnp.minimum(jnp.arange(N_BLK_MAX, dtype=jnp.int32), n_valid[0] - 1)
    blk_e = jnp.minimum(jnp.sum(blk_end[None, :] <= blk[:, None], axis=-1), N_EXPERTS - 1).astype(jnp.int32)
    tok = jnp.broadcast_to(jnp.arange(T_ALL, dtype=jnp.int32)[:, None], dest.shape)
    src = jnp.zeros((N_BLK_MAX * TM_E,), jnp.int32).at[dest.reshape(-1)].set(tok.reshape(-1))
    ybuf = routed_experts(h2, src, blk_e, n_valid, p['moe_w_gu'], p['moe_w_dn'], l)
    return shared_combine(h2, p['sh_w_gu'], p['sh_w_dn'], ybuf, dest, gates, x1, mod, l)


def kernel(x_prompt, x_sample, cache_diff_k, cache_diff_v, cache_swa_k, cache_swa_v, state_lru, c, c_ctx,
           w_mod, b_mod, norm1, norm2, w_in, w_out, diff_lam, diff_subln, swa_sink, hy_conv, hy_w1, hy_b1,
           hy_w2, hy_b2, hy_w3, hy_freq, hy_log_rate, hy_skip, lru_conv, lru_wa, lru_ba, lru_wx, lru_bx, lru_lam,
           router_w, router_bias, moe_w_gu, moe_w_dn, sh_w_gu, sh_w_dn, final_norm):
    x = jnp.concatenate([x_prompt.reshape(T_PROMPT, D_MODEL), x_sample.reshape(T_SAMPLE, D_MODEL)], axis=0)
    cond = jnp.concatenate([c_ctx[None, :], c, jnp.zeros((8 - N_COND, D_MODEL), F32)], axis=0)
    dk_l, dv_l, sk_l, sv_l, st_l = [], [], [], [], []
    tables = rope_tables(DEC_SEQ)
    consts = dft_constants()
    h0_prompt = jnp.zeros((BATCH, 1, 2, LRU_W), F32)
    col = np.cumsum((0,) + IN_SIZES).tolist()
    empty = lambda: jnp.zeros((T_ALL, GROUP_W), F32)
    for l in range(DEPTH):
        p = {'norm2': norm2, 'router_w': router_w, 'router_bias': router_bias, 'moe_w_gu': moe_w_gu,
             'moe_w_dn': moe_w_dn, 'sh_w_gu': sh_w_gu, 'sh_w_dn': sh_w_dn}
        lam_init = 0.8 - 0.6 * math.exp(-0.3 * l)
        mod = modulation(cond, w_mod, b_mod, l)[:N_COND].reshape(N_COND * 6, 1, D_MODEL)
        proj = in_proj(x, norm1, mod, w_in, l)
        proj_p = proj[:T_PROMPT].reshape(BATCH, SEQ, D_IN)
        dk_l.append(proj_p[..., col[1]:col[2]].reshape(BATCH, SEQ, DIFF_HEADS, 2, DIFF_QK))
        dv_l.append(proj_p[..., col[2]:col[3]].reshape(BATCH, SEQ, DIFF_HEADS, DIFF_VD))
        sk_l.append(proj_p[..., col[4]:col[5]].reshape(BATCH, SEQ, SWA_KV_HEADS, HEAD_DIM))
        sv_l.append(proj_p[..., col[5]:col[6]].reshape(BATCH, SEQ, SWA_KV_HEADS, HEAD_DIM))
        d_out = diff_attention(proj, empty(), diff_lam, diff_subln, l, lam_init)
        d_out = diff_attention(proj, d_out, diff_lam, diff_subln, l, lam_init, cache_diff_k, cache_diff_v, tables)
        s_out = swa_attention(proj, empty(), swa_sink, l)
        s_out = swa_attention(proj, s_out, swa_sink, l, cache_swa_k, cache_swa_v, tables)
        hy_args = (hy_w1, hy_b1, hy_w2, hy_b2, hy_w3, hy_freq, hy_log_rate, l)
        hy_out = jnp.concatenate(
            [hyena_group(proj, hyena_filters_onesided(SEQ, *hy_args), hy_conv, hy_skip, consts, l, True),
             hyena_group(proj, hyena_filters_onesided(DEC_SEQ, *hy_args), hy_conv, hy_skip, consts, l, False)], axis=0)
        gate_w, gate_b = lru_gate_params(lru_wa[l], lru_ba[l], lru_wx[l], lru_bx[l])
        lru_out, st = rglru(proj, empty(), gate_w, gate_b, lru_conv, lru_lam, h0_prompt, l, 0, True)
        lru_out, _ = rglru(proj, lru_out, gate_w, gate_b, lru_conv, lru_lam, state_lru, l, l, False)
        st_l.append(st)
        x1 = out_proj([d_out, s_out, hy_out, lru_out], w_out, x, mod, l)
        x = moe_layer(x1, mod, p, l)
    y = final_rmsnorm(x, final_norm)
    y_prompt = y[:T_PROMPT].reshape(BATCH, SEQ, D_MODEL)
    y_sample = y[T_PROMPT:].reshape(DEC_BATCH, DEC_SEQ, D_MODEL)
    return (y_prompt, y_sample, jnp.stack(dk_l, axis=1), jnp.stack(dv_l, axis=1), jnp.stack(sk_l, axis=1),
            jnp.stack(sv_l, axis=1), jnp.stack(st_l, axis=1))
```
